```python
import math
import jax, jax.numpy as jnp
from jax import lax
import numpy as np

D_MODEL = 1024
BATCH = 8
SEQ = 2048
DEPTH = 4
DEC_BATCH = 128
DEC_SEQ = 1
PAST_LEN = 16384
PAGE_SIZE = 128

N_MIXERS = 3
EPS = 1e-6
CONV_K = 4
SSD_INNER = 2 * D_MODEL
SSD_HEAD_DIM = 64
SSD_HEADS = SSD_INNER // SSD_HEAD_DIM
SSD_GROUPS = 8
SSD_STATE = 128
SSD_REP = SSD_HEADS // SSD_GROUPS
SSD_CONV_DIM = SSD_INNER + 2 * SSD_GROUPS * SSD_STATE
SSD_IN = SSD_INNER + SSD_CONV_DIM + SSD_HEADS
SSD_CHUNK = 128
GM_INNER = 2 * D_MODEL
GM_GROUPS = 8
GM_GROUP_DIM = GM_INNER // GM_GROUPS
GM_CHUNK = 128
ML_INNER = 2 * D_MODEL
ML_HEADS = 4
ML_HEAD_DIM = ML_INNER // ML_HEADS
ML_BLOCK = 4
ML_NBLK = ML_INNER // ML_BLOCK
ML_CHUNK = 128

kernel_name = "hybrid_ssd_gmlp_mlstm_step"

F32 = jnp.float32


def _rmsnorm(x, w):
    xf = x.astype(F32)
    y = xf * lax.rsqrt(jnp.mean(xf * xf, axis=-1, keepdims=True) + EPS)
    return (y * w.astype(F32)).astype(x.dtype)


def _causal_conv(xs, buf, w, b):
    T = xs.shape[1]
    full = jnp.concatenate([buf.astype(xs.dtype), xs], axis=1)
    y = b.astype(xs.dtype)
    for k in range(CONV_K):
        y = y + full[:, k:k + T] * w[k].astype(xs.dtype)
    return y, full[:, T:]


def _to_chunks(a, L):
    B, T = a.shape[:2]
    return jnp.moveaxis(a.reshape(B, T // L, L, *a.shape[2:]), 1, 0)


def _from_chunks(a):
    nC, B, L = a.shape[:3]
    return jnp.moveaxis(a, 0, 1).reshape(B, nC * L, *a.shape[3:])


def _ssd_scan(x, dt, A, Bm, Cm, h0):
    T = x.shape[1]
    L = math.gcd(T, SSD_CHUNK)
    mask = jnp.tril(jnp.ones((L, L), bool))[None, :, :, None, None]

    def step(h, inp):
        xc, dtc, Bc, Cc = inp
        cum = jnp.cumsum(dtc * A, axis=1)
        seg = cum[:, :, None] - cum[:, None, :]
        decay = jnp.exp(jnp.where(mask, seg, -jnp.inf))
        cb = jnp.einsum('blgn,bsgn->blsg', Cc, Bc)
        mix = cb[..., None] * decay * dtc[:, None]
        y = jnp.einsum('blsgr,bsgrp->blgrp', mix, xc)
        y = y + jnp.einsum('blgn,bgrpn->blgrp', Cc, h) * jnp.exp(cum)[..., None]
        w_end = jnp.exp(cum[:, -1:] - cum) * dtc
        h = jnp.exp(cum[:, -1])[..., None, None] * h + jnp.einsum('bsgr,bsgn,bsgrp->bgrpn', w_end, Bc, xc)
        return h, y

    h, ys = lax.scan(step, h0, (_to_chunks(x, L), _to_chunks(dt, L), _to_chunks(Bm, L), _to_chunks(Cm, L)))
    return _from_chunks(ys), h


def _mamba_layer(x, ssm0, conv0, norm_w, in_proj, conv_w, conv_b, dt_bias, A_log, D_skip, gnorm_w, out_proj):
    Bsz, T, _ = x.shape
    xn = _rmsnorm(x, norm_w)
    z, xbc, dt = jnp.split(xn @ in_proj, [SSD_INNER, SSD_INNER + SSD_CONV_DIM], axis=-1)
    xbc, conv_new = _causal_conv(xbc, conv0, conv_w, conv_b)
    xbc = jax.nn.silu(xbc)
    xs, Bm, Cm = jnp.split(xbc, [SSD_INNER, SSD_INNER + SSD_GROUPS * SSD_STATE], axis=-1)
    xs = xs.reshape(Bsz, T, SSD_GROUPS, SSD_REP, SSD_HEAD_DIM).astype(F32)
    Bm = Bm.reshape(Bsz, T, SSD_GROUPS, SSD_STATE).astype(F32)
    Cm = Cm.reshape(Bsz, T, SSD_GROUPS, SSD_STATE).astype(F32)
    dt = jax.nn.softplus(dt.astype(F32) + dt_bias.astype(F32)).reshape(Bsz, T, SSD_GROUPS, SSD_REP)
    A = -jnp.exp(A_log.astype(F32)).reshape(SSD_GROUPS, SSD_REP)
    h0 = ssm0.astype(F32).reshape(Bsz, SSD_GROUPS, SSD_REP, SSD_HEAD_DIM, SSD_STATE)
    y, h = _ssd_scan(xs, dt, A, Bm, Cm, h0)
    y = y + D_skip.astype(F32).reshape(SSD_GROUPS, SSD_REP)[..., None] * xs
    y = y.reshape(Bsz, T, SSD_GROUPS, SSD_INNER // SSD_GROUPS) * jax.nn.silu(
        z.astype(F32)).reshape(Bsz, T, SSD_GROUPS, SSD_INNER // SSD_GROUPS)
    y = y * lax.rsqrt(jnp.mean(y * y, axis=-1, keepdims=True) + EPS)
    y = y.reshape(Bsz, T, SSD_INNER) * gnorm_w.astype(F32)
    out = y.astype(x.dtype) @ out_proj
    return x + out, h.reshape(Bsz, SSD_HEADS, SSD_HEAD_DIM, SSD_STATE), conv_new


def _gmlp_layer(x, norm_w, in_proj, v_ln_w, v_ln_b, spatial_w, spatial_b, out_proj):
    Bsz, T, _ = x.shape
    xn = _rmsnorm(x, norm_w)
    u, v, z = jnp.split(xn @ in_proj, 3, axis=-1)
    vf = v.astype(F32)
    mu = jnp.mean(vf, axis=-1, keepdims=True)
    vc = vf - mu
    vn = vc * lax.rsqrt(jnp.mean(vc * vc, axis=-1, keepdims=True) + EPS) * v_ln_w.astype(F32) + v_ln_b.astype(F32)
    n_chunks = -(-T // GM_CHUNK)
    Tp = n_chunks * GM_CHUNK
    vp = jnp.pad(vn, ((0, 0), (0, Tp - T), (0, 0))).reshape(Bsz, n_chunks, GM_CHUNK, GM_GROUPS, GM_GROUP_DIM)
    ws = jnp.where(jnp.tril(jnp.ones((GM_CHUNK, GM_CHUNK), bool))[None], spatial_w.astype(F32), 0.0)
    mixed = jnp.einsum('gts,bcsgd->bctgd', ws, vp) + spatial_b.astype(F32).T[None, None, :, :, None]
    mixed = mixed.reshape(Bsz, Tp, GM_INNER)[:, :T]
    g = u.astype(F32) * mixed * jax.nn.silu(z.astype(F32))
    out = g.astype(x.dtype) @ out_proj
    return x + out, vn.astype(x.dtype)


def _blockdiag(x, w):
    xb = x.reshape(*x.shape[:-1], ML_NBLK, ML_BLOCK)
    return jnp.einsum('...nj,nji->...ni', xb, w.astype(x.dtype)).reshape(x.shape)


def _mlstm_scan(q, k, v, ig, lf, C0, n0, m0):
    T = q.shape[1]
    L = math.gcd(T, ML_CHUNK)
    mask = jnp.tril(jnp.ones((L, L), bool))

    def step(carry, inp):
        C, n, m = carry
        qc, kc, vc, igc, lfc = inp
        b = jnp.cumsum(lfc, axis=1).transpose(0, 2, 1)
        igt = igc.transpose(0, 2, 1)
        d = jnp.where(mask, b[:, :, :, None] - b[:, :, None, :] + igt[:, :, None, :], -jnp.inf)
        inter = b + m[:, :, None]
        mt = jnp.maximum(inter, jnp.max(d, axis=-1))
        w = jnp.exp(d - mt[..., None]) * jnp.einsum('blhd,bshd->bhls', qc, kc)
        wi = jnp.exp(inter - mt)
        num = wi[..., None] * jnp.einsum('bhvk,blhk->bhlv', C, qc) + jnp.einsum('bhls,bshv->bhlv', w, vc)
        den = wi * jnp.einsum('bhk,blhk->bhl', n, qc) + jnp.sum(w, axis=-1)
        den = jnp.maximum(jnp.abs(den), jnp.exp(-mt))
        h = (num / den[..., None]).transpose(0, 2, 1, 3)
        m_new = mt[:, :, -1]
        we = jnp.exp(b[:, :, -1:] - b + igt - m_new[..., None])
        dp = jnp.exp(inter[:, :, -1] - m_new)
        C = dp[..., None, None] * C + jnp.einsum('bhs,bshv,bshk->bhvk', we, vc, kc)
        n = dp[..., None] * n + jnp.einsum('bhs,bshk->bhk', we, kc)
        return (C, n, m_new), h

    xs = tuple(_to_chunks(a, L) for a in (q, k, v, ig, lf))
    (C, n, m), hs = lax.scan(step, (C0, n0, m0), xs)
    return _from_chunks(hs), C, n, m


def _mlstm_layer(x, C0, n0, m0, conv0, norm_w, in_proj, conv_w, conv_b, w_q, w_k, w_v, w_o, b_o,
                 w_if, b_if, mh_norm_w, skip, out_proj):
    Bsz, T, _ = x.shape
    xn = _rmsnorm(x, norm_w)
    xm, z = jnp.split(xn @ in_proj, 2, axis=-1)
    xc, conv_new = _causal_conv(xm, conv0, conv_w, conv_b)
    xc = jax.nn.silu(xc)
    q = _blockdiag(xc, w_q)
    k = _blockdiag(xc, w_k)
    v = _blockdiag(xm, w_v)
    gates = (jnp.concatenate([q, k, v], axis=-1) @ w_if + b_if).astype(F32)
    ig, fg = jnp.split(gates, 2, axis=-1)
    lf = jax.nn.log_sigmoid(fg)
    o = jax.nn.sigmoid((_blockdiag(xm, w_o) + b_o).astype(F32)).reshape(Bsz, T, ML_HEADS, ML_HEAD_DIM)
    shp = (Bsz, T, ML_HEADS, ML_HEAD_DIM)
    qh = q.astype(F32).reshape(shp)
    kh = k.astype(F32).reshape(shp) * (ML_HEAD_DIM ** -0.5)
    vh = v.astype(F32).reshape(shp)
    h, C, n, m = _mlstm_scan(qh, kh, vh, ig, lf, C0.astype(F32), n0.astype(F32), m0.astype(F32))
    h = o * h
    mu = jnp.mean(h, axis=-1, keepdims=True)
    hc = h - mu
    hn = (hc * lax.rsqrt(jnp.mean(hc * hc, axis=-1, keepdims=True) + EPS)).reshape(Bsz, T, ML_INNER)
    hn = hn * mh_norm_w.astype(F32) + skip.astype(F32) * xc.astype(F32)
    g = hn * jax.nn.silu(z.astype(F32))
    out = g.astype(x.dtype) @ out_proj
    return x + out, C, n, m, conv_new


def _normal(key, shape, scale):
    return jax.random.normal(key, shape, F32) * scale


def _gain(key, n):
    return 1.0 + 0.02 * jax.random.normal(key, (n,), F32)


def _mamba_params(key, p):
    ks = jax.random.split(key, 9)
    dt = jnp.exp(jax.random.uniform(ks[4], (SSD_HEADS,), F32, math.log(1e-3), math.log(1e-1)))
    return {
        p + "norm_w": _gain(ks[0], D_MODEL),
        p + "in_proj": _normal(ks[1], (D_MODEL, SSD_IN), D_MODEL ** -0.5),
        p + "conv_w": _normal(ks[2], (CONV_K, SSD_CONV_DIM), CONV_K ** -0.5),
        p + "conv_b": _normal(ks[3], (SSD_CONV_DIM,), 0.02),
        p + "dt_bias": dt + jnp.log(-jnp.expm1(-dt)),
        p + "A_log": jnp.log(jax.random.uniform(ks[5], (SSD_HEADS,), F32, 1.0, 16.0)),
        p + "D_skip": _gain(ks[6], SSD_HEADS),
        p + "gnorm_w": _gain(ks[7], SSD_INNER),
        p + "out_proj": _normal(ks[8], (SSD_INNER, D_MODEL), SSD_INNER ** -0.5),
    }


def _gmlp_params(key, p):
    ks = jax.random.split(key, 7)
    return {
        p + "norm_w": _gain(ks[0], D_MODEL),
        p + "in_proj": _normal(ks[1], (D_MODEL, 3 * GM_INNER), D_MODEL ** -0.5),
        p + "v_ln_w": _gain(ks[2], GM_INNER),
        p + "v_ln_b": _normal(ks[3], (GM_INNER,), 0.02),
        p + "spatial_w": _normal(ks[4], (GM_GROUPS, GM_CHUNK, GM_CHUNK), GM_CHUNK ** -0.5),
        p + "spatial_b": 1.0 + _normal(ks[5], (GM_GROUPS, GM_CHUNK), 0.02),
        p + "out_proj": _normal(ks[6], (GM_INNER, D_MODEL), GM_INNER ** -0.5),
    }


def _mlstm_params(key, p):
    ks = jax.random.split(key, 15)
    b_i = _normal(ks[10], (ML_HEADS,), 0.1)
    b_f = jnp.linspace(3.0, 6.0, ML_HEADS, dtype=F32) + _normal(ks[11], (ML_HEADS,), 0.1)
    return {
        p + "norm_w": _gain(ks[0], D_MODEL),
        p + "in_proj": _normal(ks[1], (D_MODEL, 2 * ML_INNER), D_MODEL ** -0.5),
        p + "conv_w": _normal(ks[2], (CONV_K, ML_INNER), CONV_K ** -0.5),
        p + "conv_b": _normal(ks[3], (ML_INNER,), 0.02),
        p + "w_q": _normal(ks[4], (ML_NBLK, ML_BLOCK, ML_BLOCK), ML_BLOCK ** -0.5),
        p + "w_k": _normal(ks[5], (ML_NBLK, ML_BLOCK, ML_BLOCK), ML_BLOCK ** -0.5),
        p + "w_v": _normal(ks[6], (ML_NBLK, ML_BLOCK, ML_BLOCK), ML_BLOCK ** -0.5),
        p + "w_o": _normal(ks[7], (ML_NBLK, ML_BLOCK, ML_BLOCK), ML_BLOCK ** -0.5),
        p + "b_o": _normal(ks[8], (ML_INNER,), 0.1),
        p + "w_if": _normal(ks[9], (3 * ML_INNER, 2 * ML_HEADS), (3 * ML_INNER) ** -0.5),
        p + "b_if": jnp.concatenate([b_i, b_f]),
        p + "mh_norm_w": _gain(ks[12], ML_INNER),
        p + "skip": _gain(ks[13], ML_INNER),
        p + "out_proj": _normal(ks[14], (ML_INNER, D_MODEL), ML_INNER ** -0.5),
    }


def setup_inputs(seed: int = 0) -> dict:
    key = jax.random.key(seed)
    ks = jax.random.split(key, 16)
    d = {
        "x_prompt": _normal(ks[0], (BATCH, SEQ, D_MODEL), 1.0),
        "x_sample": _normal(ks[1], (DEC_BATCH, DEC_SEQ, D_MODEL), 1.0),
        "state_l0_ssm": _normal(ks[2], (DEC_BATCH, SSD_HEADS, SSD_HEAD_DIM, SSD_STATE), 0.1),
        "state_l0_conv": _normal(ks[3], (DEC_BATCH, CONV_K - 1, SSD_CONV_DIM), 1.0),
        "state_l2_C": _normal(ks[4], (DEC_BATCH, ML_HEADS, ML_HEAD_DIM, ML_HEAD_DIM), 0.05),
        "state_l2_n": _normal(ks[5], (DEC_BATCH, ML_HEADS, ML_HEAD_DIM), 0.05),
        "state_l2_m": _normal(ks[6], (DEC_BATCH, ML_HEADS), 1.0),
        "state_l2_conv": _normal(ks[7], (DEC_BATCH, CONV_K - 1, ML_INNER), 1.0),
        "state_l3_ssm": _normal(ks[8], (DEC_BATCH, SSD_HEADS, SSD_HEAD_DIM, SSD_STATE), 0.1),
        "state_l3_conv": _normal(ks[9], (DEC_BATCH, CONV_K - 1, SSD_CONV_DIM), 1.0),
    }
    d.update(_mamba_params(ks[10], "l0_"))
    d.update(_gmlp_params(ks[11], "l1_"))
    d.update(_mlstm_params(ks[12], "l2_"))
    d.update(_mamba_params(ks[13], "l3_"))
    d["final_norm_w"] = _gain(ks[14], D_MODEL)
    return d


def reference(x_prompt, x_sample,
              state_l0_ssm, state_l0_conv,
              state_l2_C, state_l2_n, state_l2_m, state_l2_conv,
              state_l3_ssm, state_l3_conv,
              l0_norm_w, l0_in_proj, l0_conv_w, l0_conv_b, l0_dt_bias, l0_A_log, l0_D_skip, l0_gnorm_w, l0_out_proj,
              l1_norm_w, l1_in_proj, l1_v_ln_w, l1_v_ln_b, l1_spatial_w, l1_spatial_b, l1_out_proj,
              l2_norm_w, l2_in_proj, l2_conv_w, l2_conv_b, l2_w_q, l2_w_k, l2_w_v, l2_w_o, l2_b_o,
              l2_w_if, l2_b_if, l2_mh_norm_w, l2_skip, l2_out_proj,
              l3_norm_w, l3_in_proj, l3_conv_w, l3_conv_b, l3_dt_bias, l3_A_log, l3_D_skip, l3_gnorm_w, l3_out_proj,
              final_norm_w):
    Bp = x_prompt.shape[0]
    layer_params = [
        (l0_norm_w, l0_in_proj, l0_conv_w, l0_conv_b, l0_dt_bias, l0_A_log, l0_D_skip, l0_gnorm_w, l0_out_proj),
        (l1_norm_w, l1_in_proj, l1_v_ln_w, l1_v_ln_b, l1_spatial_w, l1_spatial_b, l1_out_proj),
        (l2_norm_w, l2_in_proj, l2_conv_w, l2_conv_b, l2_w_q, l2_w_k, l2_w_v, l2_w_o, l2_b_o,
         l2_w_if, l2_b_if, l2_mh_norm_w, l2_skip, l2_out_proj),
        (l3_norm_w, l3_in_proj, l3_conv_w, l3_conv_b, l3_dt_bias, l3_A_log, l3_D_skip, l3_gnorm_w, l3_out_proj),
    ]
    layer_states = [
        (state_l0_ssm, state_l0_conv),
        (),
        (state_l2_C, state_l2_n, state_l2_m, state_l2_conv),
        (state_l3_ssm, state_l3_conv),
    ]
    hp, hs = x_prompt, x_sample
    outs = []
    for i in range(DEPTH):
        kind = i % N_MIXERS
        prm = layer_params[i]
        st = layer_states[i]
        if kind == 0:
            ssm_zero = jnp.zeros((Bp, SSD_HEADS, SSD_HEAD_DIM, SSD_STATE), F32)
            conv_zero = jnp.zeros((Bp, CONV_K - 1, SSD_CONV_DIM), hp.dtype)
            hp, p_ssm, p_conv = _mamba_layer(hp, ssm_zero, conv_zero, *prm)
            hs, s_ssm, s_conv = _mamba_layer(hs, st[0], st[1], *prm)
            outs.append((p_ssm, p_conv, s_ssm, s_conv))
        elif kind == 1:
            hp, _ = _gmlp_layer(hp, *prm)
            hs, s_v = _gmlp_layer(hs, *prm)
            outs.append((s_v,))
        else:
            C_zero = jnp.zeros((Bp, ML_HEADS, ML_HEAD_DIM, ML_HEAD_DIM), F32)
            n_zero = jnp.zeros((Bp, ML_HEADS, ML_HEAD_DIM), F32)
            m_zero = jnp.zeros((Bp, ML_HEADS), F32)
            conv_zero = jnp.zeros((Bp, CONV_K - 1, ML_INNER), hp.dtype)
            hp, p_C, p_n, p_m, p_conv = _mlstm_layer(hp, C_zero, n_zero, m_zero, conv_zero, *prm)
            hs, s_C, s_n, s_m, s_conv = _mlstm_layer(hs, st[0], st[1], st[2], st[3], *prm)
            outs.append((p_C, p_n, p_m, p_conv, s_C, s_n, s_m, s_conv))
    y_prompt = _rmsnorm(hp, final_norm_w)
    y_sample = _rmsnorm(hs, final_norm_w)
    p0_ssm, p0_conv, s0_ssm, s0_conv = outs[0]
    (s1_v,) = outs[1]
    p2_C, p2_n, p2_m, p2_conv, s2_C, s2_n, s2_m, s2_conv = outs[2]
    p3_ssm, p3_conv, s3_ssm, s3_conv = outs[3]
    return (y_prompt, y_sample,
            p0_ssm, p0_conv, s0_ssm, s0_conv,
            s1_v,
            p2_C, p2_n, p2_m, p2_conv, s2_C, s2_n, s2_m, s2_conv,
            p3_ssm, p3_conv, s3_ssm, s3_conv)
```

```python
import functools
import math

import jax
import jax.numpy as jnp
from jax import lax
from jax.experimental import pallas as pl
from jax.experimental.pallas import tpu as pltpu

F32 = jnp.float32
BF16 = jnp.bfloat16
EPS = 1e-6
CONV_K = 4
CHUNK = 128
LANES = 128
SUBLANES = 8
SSD_HEAD_DIM = 64
SSD_STATE = 128
SSD_GROUPS = 8
ML_HEADS = 4
ML_BLOCK = 4
GM_GROUPS = 8
VMEM_LIMIT = 56 * 1024 * 1024
HI = lax.Precision.HIGHEST
NT_DIMS = (((1,), (1,)), ((), ()))


def _cparams(*sem):
    return pltpu.CompilerParams(dimension_semantics=sem, vmem_limit_bytes=VMEM_LIMIT)


def _sigmoid(x):
    return 1.0 / (1.0 + jnp.exp(-x))


def _silu(x):
    return x * _sigmoid(x)


def _softplus(x):
    return jnp.maximum(x, 0.0) + jnp.log1p(jnp.exp(-jnp.abs(x)))


def _bdot(a, b):
    return jnp.dot(a.astype(BF16), b.astype(BF16), preferred_element_type=F32)


def _bdot_nt(a, b):
    return lax.dot_general(a.astype(BF16), b.astype(BF16), NT_DIMS, preferred_element_type=F32)


def _hdot(a, b):
    return jnp.dot(a, b, precision=HI, preferred_element_type=F32)


def _rms(x, w):
    return x * lax.rsqrt(jnp.mean(x * x, axis=-1, keepdims=True) + EPS) * w


def _tril(n):
    r = lax.broadcasted_iota(jnp.int32, (n, n), 0)
    c = lax.broadcasted_iota(jnp.int32, (n, n), 1)
    return r >= c


def _expand_sel(n_in, width, n_out):
    r = lax.broadcasted_iota(jnp.int32, (n_in, n_out), 0)
    c = lax.broadcasted_iota(jnp.int32, (n_in, n_out), 1)
    return (c // width == r).astype(F32)


def _lane_segments(cols, width, rows):
    n = len(cols)
    lane = lax.broadcasted_iota(jnp.int32, (rows, n * width), 1)
    out = jnp.broadcast_to(cols[n - 1], (rows, n * width))
    for r in range(n - 2, -1, -1):
        out = jnp.where(lane < (r + 1) * width, jnp.broadcast_to(cols[r], (rows, n * width)), out)
    return out


def _row_segments(vals, height, lanes):
    n = len(vals)
    row = lax.broadcasted_iota(jnp.int32, (n * height, lanes), 0)
    out = jnp.broadcast_to(vals[n - 1], (n * height, lanes))
    for r in range(n - 2, -1, -1):
        out = jnp.where(row < (r + 1) * height, jnp.broadcast_to(vals[r], (n * height, lanes)), out)
    return out


def _norm_matmul_kernel(x_ref, nw_ref, w_ref, o_ref):
    o_ref[...] = _bdot(_rms(x_ref[...], nw_ref[...]), w_ref[...])


def _norm_matmul(x2d, norm_w, w_bf16):
    m, k = x2d.shape
    n = w_bf16.shape[1]
    tm = min(m, 256)
    return pl.pallas_call(
        _norm_matmul_kernel,
        name="norm_matmul",
        grid=(m // tm,),
        in_specs=[pl.BlockSpec((tm, k), lambda i: (i, 0)),
                  pl.BlockSpec((1, k), lambda i: (0, 0)),
                  pl.BlockSpec((k, n), lambda i: (0, 0))],
        out_specs=pl.BlockSpec((tm, n), lambda i: (i, 0)),
        out_shape=jax.ShapeDtypeStruct((m, n), F32),
        compiler_params=_cparams("parallel"),
    )(x2d, norm_w.reshape(1, k), w_bf16)


def _matmul_res_kernel(g_ref, w_ref, x_ref, o_ref):
    o_ref[...] = x_ref[...] + _bdot(g_ref[...], w_ref[...])


def _matmul_res(g2d, w_bf16, x2d):
    m, k = g2d.shape
    n = w_bf16.shape[1]
    tm = min(m, 512)
    return pl.pallas_call(
        _matmul_res_kernel,
        name="matmul_res",
        grid=(m // tm,),
        in_specs=[pl.BlockSpec((tm, k), lambda i: (i, 0)),
                  pl.BlockSpec((k, n), lambda i: (0, 0)),
                  pl.BlockSpec((tm, n), lambda i: (i, 0))],
        out_specs=pl.BlockSpec((tm, n), lambda i: (i, 0)),
        out_shape=jax.ShapeDtypeStruct((m, n), F32),
        compiler_params=_cparams("parallel"),
    )(g2d, w_bf16, x2d)


def _conv_carry(buf_ref, is_first):
    @pl.when(is_first)
    def _():
        buf_ref[0:SUBLANES, :] = jnp.zeros((SUBLANES, buf_ref.shape[1]), F32)

    @pl.when(jnp.logical_not(is_first))
    def _():
        buf_ref[0:SUBLANES, :] = buf_ref[CHUNK:CHUNK + SUBLANES, :]


def _conv_cols(buf_ref, cw_ref, cb_ref, cols):
    acc = cb_ref[:, cols]
    for k in range(CONV_K):
        off = SUBLANES - (CONV_K - 1) + k
        acc = acc + cw_ref[k:k + 1, cols] * buf_ref[off:off + CHUNK, cols]
    return acc


def _ssd_scan_kernel(xs_ref, bc_ref, dt_ref, cw_ref, cb_ref, dtb_ref, alog_ref, drep_ref,
                     y_ref, h_ref, buf_ref, act_ref, *, inner, heads):
    c = pl.program_id(1)
    L = CHUNK
    gn = SSD_GROUPS * SSD_STATE
    rep = heads // SSD_GROUPS
    gw = rep * SSD_HEAD_DIM

    _conv_carry(buf_ref, c == 0)

    @pl.when(c == 0)
    def _():
        h_ref[...] = jnp.zeros(h_ref.shape, F32)

    buf_ref[SUBLANES:SUBLANES + L, 0:inner] = xs_ref[...]
    buf_ref[SUBLANES:SUBLANES + L, inner:inner + 2 * gn] = bc_ref[...]
    cw = 512
    for j in range((inner + 2 * gn) // cw):
        cols = slice(j * cw, (j + 1) * cw)
        act_ref[:, cols] = _silu(_conv_cols(buf_ref, cw_ref, cb_ref, cols))

    lane = lax.broadcasted_iota(jnp.int32, (1, LANES), 1)
    dt = _softplus(dt_ref[...] + dtb_ref[...])
    a_neg = jnp.where(lane < heads, -jnp.exp(alog_ref[...]), 0.0)
    da = dt * a_neg
    tri = _tril(L)
    cum = _hdot(tri.astype(F32), da)
    cum_t = cum.T
    dt_t = dt.T
    ecum = jnp.exp(cum)
    cum_end = cum[L - 1:L, :]
    wend = jnp.exp(cum_end - cum) * dt
    dend = jnp.exp(cum_end)

    lane_g = lax.broadcasted_iota(jnp.int32, (L, gw), 1) // SSD_HEAD_DIM
    for g in range(SSD_GROUPS):
        bg = act_ref[:, inner + g * SSD_STATE:inner + (g + 1) * SSD_STATE].astype(BF16)
        cg = act_ref[:, inner + gn + g * SSD_STATE:inner + gn + (g + 1) * SSD_STATE].astype(BF16)
        cb = lax.dot_general(cg, bg, NT_DIMS, preferred_element_type=F32)
        xg = act_ref[:, g * gw:(g + 1) * gw]
        mixes = []
        for r in range(rep):
            h = g * rep + r
            seg = cum[:, h:h + 1] - cum_t[h:h + 1, :]
            decay = jnp.exp(jnp.where(tri, seg, -jnp.inf))
            mixes.append((cb * decay * dt_t[h:h + 1, :]).astype(BF16))
        mixcat = jnp.concatenate(mixes, axis=1)
        xblk = jnp.concatenate([jnp.where(lane_g == r, xg, 0.0) for r in range(rep)],
                               axis=0).astype(BF16)
        y = jnp.dot(mixcat, xblk, preferred_element_type=F32)
        rows = slice(g * gw, (g + 1) * gw)
        hg = h_ref[0, rows, :]
        yi = lax.dot_general(cg, hg.astype(BF16), NT_DIMS, preferred_element_type=F32)
        hs = [g * rep + r for r in range(rep)]
        e_g = _lane_segments([ecum[:, h:h + 1] for h in hs], SSD_HEAD_DIM, L)
        y_ref[:, rows] = y + yi * e_g + drep_ref[:, rows] * xg
        w_g = _lane_segments([wend[:, h:h + 1] for h in hs], SSD_HEAD_DIM, L)
        upd = jnp.dot((xg * w_g).T.astype(BF16), bg, preferred_element_type=F32)
        dec = _row_segments([dend[:, h:h + 1] for h in hs], SSD_HEAD_DIM, SSD_STATE)
        h_ref[0, rows, :] = dec * hg + upd


def _ssd_scan(zx, bsz, seq, inner, heads, conv_w, conv_b, dt_bias, a_log, d_skip):
    nc = seq // CHUNK
    gn = SSD_GROUPS * SSD_STATE
    conv_dim = inner + 2 * gn
    xs_blk = inner // inner
    bc_blk = (2 * inner) // (2 * gn)
    dt_blk = (inner + conv_dim) // LANES
    pad = LANES - heads
    dtb = jnp.pad(dt_bias, (0, pad)).reshape(1, LANES)
    alog = jnp.pad(a_log, (0, pad)).reshape(1, LANES)
    drep = jnp.repeat(d_skip, SSD_HEAD_DIM).reshape(1, inner)
    kern = functools.partial(_ssd_scan_kernel, inner=inner, heads=heads)
    row = lambda b, c: b * nc + c
    return pl.pallas_call(
        kern,
        name="ssd_scan",
        grid=(bsz, nc),
        in_specs=[pl.BlockSpec((CHUNK, inner), lambda b, c: (row(b, c), xs_blk)),
                  pl.BlockSpec((CHUNK, 2 * gn), lambda b, c: (row(b, c), bc_blk)),
                  pl.BlockSpec((CHUNK, LANES), lambda b, c: (row(b, c), dt_blk)),
                  pl.BlockSpec((CONV_K, conv_dim), lambda b, c: (0, 0)),
                  pl.BlockSpec((1, conv_dim), lambda b, c: (0, 0)),
                  pl.BlockSpec((1, LANES), lambda b, c: (0, 0)),
                  pl.BlockSpec((1, LANES), lambda b, c: (0, 0)),
                  pl.BlockSpec((1, inner), lambda b, c: (0, 0))],
        out_specs=[pl.BlockSpec((CHUNK, inner), lambda b, c: (row(b, c), 0)),
                   pl.BlockSpec((1, inner, SSD_STATE), lambda b, c: (b, 0, 0))],
        out_shape=[jax.ShapeDtypeStruct((bsz * seq, inner), F32),
                   jax.ShapeDtypeStruct((bsz, inner, SSD_STATE), F32)],
        scratch_shapes=[pltpu.VMEM((CHUNK + SUBLANES, conv_dim), F32),
                        pltpu.VMEM((CHUNK, conv_dim), F32)],
        compiler_params=_cparams("parallel", "arbitrary"),
    )(zx, zx, zx, conv_w, conv_b.reshape(1, conv_dim), dtb, alog, drep)


def _ssd_post_kernel(*refs, has_add, has_final):
    y_ref, z_ref, x_ref, gw_ref, w_ref = refs[:5]
    pos = 5
    add_ref = fin_ref = None
    if has_add:
        add_ref = refs[pos]
        pos += 1
    if has_final:
        fin_ref = refs[pos]
        pos += 1
    o_ref = refs[pos]
    inner = y_ref.shape[1]
    gwid = inner // SSD_GROUPS
    parts = []
    for g in range(SSD_GROUPS):
        cols = slice(g * gwid, (g + 1) * gwid)
        y = y_ref[:, cols]
        if has_add:
            y = y + add_ref[:, cols]
        y = y * _silu(z_ref[:, cols])
        y = y * lax.rsqrt(jnp.mean(y * y, axis=-1, keepdims=True) + EPS)
        parts.append((y * gw_ref[:, cols]).astype(BF16))
    out = x_ref[...] + jnp.dot(jnp.concatenate(parts, axis=1), w_ref[...],
                               preferred_element_type=F32)
    if has_final:
        out = _rms(out, fin_ref[...])
    o_ref[...] = out


def _ssd_post(y, zx, x2d, gnorm_w, w_out_bf16, add=None, final_w=None):
    m, inner = y.shape
    d = x2d.shape[1]
    tm = min(m, 512)
    ins = [y, zx, x2d, gnorm_w.reshape(1, inner), w_out_bf16]
    specs = [pl.BlockSpec((tm, inner), lambda i: (i, 0)),
             pl.BlockSpec((tm, inner), lambda i: (i, 0)),
             pl.BlockSpec((tm, d), lambda i: (i, 0)),
             pl.BlockSpec((1, inner), lambda i: (0, 0)),
             pl.BlockSpec((inner, d), lambda i: (0, 0))]
    if add is not None:
        ins.append(add)
        specs.append(pl.BlockSpec((tm, inner), lambda i: (i, 0)))
    if final_w is not None:
        ins.append(final_w.reshape(1, d))
        specs.append(pl.BlockSpec((1, d), lambda i: (0, 0)))
    kern = functools.partial(_ssd_post_kernel, has_add=add is not None, has_final=final_w is not None)
    return pl.pallas_call(
        kern,
        name="ssd_post",
        grid=(m // tm,),
        in_specs=specs,
        out_specs=pl.BlockSpec((tm, d), lambda i: (i, 0)),
        out_shape=jax.ShapeDtypeStruct((m, d), F32),
        compiler_params=_cparams("parallel"),
    )(*ins)


def _ssd_dec_pre_kernel(xbc_ref, dt_ref, c0_ref, cw_ref, cb_ref, dtb_ref, alog_ref, drep_ref,
                        cnew_ref, dtx_ref, darep_ref, b_ref, c_ref, dx_ref, *, inner, heads):
    gn = SSD_GROUPS * SSD_STATE
    cd = inner + 2 * gn
    xnew = xbc_ref[...]
    acc = cb_ref[...] + cw_ref[CONV_K - 1:CONV_K, :] * xnew
    for k in range(CONV_K - 1):
        acc = acc + cw_ref[k:k + 1, :] * c0_ref[:, k * cd:(k + 1) * cd]
    for k in range(CONV_K - 2):
        cnew_ref[:, k * cd:(k + 1) * cd] = c0_ref[:, (k + 1) * cd:(k + 2) * cd]
    cnew_ref[:, (CONV_K - 2) * cd:(CONV_K - 1) * cd] = xnew
    act = _silu(acc)
    xs = act[:, 0:inner]
    b_ref[...] = act[:, inner:inner + gn]
    c_ref[...] = act[:, inner + gn:inner + 2 * gn]
    lane = lax.broadcasted_iota(jnp.int32, (1, LANES), 1)
    dt = _softplus(dt_ref[...] + dtb_ref[...])
    a_neg = jnp.where(lane < heads, -jnp.exp(alog_ref[...]), 0.0)
    da = jnp.exp(dt * a_neg)
    sel = _expand_sel(LANES, SSD_HEAD_DIM, inner)
    dtx_ref[...] = _hdot(dt, sel) * xs
    darep_ref[...] = _hdot(da, sel)
    dx_ref[...] = drep_ref[...] * xs


def _ssd_dec_state_kernel(h_ref, dtxT_ref, daT_ref, b_ref, c_ref, hn_ref, yT_ref, *, bb, rep):
    i = pl.program_id(0)
    gw = rep * SSD_HEAD_DIM
    nb = yT_ref.shape[1]
    lane = lax.broadcasted_iota(jnp.int32, (gw, nb), 1)

    def body(bi, carry):
        bglob = i * bb + bi
        bmat = b_ref[bi]
        cmat = c_ref[bi]
        for g in range(SSD_GROUPS):
            rows = slice(g * gw, (g + 1) * gw)
            msk = lane == bglob
            dtx = jnp.sum(jnp.where(msk, dtxT_ref[rows, :], 0.0), axis=1, keepdims=True)
            dac = jnp.sum(jnp.where(msk, daT_ref[rows, :], 0.0), axis=1, keepdims=True)
            hnew = dac * h_ref[bi, rows, :] + dtx * bmat[g:g + 1, :]
            hn_ref[bi, rows, :] = hnew
            ycol = jnp.sum(hnew * cmat[g:g + 1, :], axis=1, keepdims=True)
            yT_ref[rows, :] = jnp.where(msk, ycol, yT_ref[rows, :])
        return carry

    lax.fori_loop(0, bb, body, 0)


def _ssd_decode(zx, x2d, ssm0, conv0, inner, heads, prm, w_out_bf16, final_w=None):
    norm_w, in_proj, conv_w, conv_b, dt_bias, a_log, d_skip, gnorm_w, out_proj = prm
    nb = zx.shape[0]
    gn = SSD_GROUPS * SSD_STATE
    cd = inner + 2 * gn
    pad = LANES - heads
    dtb = jnp.pad(dt_bias, (0, pad)).reshape(1, LANES)
    alog = jnp.pad(a_log, (0, pad)).reshape(1, LANES)
    drep = jnp.repeat(d_skip, SSD_HEAD_DIM).reshape(1, inner)
    full = lambda shape: pl.BlockSpec(shape, lambda i: tuple(0 for _ in shape))
    kern = functools.partial(_ssd_dec_pre_kernel, inner=inner, heads=heads)
    cnew, dtx, darep, bact, cact, dx = pl.pallas_call(
        kern,
        name="ssd_dec_pre",
        grid=(1,),
        in_specs=[full((nb, cd)), full((nb, LANES)), full((nb, (CONV_K - 1) * cd)), full((CONV_K, cd)), full((1, cd)),
                  full((1, LANES)), full((1, LANES)), full((1, inner))],
        out_specs=[full((nb, (CONV_K - 1) * cd)), full((nb, inner)), full((nb, inner)),
                   full((nb, gn)), full((nb, gn)), full((nb, inner))],
        out_shape=[jax.ShapeDtypeStruct((nb, (CONV_K - 1) * cd), F32),
                   jax.ShapeDtypeStruct((nb, inner), F32), jax.ShapeDtypeStruct((nb, inner), F32),
                   jax.ShapeDtypeStruct((nb, gn), F32), jax.ShapeDtypeStruct((nb, gn), F32),
                   jax.ShapeDtypeStruct((nb, inner), F32)],
        compiler_params=_cparams("arbitrary"),
    )(zx[:, inner:inner + cd], zx[:, inner + cd:inner + cd + LANES],
      conv0.reshape(nb, (CONV_K - 1) * cd), conv_w, conv_b.reshape(1, cd), dtb, alog, drep)

    bb = 4 if nb % 4 == 0 else 1
    rep = heads // SSD_GROUPS
    kern = functools.partial(_ssd_dec_state_kernel, bb=bb, rep=rep)
    hnew, y_t = pl.pallas_call(
        kern,
        name="ssd_dec_state",
        grid=(nb // bb,),
        in_specs=[pl.BlockSpec((bb, inner, SSD_STATE), lambda i: (i, 0, 0)),
                  pl.BlockSpec((inner, nb), lambda i: (0, 0)),
                  pl.BlockSpec((inner, nb), lambda i: (0, 0)),
                  pl.BlockSpec((bb, SSD_GROUPS, SSD_STATE), lambda i: (i, 0, 0)),
                  pl.BlockSpec((bb, SSD_GROUPS, SSD_STATE), lambda i: (i, 0, 0))],
        out_specs=[pl.BlockSpec((bb, inner, SSD_STATE), lambda i: (i, 0, 0)),
                   pl.BlockSpec((inner, nb), lambda i: (0, 0))],
        out_shape=[jax.ShapeDtypeStruct((nb, inner, SSD_STATE), F32),
                   jax.ShapeDtypeStruct((inner, nb), F32)],
        compiler_params=_cparams("arbitrary"),
    )(ssm0.reshape(nb, inner, SSD_STATE), dtx.T, darep.T,
      bact.reshape(nb, SSD_GROUPS, SSD_STATE), cact.reshape(nb, SSD_GROUPS, SSD_STATE))

    out = _ssd_post(y_t.T, zx, x2d, gnorm_w, w_out_bf16, add=dx, final_w=final_w)
    return out, hnew.reshape(nb, heads, SSD_HEAD_DIM, SSD_STATE), cnew.reshape(nb, CONV_K - 1, cd)


def _layernorm(v, w, b):
    mu = jnp.mean(v, axis=-1, keepdims=True)
    vc = v - mu
    return vc * lax.rsqrt(jnp.mean(vc * vc, axis=-1, keepdims=True) + EPS) * w + b


def _gmlp_kernel(u_ref, v_ref, z_ref, x_ref, lw_ref, lb_ref, ws_ref, sbT_ref, w_ref, o_ref, *, nck):
    inner = v_ref.shape[1]
    gd = inner // GM_GROUPS
    tri = _tril(CHUNK)
    for ck in range(nck):
        rows = slice(ck * CHUNK, (ck + 1) * CHUNK)
        vn = _layernorm(v_ref[rows, :], lw_ref[...], lb_ref[...])
        parts = []
        for g in range(GM_GROUPS):
            cols = slice(g * gd, (g + 1) * gd)
            wg = jnp.where(tri, ws_ref[g], 0.0)
            mixed = _bdot(wg, vn[:, cols]) + sbT_ref[:, g:g + 1]
            parts.append((u_ref[rows, cols] * mixed * _silu(z_ref[rows, cols])).astype(BF16))
        o_ref[rows, :] = x_ref[rows, :] + jnp.dot(jnp.concatenate(parts, axis=1), w_ref[...],
                                                  preferred_element_type=F32)


def _gmlp_prompt(uvz, x2d, inner, v_ln_w, v_ln_b, spatial_w, spatial_b, w_out_bf16):
    m, d = x2d.shape
    nck = 2 if (m // CHUNK) % 2 == 0 else 1
    tm = nck * CHUNK
    sb_t = jnp.pad(spatial_b.T, ((0, 0), (0, LANES - GM_GROUPS)))
    kern = functools.partial(_gmlp_kernel, nck=nck)
    return pl.pallas_call(
        kern,
        name="gmlp_prompt",
        grid=(m // tm,),
        in_specs=[pl.BlockSpec((tm, inner), lambda i: (i, 0)),
                  pl.BlockSpec((tm, inner), lambda i: (i, 1)),
                  pl.BlockSpec((tm, inner), lambda i: (i, 2)),
                  pl.BlockSpec((tm, d), lambda i: (i, 0)),
                  pl.BlockSpec((1, inner), lambda i: (0, 0)),
                  pl.BlockSpec((1, inner), lambda i: (0, 0)),
                  pl.BlockSpec((GM_GROUPS, CHUNK, CHUNK), lambda i: (0, 0, 0)),
                  pl.BlockSpec((CHUNK, LANES), lambda i: (0, 0)),
                  pl.BlockSpec((inner, d), lambda i: (0, 0))],
        out_specs=pl.BlockSpec((tm, d), lambda i: (i, 0)),
        out_shape=jax.ShapeDtypeStruct((m, d), F32),
        compiler_params=_cparams("parallel"),
    )(uvz, uvz, uvz, x2d, v_ln_w.reshape(1, inner), v_ln_b.reshape(1, inner), spatial_w, sb_t,
      w_out_bf16)


def _gmlp_dec_kernel(u_ref, v_ref, z_ref, x_ref, lw_ref, lb_ref, w00_ref, sb0_ref, w_ref,
                     o_ref, vn_ref):
    vn = _layernorm(v_ref[...], lw_ref[...], lb_ref[...])
    vn_ref[...] = vn
    mixed = w00_ref[...] * vn + sb0_ref[...]
    g = u_ref[...] * mixed * _silu(z_ref[...])
    o_ref[...] = x_ref[...] + _bdot(g, w_ref[...])


def _gmlp_decode(uvz, x2d, inner, v_ln_w, v_ln_b, spatial_w, spatial_b, w_out_bf16):
    nb, d = x2d.shape
    gd = inner // GM_GROUPS
    w00 = jnp.repeat(spatial_w[:, 0, 0], gd).reshape(1, inner)
    sb0 = jnp.repeat(spatial_b[:, 0], gd).reshape(1, inner)
    vec = pl.BlockSpec((1, inner), lambda i: (0, 0))
    return pl.pallas_call(
        _gmlp_dec_kernel,
        name="gmlp_dec",
        grid=(1,),
        in_specs=[pl.BlockSpec((nb, inner), lambda i: (0, 0)),
                  pl.BlockSpec((nb, inner), lambda i: (0, 1)),
                  pl.BlockSpec((nb, inner), lambda i: (0, 2)),
                  pl.BlockSpec((nb, d), lambda i: (0, 0)),
                  vec, vec, vec, vec,
                  pl.BlockSpec((inner, d), lambda i: (0, 0))],
        out_specs=[pl.BlockSpec((nb, d), lambda i: (0, 0)),
                   pl.BlockSpec((nb, inner), lambda i: (0, 0))],
        out_shape=[jax.ShapeDtypeStruct((nb, d), F32), jax.ShapeDtypeStruct((nb, inner), F32)],
        compiler_params=_cparams("arbitrary"),
    )(uvz, uvz, uvz, x2d, v_ln_w.reshape(1, inner), v_ln_b.reshape(1, inner), w00, sb0, w_out_bf16)


def _blockdiag_coefs(w):
    n = w.shape[0]
    rows = []
    for d in range(-(ML_BLOCK - 1), ML_BLOCK):
        cols = []
        for i in range(ML_BLOCK):
            j = i + d
            cols.append(w[:, j, i] if 0 <= j < ML_BLOCK else jnp.zeros((n,), w.dtype))
        rows.append(jnp.stack(cols, axis=1).reshape(n * ML_BLOCK))
    return jnp.stack(rows, axis=0)


def _blockdiag_apply(x, coef_refs):
    width = x.shape[1]
    outs = [None] * len(coef_refs)
    for di, d in enumerate(range(-(ML_BLOCK - 1), ML_BLOCK)):
        xs = x if d == 0 else pltpu.roll(x, (-d) % width, 1)
        for n, cref in enumerate(coef_refs):
            t = xs * cref[di:di + 1, :]
            outs[n] = t if outs[n] is None else outs[n] + t
    return outs


def _mlstm_scan_kernel(xm_ref, z_ref, cw_ref, cb_ref, cq_ref, ck_ref, cv_ref, co_ref, bo_ref,
                       wifq_ref, wifk_ref, wifv_ref, bif_ref, mhw_ref, skip_ref,
                       g_ref, cst_ref, nst_ref, mst_ref, buf_ref, *, heads):
    c = pl.program_id(1)
    L = CHUNK
    inner = xm_ref.shape[1]
    hd = inner // heads
    scale = hd ** -0.5

    _conv_carry(buf_ref, c == 0)

    @pl.when(c == 0)
    def _():
        cst_ref[...] = jnp.zeros(cst_ref.shape, F32)
        nst_ref[...] = jnp.zeros(nst_ref.shape, F32)
        mst_ref[...] = jnp.zeros(mst_ref.shape, F32)

    xm = xm_ref[...]
    buf_ref[SUBLANES:SUBLANES + L, :] = xm
    xc = _silu(_conv_cols(buf_ref, cw_ref, cb_ref, slice(0, inner)))
    q, k = _blockdiag_apply(xc, [cq_ref, ck_ref])
    v, o_pre = _blockdiag_apply(xm, [cv_ref, co_ref])
    gates = _bdot(q, wifq_ref[...]) + _bdot(k, wifk_ref[...]) + _bdot(v, wifv_ref[...]) + bif_ref[...]
    lf = -_softplus(-gates)
    tri = _tril(L)
    bt = _hdot(tri.astype(F32), lf)
    g_t = gates.T
    b_t = bt.T
    ksc = k * scale
    o_gate = _sigmoid(o_pre + bo_ref[...])
    mvec = mst_ref[0]
    for h in range(heads):
        cols = slice(h * hd, (h + 1) * hd)
        bcol = bt[:, heads + h:heads + h + 1]
        igcol = gates[:, h:h + 1]
        brow = b_t[heads + h:heads + h + 1, :]
        igrow = g_t[h:h + 1, :]
        m_old = mvec[:, h:h + 1]
        d = jnp.where(tri, bcol - brow + igrow, -jnp.inf)
        inter = bcol + m_old
        mt = jnp.maximum(inter, jnp.max(d, axis=1, keepdims=True))
        qh = q[:, cols].astype(BF16)
        kh = ksc[:, cols]
        khb = kh.astype(BF16)
        vh = v[:, cols]
        s = lax.dot_general(qh, khb, NT_DIMS, preferred_element_type=F32)
        w = jnp.exp(d - mt) * s
        wi = jnp.exp(inter - mt)
        cmat = cst_ref[0, h]
        cq = lax.dot_general(qh, cmat.astype(BF16), NT_DIMS, preferred_element_type=F32)
        num = wi * cq + _bdot(w, vh)
        nrow = nst_ref[0, h:h + 1, :]
        nq = jnp.sum(q[:, cols] * nrow, axis=1, keepdims=True)
        den = wi * nq + jnp.sum(w, axis=1, keepdims=True)
        den = jnp.maximum(jnp.abs(den), jnp.exp(-mt))
        hh = o_gate[:, cols] * (num / den)
        m_new = mt[L - 1:L, :]
        we = jnp.exp(bcol[L - 1:L, :] - bcol + igcol - m_new)
        dp = jnp.exp(inter[L - 1:L, :] - m_new)
        cst_ref[0, h] = dp * cmat + jnp.dot((we * vh).T.astype(BF16), khb,
                                            preferred_element_type=F32)
        nst_ref[0, h:h + 1, :] = dp * nrow + jnp.sum(we * kh, axis=0, keepdims=True)
        mst_ref[0, :, h:h + 1] = m_new
        mu = jnp.mean(hh, axis=1, keepdims=True)
        hc = hh - mu
        hn = hc * lax.rsqrt(jnp.mean(hc * hc, axis=1, keepdims=True) + EPS)
        hn = hn * mhw_ref[:, cols] + skip_ref[:, cols] * xc[:, cols]
        g_ref[:, cols] = hn * _silu(z_ref[:, cols])


def _mlstm_consts(prm, inner):
    (norm_w, in_proj, conv_w, conv_b, w_q, w_k, w_v, w_o, b_o, w_if, b_if, mh_norm_w, skip,
     out_proj) = prm
    heads = ML_HEADS
    padn = LANES - 2 * heads
    wif = jnp.pad(w_if, ((0, 0), (0, padn))).astype(BF16)
    bif = jnp.pad(b_if, (0, padn)).reshape(1, LANES)
    return dict(
        conv_w=conv_w, conv_b=conv_b.reshape(1, inner),
        cq=_blockdiag_coefs(w_q), ck=_blockdiag_coefs(w_k), cv=_blockdiag_coefs(w_v),
        co=_blockdiag_coefs(w_o), bo=b_o.reshape(1, inner),
        wifq=wif[0:inner], wifk=wif[inner:2 * inner], wifv=wif[2 * inner:3 * inner], bif=bif,
        mhw=mh_norm_w.reshape(1, inner), skip=skip.reshape(1, inner))


def _mlstm_scan(xmz, bsz, seq, inner, cst):
    nc = seq // CHUNK
    heads = ML_HEADS
    hd = inner // heads
    nco = 2 * ML_BLOCK - 1
    row = lambda b, c: b * nc + c
    c2 = lambda shape: pl.BlockSpec(shape, lambda b, c: (0, 0))
    kern = functools.partial(_mlstm_scan_kernel, heads=heads)
    return pl.pallas_call(
        kern,
        name="mlstm_scan",
        grid=(bsz, nc),
        in_specs=[pl.BlockSpec((CHUNK, inner), lambda b, c: (row(b, c), 0)),
                  pl.BlockSpec((CHUNK, inner), lambda b, c: (row(b, c), 1)),
                  c2((CONV_K, inner)), c2((1, inner)),
                  c2((nco, inner)), c2((nco, inner)), c2((nco, inner)), c2((nco, inner)),
                  c2((1, inner)),
                  c2((inner, LANES)), c2((inner, LANES)), c2((inner, LANES)), c2((1, LANES)),
                  c2((1, inner)), c2((1, inner))],
        out_specs=[pl.BlockSpec((CHUNK, inner), lambda b, c: (row(b, c), 0)),
                   pl.BlockSpec((1, heads, hd, hd), lambda b, c: (b, 0, 0, 0)),
                   pl.BlockSpec((1, heads, hd), lambda b, c: (b, 0, 0)),
                   pl.BlockSpec((1, 1, heads), lambda b, c: (b, 0, 0))],
        out_shape=[jax.ShapeDtypeStruct((bsz * seq, inner), F32),
                   jax.ShapeDtypeStruct((bsz, heads, hd, hd), F32),
                   jax.ShapeDtypeStruct((bsz, heads, hd), F32),
                   jax.ShapeDtypeStruct((bsz, 1, heads), F32)],
        scratch_shapes=[pltpu.VMEM((CHUNK + SUBLANES, inner), F32)],
        compiler_params=_cparams("parallel", "arbitrary"),
    )(xmz, xmz, cst["conv_w"], cst["conv_b"], cst["cq"], cst["ck"], cst["cv"], cst["co"], cst["bo"],
      cst["wifq"], cst["wifk"], cst["wifv"], cst["bif"], cst["mhw"], cst["skip"])


def _mlstm_dec_pre_kernel(xm_ref, c0_ref, n0_ref, m0_ref, cw_ref, cb_ref, cq_ref, ck_ref, cv_ref,
                          co_ref, bo_ref, wifq_ref, wifk_ref, wifv_ref, bif_ref,
                          cnew_ref, q_ref, k_ref, v_ref, og_ref, xc_ref, wev_ref, dprep_ref,
                          wrep_ref, denrep_ref, nnew_ref, mnew_ref, *, heads):
    inner = xm_ref.shape[1]
    nb = xm_ref.shape[0]
    hd = inner // heads
    xm = xm_ref[...]
    acc = cb_ref[...] + cw_ref[CONV_K - 1:CONV_K, :] * xm
    for kk in range(CONV_K - 1):
        acc = acc + cw_ref[kk:kk + 1, :] * c0_ref[:, kk * inner:(kk + 1) * inner]
    for kk in range(CONV_K - 2):
        cnew_ref[:, kk * inner:(kk + 1) * inner] = c0_ref[:, (kk + 1) * inner:(kk + 2) * inner]
    cnew_ref[:, (CONV_K - 2) * inner:(CONV_K - 1) * inner] = xm
    xc = _silu(acc)
    q, k = _blockdiag_apply(xc, [cq_ref, ck_ref])
    v, o_pre = _blockdiag_apply(xm, [cv_ref, co_ref])
    gates = _bdot(q, wifq_ref[...]) + _bdot(k, wifk_ref[...]) + _bdot(v, wifv_ref[...]) + bif_ref[...]
    ksc = k * (hd ** -0.5)
    lane = lax.broadcasted_iota(jnp.int32, (nb, LANES), 1)
    ig = gates
    lf = pltpu.roll(-_softplus(-gates), LANES - heads, 1)
    n0 = n0_ref[...]
    qk = jnp.zeros((nb, LANES), F32)
    nq = jnp.zeros((nb, LANES), F32)
    for h in range(heads):
        cols = slice(h * hd, (h + 1) * hd)
        qk = jnp.where(lane == h, jnp.sum(q[:, cols] * ksc[:, cols], axis=1, keepdims=True), qk)
        nq = jnp.where(lane == h, jnp.sum(q[:, cols] * n0[:, cols], axis=1, keepdims=True), nq)
    inter = lf + m0_ref[...]
    mt = jnp.maximum(inter, ig)
    wi = jnp.exp(inter - mt)
    we = jnp.exp(ig - mt)
    w = we * qk
    den = jnp.maximum(jnp.abs(wi * nq + w), jnp.exp(-mt))
    mnew_ref[...] = mt
    sel = _expand_sel(LANES, hd, inner)
    wi_rep = _hdot(wi, sel)
    we_rep = _hdot(we, sel)
    dprep_ref[...] = wi_rep
    wrep_ref[...] = _hdot(w, sel)
    denrep_ref[...] = _hdot(den, sel)
    wev_ref[...] = we_rep * v
    nnew_ref[...] = wi_rep * n0 + we_rep * ksc
    q_ref[...] = q
    k_ref[...] = ksc
    v_ref[...] = v
    og_ref[...] = _sigmoid(o_pre + bo_ref[...])
    xc_ref[...] = xc


def _mlstm_dec_state_kernel(c_ref, q_ref, k_ref, dp_ref, wevT_ref, cn_ref, cqT_ref, *, heads):
    b = pl.program_id(0)
    hd = c_ref.shape[2]
    nb = cqT_ref.shape[1]
    lane = lax.broadcasted_iota(jnp.int32, (hd, nb), 1)
    msk = lane == b
    for h in range(heads):
        rows = slice(h * hd, (h + 1) * hd)
        cmat = c_ref[0, h]
        cqcol = jnp.sum(cmat * q_ref[0, h:h + 1, :], axis=1, keepdims=True)
        wev = jnp.sum(jnp.where(msk, wevT_ref[rows, :], 0.0), axis=1, keepdims=True)
        cn_ref[0, h] = dp_ref[0, h:h + 1, :] * cmat + wev * k_ref[0, h:h + 1, :]
        cqT_ref[rows, :] = jnp.where(msk, cqcol, cqT_ref[rows, :])


def _mlstm_dec_post_kernel(cq_ref, v_ref, dp_ref, w_ref, den_ref, og_ref, xc_ref, z_ref,
                           mhw_ref, skip_ref, g_ref, *, heads):
    inner = cq_ref.shape[1]
    hd = inner // heads
    for h in range(heads):
        cols = slice(h * hd, (h + 1) * hd)
        num = dp_ref[:, cols] * cq_ref[:, cols] + w_ref[:, cols] * v_ref[:, cols]
        hh = og_ref[:, cols] * (num / den_ref[:, cols])
        mu = jnp.mean(hh, axis=1, keepdims=True)
        hc = hh - mu
        hn = hc * lax.rsqrt(jnp.mean(hc * hc, axis=1, keepdims=True) + EPS)
        hn = hn * mhw_ref[:, cols] + skip_ref[:, cols] * xc_ref[:, cols]
        g_ref[:, cols] = hn * _silu(z_ref[:, cols])


def _mlstm_decode(xmz, c0, n0, m0, conv0, inner, cst):
    nb = xmz.shape[0]
    heads = ML_HEADS
    hd = inner // heads
    nco = 2 * ML_BLOCK - 1
    full = lambda shape: pl.BlockSpec(shape, lambda i: tuple(0 for _ in shape))
    m0p = jnp.pad(m0, ((0, 0), (0, LANES - heads)))
    kern = functools.partial(_mlstm_dec_pre_kernel, heads=heads)
    big = jax.ShapeDtypeStruct((nb, inner), F32)
    outs = pl.pallas_call(
        kern,
        name="mlstm_dec_pre",
        grid=(1,),
        in_specs=[pl.BlockSpec((nb, inner), lambda i: (0, 0)),
                  full((nb, (CONV_K - 1) * inner)), full((nb, inner)), full((nb, LANES)),
                  full((CONV_K, inner)), full((1, inner)),
                  full((nco, inner)), full((nco, inner)), full((nco, inner)), full((nco, inner)),
                  full((1, inner)),
                  full((inner, LANES)), full((inner, LANES)), full((inner, LANES)), full((1, LANES))],
        out_specs=[full((nb, (CONV_K - 1) * inner))] + [full((nb, inner))] * 10 + [full((nb, LANES))],
        out_shape=[jax.ShapeDtypeStruct((nb, (CONV_K - 1) * inner), F32)] + [big] * 10
                  + [jax.ShapeDtypeStruct((nb, LANES), F32)],
        compiler_params=_cparams("arbitrary"),
    )(xmz, conv0.reshape(nb, (CONV_K - 1) * inner), n0.reshape(nb, inner), m0p,
      cst["conv_w"], cst["conv_b"], cst["cq"], cst["ck"], cst["cv"], cst["co"], cst["bo"],
      cst["wifq"], cst["wifk"], cst["wifv"], cst["bif"])
    cnew, q, ksc, v, og, xc, wev, dprep, wrep, denrep, nnew, mnew = outs

    kern = functools.partial(_mlstm_dec_state_kernel, heads=heads)
    h3 = lambda: pl.BlockSpec((1, heads, hd), lambda i: (i, 0, 0))
    c_new, cq_t = pl.pallas_call(
        kern,
        name="mlstm_dec_state",
        grid=(nb,),
        in_specs=[pl.BlockSpec((1, heads, hd, hd), lambda i: (i, 0, 0, 0)),
                  h3(), h3(), h3(),
                  pl.BlockSpec((inner, nb), lambda i: (0, 0))],
        out_specs=[pl.BlockSpec((1, heads, hd, hd), lambda i: (i, 0, 0, 0)),
                   pl.BlockSpec((inner, nb), lambda i: (0, 0))],
        out_shape=[jax.ShapeDtypeStruct((nb, heads, hd, hd), F32),
                   jax.ShapeDtypeStruct((inner, nb), F32)],
        compiler_params=_cparams("arbitrary"),
    )(c0, q.reshape(nb, heads, hd), ksc.reshape(nb, heads, hd), dprep.reshape(nb, heads, hd), wev.T)

    kern = functools.partial(_mlstm_dec_post_kernel, heads=heads)
    g = pl.pallas_call(
        kern,
        name="mlstm_dec_post",
        grid=(1,),
        in_specs=[full((nb, inner))] * 7
                 + [pl.BlockSpec((nb, inner), lambda i: (0, 1)), full((1, inner)), full((1, inner))],
        out_specs=full((nb, inner)),
        out_shape=big,
        compiler_params=_cparams("arbitrary"),
    )(cq_t.T, v, dprep, wrep, denrep, og, xc, xmz, cst["mhw"], cst["skip"])
    return (g, c_new, nnew.reshape(nb, heads, hd), mnew[:, 0:heads],
            cnew.reshape(nb, CONV_K - 1, inner))


def _ssd_in_weights(in_proj, inner, conv_dim):
    ncol = inner + conv_dim
    pad = LANES - (in_proj.shape[1] - ncol)
    return jnp.pad(in_proj, ((0, 0), (0, pad))).astype(BF16)


def _ssd_layer(xp, xs, ssm0, conv0, prm, final_w=None):
    norm_w, in_proj, conv_w, conv_b, dt_bias, a_log, d_skip, gnorm_w, out_proj = prm
    bsz, seq, d = xp.shape
    nb = xs.shape[0]
    inner = out_proj.shape[0]
    heads = a_log.shape[0]
    conv_dim = conv_w.shape[1]
    w_in = _ssd_in_weights(in_proj, inner, conv_dim)
    w_out = out_proj.astype(BF16)
    xp2 = xp.reshape(bsz * seq, d)
    xs2 = xs.reshape(nb, d)

    zx_p = _norm_matmul(xp2, norm_w, w_in)
    y_p, h_p = _ssd_scan(zx_p, bsz, seq, inner, heads, conv_w, conv_b, dt_bias, a_log, d_skip)
    out_p = _ssd_post(y_p, zx_p, xp2, gnorm_w, w_out, final_w=final_w)
    conv_p = zx_p.reshape(bsz, seq, -1)[:, seq - (CONV_K - 1):, inner:inner + conv_dim]

    zx_s = _norm_matmul(xs2, norm_w, w_in)
    out_s, h_s, conv_s = _ssd_decode(zx_s, xs2, ssm0, conv0, inner, heads, prm, w_out, final_w=final_w)
    return (out_p.reshape(bsz, seq, d), out_s.reshape(nb, 1, d),
            h_p.reshape(bsz, heads, SSD_HEAD_DIM, SSD_STATE), conv_p, h_s, conv_s)


def _gmlp_layer(xp, xs, prm):
    norm_w, in_proj, v_ln_w, v_ln_b, spatial_w, spatial_b, out_proj = prm
    bsz, seq, d = xp.shape
    nb = xs.shape[0]
    inner = out_proj.shape[0]
    w_in = in_proj.astype(BF16)
    w_out = out_proj.astype(BF16)
    xp2 = xp.reshape(bsz * seq, d)
    xs2 = xs.reshape(nb, d)
    uvz_p = _norm_matmul(xp2, norm_w, w_in)
    out_p = _gmlp_prompt(uvz_p, xp2, inner, v_ln_w, v_ln_b, spatial_w, spatial_b, w_out)
    uvz_s = _norm_matmul(xs2, norm_w, w_in)
    out_s, vn_s = _gmlp_decode(uvz_s, xs2, inner, v_ln_w, v_ln_b, spatial_w, spatial_b, w_out)
    return out_p.reshape(bsz, seq, d), out_s.reshape(nb, 1, d), vn_s.reshape(nb, 1, inner)


def _mlstm_layer(xp, xs, c0, n0, m0, conv0, prm):
    norm_w, in_proj = prm[0], prm[1]
    out_proj = prm[-1]
    bsz, seq, d = xp.shape
    nb = xs.shape[0]
    inner = out_proj.shape[0]
    w_in = in_proj.astype(BF16)
    w_out = out_proj.astype(BF16)
    cst = _mlstm_consts(prm, inner)
    xp2 = xp.reshape(bsz * seq, d)
    xs2 = xs.reshape(nb, d)

    xmz_p = _norm_matmul(xp2, norm_w, w_in)
    g_p, c_p, n_p, m_p = _mlstm_scan(xmz_p, bsz, seq, inner, cst)
    out_p = _matmul_res(g_p, w_out, xp2)
    conv_p = xmz_p.reshape(bsz, seq, -1)[:, seq - (CONV_K - 1):, 0:inner]

    xmz_s = _norm_matmul(xs2, norm_w, w_in)
    g_s, c_s, n_s, m_s, conv_s = _mlstm_decode(xmz_s, c0, n0, m0, conv0, inner, cst)
    out_s = _matmul_res(g_s, w_out, xs2)
    return (out_p.reshape(bsz, seq, d), out_s.reshape(nb, 1, d),
            c_p, n_p, m_p.reshape(bsz, ML_HEADS), conv_p, c_s, n_s, m_s, conv_s)


def kernel(x_prompt, x_sample, state_l0_ssm, state_l0_conv, state_l2_C, state_l2_n, state_l2_m, state_l2_conv, state_l3_ssm, state_l3_conv, l0_norm_w, l0_in_proj, l0_conv_w, l0_conv_b, l0_dt_bias, l0_A_log, l0_D_skip, l0_gnorm_w, l0_out_proj, l1_norm_w, l1_in_proj, l1_v_ln_w, l1_v_ln_b, l1_spatial_w, l1_spatial_b, l1_out_proj, l2_norm_w, l2_in_proj, l2_conv_w, l2_conv_b, l2_w_q, l2_w_k, l2_w_v, l2_w_o, l2_b_o, l2_w_if, l2_b_if, l2_mh_norm_w, l2_skip, l2_out_proj, l3_norm_w, l3_in_proj, l3_conv_w, l3_conv_b, l3_dt_bias, l3_A_log, l3_D_skip, l3_gnorm_w, l3_out_proj, final_norm_w):
    p0 = (l0_norm_w, l0_in_proj, l0_conv_w, l0_conv_b, l0_dt_bias, l0_A_log, l0_D_skip, l0_gnorm_w, l0_out_proj)
    p1 = (l1_norm_w, l1_in_proj, l1_v_ln_w, l1_v_ln_b, l1_spatial_w, l1_spatial_b, l1_out_proj)
    p2 = (l2_norm_w, l2_in_proj, l2_conv_w, l2_conv_b, l2_w_q, l2_w_k, l2_w_v, l2_w_o, l2_b_o,
          l2_w_if, l2_b_if, l2_mh_norm_w, l2_skip, l2_out_proj)
    p3 = (l3_norm_w, l3_in_proj, l3_conv_w, l3_conv_b, l3_dt_bias, l3_A_log, l3_D_skip, l3_gnorm_w, l3_out_proj)

    hp, hs, p0_ssm, p0_conv, s0_ssm, s0_conv = _ssd_layer(x_prompt, x_sample, state_l0_ssm, state_l0_conv, p0)
    hp, hs, s1_v = _gmlp_layer(hp, hs, p1)
    hp, hs, p2_C, p2_n, p2_m, p2_conv, s2_C, s2_n, s2_m, s2_conv = _mlstm_layer(
        hp, hs, state_l2_C, state_l2_n, state_l2_m, state_l2_conv, p2)
    y_prompt, y_sample, p3_ssm, p3_conv, s3_ssm, s3_conv = _ssd_layer(
        hp, hs, state_l3_ssm, state_l3_conv, p3, final_w=final_norm_w)
    return (y_prompt, y_sample,
            p0_ssm, p0_conv, s0_ssm, s0_conv,
            s1_v,
            p2_C, p2_n, p2_m, p2_conv, s2_C, s2_n, s2_m, s2_conv,
            p3_ssm, p3_conv, s3_ssm, s3_conv)
```

```python
import functools
import math

import jax
import jax.numpy as jnp
from jax import lax
from jax.experimental import pallas as pl
from jax.experimental.pallas import tpu as pltpu

F32 = jnp.float32
BF16 = jnp.bfloat16
EPS = 1e-6
CONV_K = 4
CHUNK = 128
LANES = 128
MXU_TILE = 256
SUBLANES = 8
SSD_HEAD_DIM = 64
SSD_STATE = 128
SSD_GROUPS = 8
ML_HEADS = 4
ML_BLOCK = 4
GM_GROUPS = 8
VMEM_LIMIT = 56 * 1024 * 1024
HI = lax.Precision.HIGHEST
NT_DIMS = (((1,), (1,)), ((), ()))


def _cparams(*sem):
    return pltpu.CompilerParams(dimension_semantics=sem, vmem_limit_bytes=VMEM_LIMIT)


def _sigmoid(x):
    return 1.0 / (1.0 + jnp.exp(-x))


def _silu(x):
    return x * _sigmoid(x)


def _softplus(x):
    return jnp.maximum(x, 0.0) + jnp.log(1.0 + jnp.exp(-jnp.abs(x)))


def _bdot(a, b):
    return jnp.dot(a.astype(BF16), b.astype(BF16), preferred_element_type=F32)


def _bdot_nt(a, b):
    return lax.dot_general(a.astype(BF16), b.astype(BF16), NT_DIMS, preferred_element_type=F32)


def _hdot(a, b):
    return jnp.dot(a, b, precision=HI, preferred_element_type=F32)


def _rms(x, w):
    return x * lax.rsqrt(jnp.mean(x * x, axis=-1, keepdims=True) + EPS) * w


def _tril(n):
    r = lax.broadcasted_iota(jnp.int32, (n, n), 0)
    c = lax.broadcasted_iota(jnp.int32, (n, n), 1)
    return r >= c


def _expand_sel(n_in, width, n_out):
    r = lax.broadcasted_iota(jnp.int32, (n_in, n_out), 0)
    c = lax.broadcasted_iota(jnp.int32, (n_in, n_out), 1)
    return (c // width == r).astype(F32)


def _split_expand(x, sel_bf16):
    hi = x.astype(BF16)
    lo = (x - hi.astype(F32)).astype(BF16)
    return (jnp.dot(hi, sel_bf16, preferred_element_type=F32)
            + jnp.dot(lo, sel_bf16, preferred_element_type=F32))


def _norm_matmul_kernel(x_ref, nw_ref, w_ref, o_ref):
    o_ref[...] = _bdot(_rms(x_ref[...], nw_ref[...]), w_ref[...])


def _norm_matmul(x2d, norm_w, w_bf16):
    m, k = x2d.shape
    n = w_bf16.shape[1]
    tm = min(m, 256)
    return pl.pallas_call(
        _norm_matmul_kernel,
        name="norm_matmul",
        grid=(m // tm,),
        in_specs=[pl.BlockSpec((tm, k), lambda i: (i, 0)),
                  pl.BlockSpec((1, k), lambda i: (0, 0)),
                  pl.BlockSpec((k, n), lambda i: (0, 0))],
        out_specs=pl.BlockSpec((tm, n), lambda i: (i, 0)),
        out_shape=jax.ShapeDtypeStruct((m, n), F32),
        compiler_params=_cparams("parallel"),
    )(x2d, norm_w.reshape(1, k), w_bf16)


def _matmul_res_kernel(g_ref, w_ref, x_ref, o_ref):
    o_ref[...] = x_ref[...] + _bdot(g_ref[...], w_ref[...])


def _matmul_res(g2d, w_bf16, x2d):
    m, k = g2d.shape
    n = w_bf16.shape[1]
    tm = min(m, 512)
    return pl.pallas_call(
        _matmul_res_kernel,
        name="matmul_res",
        grid=(m // tm,),
        in_specs=[pl.BlockSpec((tm, k), lambda i: (i, 0)),
                  pl.BlockSpec((k, n), lambda i: (0, 0)),
                  pl.BlockSpec((tm, n), lambda i: (i, 0))],
        out_specs=pl.BlockSpec((tm, n), lambda i: (i, 0)),
        out_shape=jax.ShapeDtypeStruct((m, n), F32),
        compiler_params=_cparams("parallel"),
    )(g2d, w_bf16, x2d)


CONV_ROWS = CHUNK + SUBLANES
CONV_VREGS = CONV_ROWS // SUBLANES
CUR = slice(SUBLANES, CONV_ROWS)


def _conv_load(buf_ref, src_ref, col0, is_first):
    ntile = src_ref.shape[1] // LANES
    t0 = col0 // LANES

    @pl.when(is_first)
    def _():
        buf_ref[t0:t0 + ntile, 0:SUBLANES, :] = jnp.zeros((ntile, SUBLANES, LANES), F32)

    @pl.when(jnp.logical_not(is_first))
    def _():
        buf_ref[t0:t0 + ntile, 0:SUBLANES, :] = buf_ref[t0:t0 + ntile, CHUNK:CONV_ROWS, :]

    for j in range(ntile):
        buf_ref[t0 + j, CUR, :] = src_ref[:, j * LANES:(j + 1) * LANES]


def _conv_silu(buf_ref, act_ref, cw_ref, cb_ref):
    nv = CONV_VREGS
    for j in range(buf_ref.shape[0]):
        cols = slice(j * LANES, (j + 1) * LANES)
        v = [buf_ref[j, pl.ds(a, SUBLANES, stride=nv), :] for a in range(nv)]
        wrap = {a: pltpu.roll(v[a], 1, 0) for a in range(nv - (CONV_K - 1), nv)}
        coef = [jnp.broadcast_to(cw_ref[k:k + 1, cols], (SUBLANES, LANES)) for k in range(CONV_K)]
        bias = jnp.broadcast_to(cb_ref[:, cols], (SUBLANES, LANES))
        for a in range(nv):
            acc = bias + coef[CONV_K - 1] * v[a]
            for back in range(1, CONV_K):
                tap = v[a - back] if a >= back else wrap[a - back + nv]
                acc = acc + coef[CONV_K - 1 - back] * tap
            act_ref[j, pl.ds(a, SUBLANES, stride=nv), :] = _silu(acc)


def _act_cols(act_ref, col0, width):
    t0 = col0 // LANES
    tiles = [act_ref[t0 + j, CUR, :] for j in range(width // LANES)]
    return tiles[0] if len(tiles) == 1 else jnp.concatenate(tiles, axis=1)


def _ssd_scan_kernel(xs_ref, bc_ref, dt_ref, cw_ref, cb_ref, dtb_ref, alog_ref, drep_ref, sel_ref,
                     y_ref, h_ref, buf_ref, act_ref, *, inner, heads):
    c = pl.program_id(1)
    L = CHUNK
    gn = SSD_GROUPS * SSD_STATE
    rep = heads // SSD_GROUPS
    gw = rep * SSD_HEAD_DIM

    @pl.when(c == 0)
    def _():
        h_ref[...] = jnp.zeros(h_ref.shape, F32)

    _conv_load(buf_ref, xs_ref, 0, c == 0)
    _conv_load(buf_ref, bc_ref, inner, c == 0)
    _conv_silu(buf_ref, act_ref, cw_ref, cb_ref)

    lane = lax.broadcasted_iota(jnp.int32, (1, LANES), 1)
    dt = _softplus(dt_ref[...] + dtb_ref[...])
    a_neg = jnp.where(lane < heads, -jnp.exp(alog_ref[...]), 0.0)
    da = dt * a_neg
    tri = _tril(L)
    cum = _hdot(tri.astype(F32), da)
    cum_t = cum.T
    dt_t = dt.T
    ecum = jnp.exp(cum)
    cum_end = cum[L - 1:L, :]
    wend = jnp.exp(cum_end - cum) * dt
    dend = jnp.exp(cum_end)
    e_all = _split_expand(ecum, sel_ref[...])
    w_all = _split_expand(wend, sel_ref[...])

    lane_g = lax.broadcasted_iota(jnp.int32, (L, gw), 1) // SSD_HEAD_DIM
    for g in range(SSD_GROUPS):
        bg = _act_cols(act_ref, inner + g * SSD_STATE, SSD_STATE).astype(BF16)
        cg = _act_cols(act_ref, inner + gn + g * SSD_STATE, SSD_STATE).astype(BF16)
        cb = lax.dot_general(cg, bg, NT_DIMS, preferred_element_type=F32)
        xg = _act_cols(act_ref, g * gw, gw)
        mixes = []
        for r in range(rep):
            h = g * rep + r
            seg = cum[:, h:h + 1] - cum_t[h:h + 1, :]
            decay = jnp.exp(jnp.where(tri, seg, -jnp.inf))
            mixes.append((cb * decay * dt_t[h:h + 1, :]).astype(BF16))
        mixcat = jnp.concatenate(mixes, axis=1)
        xblk = jnp.concatenate([jnp.where(lane_g == r, xg, 0.0).astype(BF16) for r in range(rep)],
                               axis=0)
        y = jnp.dot(mixcat, xblk, preferred_element_type=F32)
        rows = slice(g * gw, (g + 1) * gw)
        hg = h_ref[0, rows, :]
        yi = lax.dot_general(cg, hg.astype(BF16), NT_DIMS, preferred_element_type=F32)
        hs = [g * rep + r for r in range(rep)]
        y_ref[:, rows] = y + yi * e_all[:, rows] + drep_ref[:, rows] * xg
        upd = jnp.dot((xg * w_all[:, rows]).T.astype(BF16), bg, preferred_element_type=F32)
        for r, h in enumerate(hs):
            hr = slice(r * SSD_HEAD_DIM, (r + 1) * SSD_HEAD_DIM)
            h_ref[0, g * gw + r * SSD_HEAD_DIM:g * gw + (r + 1) * SSD_HEAD_DIM, :] = (
                dend[:, h:h + 1] * hg[hr, :] + upd[hr, :])


def _ssd_scan(zx, bsz, seq, inner, heads, conv_w, conv_b, dt_bias, a_log, d_skip):
    nc = seq // CHUNK
    gn = SSD_GROUPS * SSD_STATE
    conv_dim = inner + 2 * gn
    xs_blk = inner // inner
    bc_blk = (2 * inner) // (2 * gn)
    dt_blk = (inner + conv_dim) // LANES
    pad = LANES - heads
    dtb = jnp.pad(dt_bias, (0, pad)).reshape(1, LANES)
    alog = jnp.pad(a_log, (0, pad)).reshape(1, LANES)
    drep = jnp.repeat(d_skip, SSD_HEAD_DIM).reshape(1, inner)
    sel = (jnp.arange(inner)[None, :] // SSD_HEAD_DIM == jnp.arange(LANES)[:, None]).astype(BF16)
    kern = functools.partial(_ssd_scan_kernel, inner=inner, heads=heads)
    row = lambda b, c: b * nc + c
    return pl.pallas_call(
        kern,
        name="ssd_scan",
        grid=(bsz, nc),
        in_specs=[pl.BlockSpec((CHUNK, inner), lambda b, c: (row(b, c), xs_blk)),
                  pl.BlockSpec((CHUNK, 2 * gn), lambda b, c: (row(b, c), bc_blk)),
                  pl.BlockSpec((CHUNK, LANES), lambda b, c: (row(b, c), dt_blk)),
                  pl.BlockSpec((CONV_K, conv_dim), lambda b, c: (0, 0)),
                  pl.BlockSpec((1, conv_dim), lambda b, c: (0, 0)),
                  pl.BlockSpec((1, LANES), lambda b, c: (0, 0)),
                  pl.BlockSpec((1, LANES), lambda b, c: (0, 0)),
                  pl.BlockSpec((1, inner), lambda b, c: (0, 0)),
                  pl.BlockSpec((LANES, inner), lambda b, c: (0, 0))],
        out_specs=[pl.BlockSpec((CHUNK, inner), lambda b, c: (row(b, c), 0)),
                   pl.BlockSpec((1, inner, SSD_STATE), lambda b, c: (b, 0, 0))],
        out_shape=[jax.ShapeDtypeStruct((bsz * seq, inner), F32),
                   jax.ShapeDtypeStruct((bsz, inner, SSD_STATE), F32)],
        scratch_shapes=[pltpu.VMEM((conv_dim // LANES, CONV_ROWS, LANES), F32),
                        pltpu.VMEM((conv_dim // LANES, CONV_ROWS, LANES), F32)],
        compiler_params=_cparams("parallel", "arbitrary"),
    )(zx, zx, zx, conv_w, conv_b.reshape(1, conv_dim), dtb, alog, drep, sel)


def _ssd_post_kernel(*refs, has_add, has_final):
    y_ref, z_ref, x_ref, gw_ref, w_ref = refs[:5]
    pos = 5
    add_ref = fin_ref = None
    if has_add:
        add_ref = refs[pos]
        pos += 1
    if has_final:
        fin_ref = refs[pos]
        pos += 1
    o_ref = refs[pos]
    inner = y_ref.shape[1]
    gwid = inner // SSD_GROUPS
    parts = []
    for g in range(SSD_GROUPS):
        cols = slice(g * gwid, (g + 1) * gwid)
        y = y_ref[:, cols]
        if has_add:
            y = y + add_ref[:, cols]
        y = y * _silu(z_ref[:, cols])
        y = y * lax.rsqrt(jnp.mean(y * y, axis=-1, keepdims=True) + EPS)
        parts.append((y * gw_ref[:, cols]).astype(BF16))
    out = x_ref[...] + jnp.dot(jnp.concatenate(parts, axis=1), w_ref[...],
                               preferred_element_type=F32)
    if has_final:
        out = _rms(out, fin_ref[...])
    o_ref[...] = out


def _ssd_post(y, zx, x2d, gnorm_w, w_out_bf16, add=None, final_w=None):
    m, inner = y.shape
    d = x2d.shape[1]
    tm = min(m, 512)
    ins = [y, zx, x2d, gnorm_w.reshape(1, inner), w_out_bf16]
    specs = [pl.BlockSpec((tm, inner), lambda i: (i, 0)),
             pl.BlockSpec((tm, inner), lambda i: (i, 0)),
             pl.BlockSpec((tm, d), lambda i: (i, 0)),
             pl.BlockSpec((1, inner), lambda i: (0, 0)),
             pl.BlockSpec((inner, d), lambda i: (0, 0))]
    if add is not None:
        ins.append(add)
        specs.append(pl.BlockSpec((tm, inner), lambda i: (i, 0)))
    if final_w is not None:
        ins.append(final_w.reshape(1, d))
        specs.append(pl.BlockSpec((1, d), lambda i: (0, 0)))
    kern = functools.partial(_ssd_post_kernel, has_add=add is not None, has_final=final_w is not None)
    return pl.pallas_call(
        kern,
        name="ssd_post",
        grid=(m // tm,),
        in_specs=specs,
        out_specs=pl.BlockSpec((tm, d), lambda i: (i, 0)),
        out_shape=jax.ShapeDtypeStruct((m, d), F32),
        compiler_params=_cparams("parallel"),
    )(*ins)


def _ssd_dec_pre_kernel(xbc_ref, dt_ref, c0_ref, cw_ref, cb_ref, dtb_ref, alog_ref, drep_ref,
                        cnew_ref, dtx_ref, darep_ref, b_ref, c_ref, dx_ref, *, inner, heads):
    gn = SSD_GROUPS * SSD_STATE
    cd = inner + 2 * gn
    xnew = xbc_ref[...]
    acc = cb_ref[...] + cw_ref[CONV_K - 1:CONV_K, :] * xnew
    for k in range(CONV_K - 1):
        acc = acc + cw_ref[k:k + 1, :] * c0_ref[:, k * cd:(k + 1) * cd]
    for k in range(CONV_K - 2):
        cnew_ref[:, k * cd:(k + 1) * cd] = c0_ref[:, (k + 1) * cd:(k + 2) * cd]
    cnew_ref[:, (CONV_K - 2) * cd:(CONV_K - 1) * cd] = xnew
    act = _silu(acc)
    xs = act[:, 0:inner]
    b_ref[...] = act[:, inner:inner + gn]
    c_ref[...] = act[:, inner + gn:inner + 2 * gn]
    lane = lax.broadcasted_iota(jnp.int32, (1, LANES), 1)
    dt = _softplus(dt_ref[...] + dtb_ref[...])
    a_neg = jnp.where(lane < heads, -jnp.exp(alog_ref[...]), 0.0)
    da = jnp.exp(dt * a_neg)
    sel = _expand_sel(LANES, SSD_HEAD_DIM, inner)
    dtx_ref[...] = _hdot(dt, sel) * xs
    darep_ref[...] = _hdot(da, sel)
    dx_ref[...] = drep_ref[...] * xs


def _ssd_dec_state_kernel(h_ref, dtxT_ref, daT_ref, b_ref, c_ref, hn_ref, yT_ref, *, bb, rep):
    i = pl.program_id(0)
    gw = rep * SSD_HEAD_DIM
    nb = yT_ref.shape[1]
    lane = lax.broadcasted_iota(jnp.int32, (gw, nb), 1)

    def body(bi, carry):
        bglob = i * bb + bi
        bmat = b_ref[bi]
        cmat = c_ref[bi]
        for g in range(SSD_GROUPS):
            rows = slice(g * gw, (g + 1) * gw)
            msk = lane == bglob
            dtx = jnp.sum(jnp.where(msk, dtxT_ref[rows, :], 0.0), axis=1, keepdims=True)
            dac = jnp.sum(jnp.where(msk, daT_ref[rows, :], 0.0), axis=1, keepdims=True)
            hnew = dac * h_ref[bi, rows, :] + dtx * bmat[g:g + 1, :]
            hn_ref[bi, rows, :] = hnew
            ycol = jnp.sum(hnew * cmat[g:g + 1, :], axis=1, keepdims=True)
            yT_ref[rows, :] = jnp.where(msk, ycol, yT_ref[rows, :])
        return carry

    lax.fori_loop(0, bb, body, 0)


def _ssd_decode(zx, x2d, ssm0, conv0, inner, heads, prm, w_out_bf16, final_w=None):
    norm_w, in_proj, conv_w, conv_b, dt_bias, a_log, d_skip, gnorm_w, out_proj = prm
    nb = zx.shape[0]
    gn = SSD_GROUPS * SSD_STATE
    cd = inner + 2 * gn
    pad = LANES - heads
    dtb = jnp.pad(dt_bias, (0, pad)).reshape(1, LANES)
    alog = jnp.pad(a_log, (0, pad)).reshape(1, LANES)
    drep = jnp.repeat(d_skip, SSD_HEAD_DIM).reshape(1, inner)
    full = lambda shape: pl.BlockSpec(shape, lambda i: tuple(0 for _ in shape))
    kern = functools.partial(_ssd_dec_pre_kernel, inner=inner, heads=heads)
    cnew, dtx, darep, bact, cact, dx = pl.pallas_call(
        kern,
        name="ssd_dec_pre",
        grid=(1,),
        in_specs=[full((nb, cd)), full((nb, LANES)), full((nb, (CONV_K - 1) * cd)), full((CONV_K, cd)), full((1, cd)),
                  full((1, LANES)), full((1, LANES)), full((1, inner))],
        out_specs=[full((nb, (CONV_K - 1) * cd)), full((nb, inner)), full((nb, inner)),
                   full((nb, gn)), full((nb, gn)), full((nb, inner))],
        out_shape=[jax.ShapeDtypeStruct((nb, (CONV_K - 1) * cd), F32),
                   jax.ShapeDtypeStruct((nb, inner), F32), jax.ShapeDtypeStruct((nb, inner), F32),
                   jax.ShapeDtypeStruct((nb, gn), F32), jax.ShapeDtypeStruct((nb, gn), F32),
                   jax.ShapeDtypeStruct((nb, inner), F32)],
        compiler_params=_cparams("arbitrary"),
    )(zx[:, inner:inner + cd], zx[:, inner + cd:inner + cd + LANES],
      conv0.reshape(nb, (CONV_K - 1) * cd), conv_w, conv_b.reshape(1, cd), dtb, alog, drep)

    bb = 4 if nb % 4 == 0 else 1
    rep = heads // SSD_GROUPS
    kern = functools.partial(_ssd_dec_state_kernel, bb=bb, rep=rep)
    hnew, y_t = pl.pallas_call(
        kern,
        name="ssd_dec_state",
        grid=(nb // bb,),
        in_specs=[pl.BlockSpec((bb, inner, SSD_STATE), lambda i: (i, 0, 0)),
                  pl.BlockSpec((inner, nb), lambda i: (0, 0)),
                  pl.BlockSpec((inner, nb), lambda i: (0, 0)),
                  pl.BlockSpec((bb, SSD_GROUPS, SSD_STATE), lambda i: (i, 0, 0)),
                  pl.BlockSpec((bb, SSD_GROUPS, SSD_STATE), lambda i: (i, 0, 0))],
        out_specs=[pl.BlockSpec((bb, inner, SSD_STATE), lambda i: (i, 0, 0)),
                   pl.BlockSpec((inner, nb), lambda i: (0, 0))],
        out_shape=[jax.ShapeDtypeStruct((nb, inner, SSD_STATE), F32),
                   jax.ShapeDtypeStruct((inner, nb), F32)],
        compiler_params=_cparams("arbitrary"),
    )(ssm0.reshape(nb, inner, SSD_STATE), dtx.T, darep.T,
      bact.reshape(nb, SSD_GROUPS, SSD_STATE), cact.reshape(nb, SSD_GROUPS, SSD_STATE))

    out = _ssd_post(y_t.T, zx, x2d, gnorm_w, w_out_bf16, add=dx, final_w=final_w)
    return out, hnew.reshape(nb, heads, SSD_HEAD_DIM, SSD_STATE), cnew.reshape(nb, CONV_K - 1, cd)


def _layernorm(v, w, b):
    mu = jnp.mean(v, axis=-1, keepdims=True)
    vc = v - mu
    return vc * lax.rsqrt(jnp.mean(vc * vc, axis=-1, keepdims=True) + EPS) * w + b


def _gmlp_kernel(u_ref, v_ref, z_ref, x_ref, lw_ref, lb_ref, ws_ref, sbT_ref, w_ref, o_ref, *, nck):
    inner = v_ref.shape[1]
    gd = inner // GM_GROUPS
    tri = _tril(CHUNK)
    for ck in range(nck):
        rows = slice(ck * CHUNK, (ck + 1) * CHUNK)
        vn = _layernorm(v_ref[rows, :], lw_ref[...], lb_ref[...])
        parts = []
        for g in range(GM_GROUPS):
            cols = slice(g * gd, (g + 1) * gd)
            wg = jnp.where(tri, ws_ref[g], 0.0)
            mixed = _bdot(wg, vn[:, cols]) + sbT_ref[:, g:g + 1]
            parts.append((u_ref[rows, cols] * mixed * _silu(z_ref[rows, cols])).astype(BF16))
        o_ref[rows, :] = x_ref[rows, :] + jnp.dot(jnp.concatenate(parts, axis=1), w_ref[...],
                                                  preferred_element_type=F32)


def _gmlp_prompt(uvz, x2d, inner, v_ln_w, v_ln_b, spatial_w, spatial_b, w_out_bf16):
    m, d = x2d.shape
    nck = 2 if (m // CHUNK) % 2 == 0 else 1
    tm = nck * CHUNK
    sb_t = jnp.pad(spatial_b.T, ((0, 0), (0, LANES - GM_GROUPS)))
    kern = functools.partial(_gmlp_kernel, nck=nck)
    return pl.pallas_call(
        kern,
        name="gmlp_prompt",
        grid=(m // tm,),
        in_specs=[pl.BlockSpec((tm, inner), lambda i: (i, 0)),
                  pl.BlockSpec((tm, inner), lambda i: (i, 1)),
                  pl.BlockSpec((tm, inner), lambda i: (i, 2)),
                  pl.BlockSpec((tm, d), lambda i: (i, 0)),
                  pl.BlockSpec((1, inner), lambda i: (0, 0)),
                  pl.BlockSpec((1, inner), lambda i: (0, 0)),
                  pl.BlockSpec((GM_GROUPS, CHUNK, CHUNK), lambda i: (0, 0, 0)),
                  pl.BlockSpec((CHUNK, LANES), lambda i: (0, 0)),
                  pl.BlockSpec((inner, d), lambda i: (0, 0))],
        out_specs=pl.BlockSpec((tm, d), lambda i: (i, 0)),
        out_shape=jax.ShapeDtypeStruct((m, d), F32),
        compiler_params=_cparams("parallel"),
    )(uvz, uvz, uvz, x2d, v_ln_w.reshape(1, inner), v_ln_b.reshape(1, inner), spatial_w, sb_t,
      w_out_bf16)


def _gmlp_dec_kernel(u_ref, v_ref, z_ref, x_ref, lw_ref, lb_ref, w00_ref, sb0_ref, w_ref,
                     o_ref, vn_ref):
    vn = _layernorm(v_ref[...], lw_ref[...], lb_ref[...])
    vn_ref[...] = vn
    mixed = w00_ref[...] * vn + sb0_ref[...]
    g = u_ref[...] * mixed * _silu(z_ref[...])
    o_ref[...] = x_ref[...] + _bdot(g, w_ref[...])


def _gmlp_decode(uvz, x2d, inner, v_ln_w, v_ln_b, spatial_w, spatial_b, w_out_bf16):
    nb, d = x2d.shape
    gd = inner // GM_GROUPS
    w00 = jnp.repeat(spatial_w[:, 0, 0], gd).reshape(1, inner)
    sb0 = jnp.repeat(spatial_b[:, 0], gd).reshape(1, inner)
    vec = pl.BlockSpec((1, inner), lambda i: (0, 0))
    return pl.pallas_call(
        _gmlp_dec_kernel,
        name="gmlp_dec",
        grid=(1,),
        in_specs=[pl.BlockSpec((nb, inner), lambda i: (0, 0)),
                  pl.BlockSpec((nb, inner), lambda i: (0, 1)),
                  pl.BlockSpec((nb, inner), lambda i: (0, 2)),
                  pl.BlockSpec((nb, d), lambda i: (0, 0)),
                  vec, vec, vec, vec,
                  pl.BlockSpec((inner, d), lambda i: (0, 0))],
        out_specs=[pl.BlockSpec((nb, d), lambda i: (0, 0)),
                   pl.BlockSpec((nb, inner), lambda i: (0, 0))],
        out_shape=[jax.ShapeDtypeStruct((nb, d), F32), jax.ShapeDtypeStruct((nb, inner), F32)],
        compiler_params=_cparams("arbitrary"),
    )(uvz, uvz, uvz, x2d, v_ln_w.reshape(1, inner), v_ln_b.reshape(1, inner), w00, sb0, w_out_bf16)


def _blockdiag_coefs(w):
    n = w.shape[0]
    rows = []
    for d in range(-(ML_BLOCK - 1), ML_BLOCK):
        cols = []
        for i in range(ML_BLOCK):
            j = i + d
            cols.append(w[:, j, i] if 0 <= j < ML_BLOCK else jnp.zeros((n,), w.dtype))
        rows.append(jnp.stack(cols, axis=1).reshape(n * ML_BLOCK))
    return jnp.stack(rows, axis=0)


def _blockdiag_apply(x, coef_refs):
    width = x.shape[1]
    outs = [None] * len(coef_refs)
    for di, d in enumerate(range(-(ML_BLOCK - 1), ML_BLOCK)):
        xs = x if d == 0 else pltpu.roll(x, (-d) % width, 1)
        for n, cref in enumerate(coef_refs):
            t = xs * cref[di:di + 1, :]
            outs[n] = t if outs[n] is None else outs[n] + t
    return outs


def _mlstm_scan_kernel(xm_ref, z_ref, cw_ref, cb_ref, wqk_ref, wvo_ref, bo_ref,
                       wifq_ref, wifk_ref, wifv_ref, bif_ref, mhw_ref, skip_ref,
                       g_ref, cst_ref, nst_ref, mst_ref,
                       buf_ref, act_ref, q_ref, k_ref, v_ref, o_ref, *, heads):
    c = pl.program_id(1)
    L = CHUNK
    inner = xm_ref.shape[1]
    hd = inner // heads
    scale = hd ** -0.5

    @pl.when(c == 0)
    def _():
        cst_ref[...] = jnp.zeros(cst_ref.shape, F32)
        nst_ref[...] = jnp.zeros(nst_ref.shape, F32)
        mst_ref[...] = jnp.zeros(mst_ref.shape, F32)

    _conv_load(buf_ref, xm_ref, 0, c == 0)
    _conv_silu(buf_ref, act_ref, cw_ref, cb_ref)

    tw = wqk_ref.shape[1]
    gates = bif_ref[...]
    for t in range(inner // tw):
        cols = slice(t * tw, (t + 1) * tw)
        qk = _bdot(_act_cols(act_ref, t * tw, tw), wqk_ref[t])
        vo = _bdot(xm_ref[:, cols], wvo_ref[t])
        q_ref[:, cols] = qk[:, 0:tw]
        k_ref[:, cols] = qk[:, tw:2 * tw]
        v_ref[:, cols] = vo[:, 0:tw]
        o_ref[:, cols] = _sigmoid(vo[:, tw:2 * tw] + bo_ref[:, cols])
        gates = (gates + _bdot(qk[:, 0:tw], wifq_ref[cols, :]) + _bdot(qk[:, tw:2 * tw], wifk_ref[cols, :])
                 + _bdot(vo[:, 0:tw], wifv_ref[cols, :]))
    lf = -_softplus(-gates)
    tri = _tril(L)
    bt = _hdot(tri.astype(F32), lf)
    g_t = gates.T
    b_t = bt.T
    mvec = mst_ref[0]
    for h in range(heads):
        cols = slice(h * hd, (h + 1) * hd)
        bcol = bt[:, heads + h:heads + h + 1]
        igcol = gates[:, h:h + 1]
        brow = b_t[heads + h:heads + h + 1, :]
        igrow = g_t[h:h + 1, :]
        m_old = mvec[:, h:h + 1]
        d = jnp.where(tri, bcol - brow + igrow, -jnp.inf)
        inter = bcol + m_old
        mt = jnp.maximum(inter, jnp.max(d, axis=1, keepdims=True))
        qf = q_ref[:, cols]
        qh = qf.astype(BF16)
        kh = k_ref[:, cols] * scale
        khb = kh.astype(BF16)
        vh = v_ref[:, cols]
        s = lax.dot_general(qh, khb, NT_DIMS, preferred_element_type=F32)
        w = jnp.exp(d - mt) * s
        wi = jnp.exp(inter - mt)
        cmat = cst_ref[0, h]
        cq = lax.dot_general(qh, cmat.astype(BF16), NT_DIMS, preferred_element_type=F32)
        num = wi * cq + _bdot(w, vh)
        nrow = nst_ref[0, h:h + 1, :]
        nq = jnp.sum(qf * nrow, axis=1, keepdims=True)
        den = wi * nq + jnp.sum(w, axis=1, keepdims=True)
        den = jnp.maximum(jnp.abs(den), jnp.exp(-mt))
        hh = o_ref[:, cols] * (num / den)
        m_new = mt[L - 1:L, :]
        we = jnp.exp(bcol[L - 1:L, :] - bcol + igcol - m_new)
        dp = jnp.exp(inter[L - 1:L, :] - m_new)
        cst_ref[0, h] = dp * cmat + jnp.dot((we * vh).T.astype(BF16), khb,
                                            preferred_element_type=F32)
        nst_ref[0, h:h + 1, :] = dp * nrow + jnp.sum(we * kh, axis=0, keepdims=True)
        mst_ref[0, :, h:h + 1] = m_new
        mu = jnp.mean(hh, axis=1, keepdims=True)
        hc = hh - mu
        hn = hc * lax.rsqrt(jnp.mean(hc * hc, axis=1, keepdims=True) + EPS)
        hn = hn * mhw_ref[:, cols] + skip_ref[:, cols] * _act_cols(act_ref, h * hd, hd)
        g_ref[:, cols] = hn * _silu(z_ref[:, cols])


def _blockdiag_tiles(w, tile):
    per = tile // ML_BLOCK
    nt = w.shape[0] // per
    eye = jnp.eye(per, dtype=w.dtype)
    wt = w.reshape(nt, per, ML_BLOCK, ML_BLOCK)
    return jnp.einsum("tnji,nm->tnjmi", wt, eye).reshape(nt, tile, tile)


def _mlstm_consts(prm, inner):
    (norm_w, in_proj, conv_w, conv_b, w_q, w_k, w_v, w_o, b_o, w_if, b_if, mh_norm_w, skip,
     out_proj) = prm
    heads = ML_HEADS
    padn = LANES - 2 * heads
    wif = jnp.pad(w_if, ((0, 0), (0, padn))).astype(BF16)
    bif = jnp.pad(b_if, (0, padn)).reshape(1, LANES)
    return dict(
        conv_w=conv_w, conv_b=conv_b.reshape(1, inner),
        cq=_blockdiag_coefs(w_q), ck=_blockdiag_coefs(w_k), cv=_blockdiag_coefs(w_v),
        co=_blockdiag_coefs(w_o), bo=b_o.reshape(1, inner),
        wqk=jnp.concatenate([_blockdiag_tiles(w_q, MXU_TILE), _blockdiag_tiles(w_k, MXU_TILE)],
                            axis=2).astype(BF16),
        wvo=jnp.concatenate([_blockdiag_tiles(w_v, MXU_TILE), _blockdiag_tiles(w_o, MXU_TILE)],
                            axis=2).astype(BF16),
        wifq=wif[0:inner], wifk=wif[inner:2 * inner], wifv=wif[2 * inner:3 * inner], bif=bif,
        mhw=mh_norm_w.reshape(1, inner), skip=skip.reshape(1, inner))


def _mlstm_scan(xmz, bsz, seq, inner, cst):
    nc = seq // CHUNK
    heads = ML_HEADS
    hd = inner // heads
    ntile = inner // MXU_TILE
    row = lambda b, c: b * nc + c
    c2 = lambda shape: pl.BlockSpec(shape, lambda b, c: (0, 0))
    c3 = lambda shape: pl.BlockSpec(shape, lambda b, c: (0, 0, 0))
    kern = functools.partial(_mlstm_scan_kernel, heads=heads)
    return pl.pallas_call(
        kern,
        name="mlstm_scan",
        grid=(bsz, nc),
        in_specs=[pl.BlockSpec((CHUNK, inner), lambda b, c: (row(b, c), 0)),
                  pl.BlockSpec((CHUNK, inner), lambda b, c: (row(b, c), 1)),
                  c2((CONV_K, inner)), c2((1, inner)),
                  c3((ntile, MXU_TILE, 2 * MXU_TILE)), c3((ntile, MXU_TILE, 2 * MXU_TILE)),
                  c2((1, inner)),
                  c2((inner, LANES)), c2((inner, LANES)), c2((inner, LANES)), c2((1, LANES)),
                  c2((1, inner)), c2((1, inner))],
        out_specs=[pl.BlockSpec((CHUNK, inner), lambda b, c: (row(b, c), 0)),
                   pl.BlockSpec((1, heads, hd, hd), lambda b, c: (b, 0, 0, 0)),
                   pl.BlockSpec((1, heads, hd), lambda b, c: (b, 0, 0)),
                   pl.BlockSpec((1, 1, heads), lambda b, c: (b, 0, 0))],
        out_shape=[jax.ShapeDtypeStruct((bsz * seq, inner), F32),
                   jax.ShapeDtypeStruct((bsz, heads, hd, hd), F32),
                   jax.ShapeDtypeStruct((bsz, heads, hd), F32),
                   jax.ShapeDtypeStruct((bsz, 1, heads), F32)],
        scratch_shapes=[pltpu.VMEM((inner // LANES, CONV_ROWS, LANES), F32)] * 2
                       + [pltpu.VMEM((CHUNK, inner), F32)] * 4,
        compiler_params=_cparams("parallel", "arbitrary"),
    )(xmz, xmz, cst["conv_w"], cst["conv_b"], cst["wqk"], cst["wvo"], cst["bo"],
      cst["wifq"], cst["wifk"], cst["wifv"], cst["bif"], cst["mhw"], cst["skip"])


def _mlstm_dec_pre_kernel(xm_ref, c0_ref, n0_ref, m0_ref, cw_ref, cb_ref, cq_ref, ck_ref, cv_ref,
                          co_ref, bo_ref, wifq_ref, wifk_ref, wifv_ref, bif_ref,
                          cnew_ref, q_ref, k_ref, v_ref, og_ref, xc_ref, wev_ref, dprep_ref,
                          wrep_ref, denrep_ref, nnew_ref, mnew_ref, *, heads):
    inner = xm_ref.shape[1]
    nb = xm_ref.shape[0]
    hd = inner // heads
    xm = xm_ref[...]
    acc = cb_ref[...] + cw_ref[CONV_K - 1:CONV_K, :] * xm
    for kk in range(CONV_K - 1):
        acc = acc + cw_ref[kk:kk + 1, :] * c0_ref[:, kk * inner:(kk + 1) * inner]
    for kk in range(CONV_K - 2):
        cnew_ref[:, kk * inner:(kk + 1) * inner] = c0_ref[:, (kk + 1) * inner:(kk + 2) * inner]
    cnew_ref[:, (CONV_K - 2) * inner:(CONV_K - 1) * inner] = xm
    xc = _silu(acc)
    q, k = _blockdiag_apply(xc, [cq_ref, ck_ref])
    v, o_pre = _blockdiag_apply(xm, [cv_ref, co_ref])
    gates = _bdot(q, wifq_ref[...]) + _bdot(k, wifk_ref[...]) + _bdot(v, wifv_ref[...]) + bif_ref[...]
    ksc = k * (hd ** -0.5)
    lane = lax.broadcasted_iota(jnp.int32, (nb, LANES), 1)
    ig = gates
    lf = pltpu.roll(-_softplus(-gates), LANES - heads, 1)
    n0 = n0_ref[...]
    qk = jnp.zeros((nb, LANES), F32)
    nq = jnp.zeros((nb, LANES), F32)
    for h in range(heads):
        cols = slice(h * hd, (h + 1) * hd)
        qk = jnp.where(lane == h, jnp.sum(q[:, cols] * ksc[:, cols], axis=1, keepdims=True), qk)
        nq = jnp.where(lane == h, jnp.sum(q[:, cols] * n0[:, cols], axis=1, keepdims=True), nq)
    inter = lf + m0_ref[...]
    mt = jnp.maximum(inter, ig)
    wi = jnp.exp(inter - mt)
    we = jnp.exp(ig - mt)
    w = we * qk
    den = jnp.maximum(jnp.abs(wi * nq + w), jnp.exp(-mt))
    mnew_ref[...] = mt
    sel = _expand_sel(LANES, hd, inner)
    wi_rep = _hdot(wi, sel)
    we_rep = _hdot(we, sel)
    dprep_ref[...] = wi_rep
    wrep_ref[...] = _hdot(w, sel)
    denrep_ref[...] = _hdot(den, sel)
    wev_ref[...] = we_rep * v
    nnew_ref[...] = wi_rep * n0 + we_rep * ksc
    q_ref[...] = q
    k_ref[...] = ksc
    v_ref[...] = v
    og_ref[...] = _sigmoid(o_pre + bo_ref[...])
    xc_ref[...] = xc


def _mlstm_dec_state_kernel(c_ref, q_ref, k_ref, dp_ref, wevT_ref, cn_ref, cqT_ref, *, heads):
    b = pl.program_id(0)
    hd = c_ref.shape[2]
    nb = cqT_ref.shape[1]
    lane = lax.broadcasted_iota(jnp.int32, (hd, nb), 1)
    msk = lane == b
    for h in range(heads):
        rows = slice(h * hd, (h + 1) * hd)
        cmat = c_ref[0, h]
        cqcol = jnp.sum(cmat * q_ref[0, h:h + 1, :], axis=1, keepdims=True)
        wev = jnp.sum(jnp.where(msk, wevT_ref[rows, :], 0.0), axis=1, keepdims=True)
        cn_ref[0, h] = dp_ref[0, h:h + 1, :] * cmat + wev * k_ref[0, h:h + 1, :]
        cqT_ref[rows, :] = jnp.where(msk, cqcol, cqT_ref[rows, :])


def _mlstm_dec_post_kernel(cq_ref, v_ref, dp_ref, w_ref, den_ref, og_ref, xc_ref, z_ref,
                           mhw_ref, skip_ref, g_ref, *, heads):
    inner = cq_ref.shape[1]
    hd = inner // heads
    for h in range(heads):
        cols = slice(h * hd, (h + 1) * hd)
        num = dp_ref[:, cols] * cq_ref[:, cols] + w_ref[:, cols] * v_ref[:, cols]
        hh = og_ref[:, cols] * (num / den_ref[:, cols])
        mu = jnp.mean(hh, axis=1, keepdims=True)
        hc = hh - mu
        hn = hc * lax.rsqrt(jnp.mean(hc * hc, axis=1, keepdims=True) + EPS)
        hn = hn * mhw_ref[:, cols] + skip_ref[:, cols] * xc_ref[:, cols]
        g_ref[:, cols] = hn * _silu(z_ref[:, cols])


def _mlstm_decode(xmz, c0, n0, m0, conv0, inner, cst):
    nb = xmz.shape[0]
    heads = ML_HEADS
    hd = inner // heads
    nco = 2 * ML_BLOCK - 1
    full = lambda shape: pl.BlockSpec(shape, lambda i: tuple(0 for _ in shape))
    m0p = jnp.pad(m0, ((0, 0), (0, LANES - heads)))
    kern = functools.partial(_mlstm_dec_pre_kernel, heads=heads)
    big = jax.ShapeDtypeStruct((nb, inner), F32)
    outs = pl.pallas_call(
        kern,
        name="mlstm_dec_pre",
        grid=(1,),
        in_specs=[pl.BlockSpec((nb, inner), lambda i: (0, 0)),
                  full((nb, (CONV_K - 1) * inner)), full((nb, inner)), full((nb, LANES)),
                  full((CONV_K, inner)), full((1, inner)),
                  full((nco, inner)), full((nco, inner)), full((nco, inner)), full((nco, inner)),
                  full((1, inner)),
                  full((inner, LANES)), full((inner, LANES)), full((inner, LANES)), full((1, LANES))],
        out_specs=[full((nb, (CONV_K - 1) * inner))] + [full((nb, inner))] * 10 + [full((nb, LANES))],
        out_shape=[jax.ShapeDtypeStruct((nb, (CONV_K - 1) * inner), F32)] + [big] * 10
                  + [jax.ShapeDtypeStruct((nb, LANES), F32)],
        compiler_params=_cparams("arbitrary"),
    )(xmz, conv0.reshape(nb, (CONV_K - 1) * inner), n0.reshape(nb, inner), m0p,
      cst["conv_w"], cst["conv_b"], cst["cq"], cst["ck"], cst["cv"], cst["co"], cst["bo"],
      cst["wifq"], cst["wifk"], cst["wifv"], cst["bif"])
    cnew, q, ksc, v, og, xc, wev, dprep, wrep, denrep, nnew, mnew = outs

    kern = functools.partial(_mlstm_dec_state_kernel, heads=heads)
    h3 = lambda: pl.BlockSpec((1, heads, hd), lambda i: (i, 0, 0))
    c_new, cq_t = pl.pallas_call(
        kern,
        name="mlstm_dec_state",
        grid=(nb,),
        in_specs=[pl.BlockSpec((1, heads, hd, hd), lambda i: (i, 0, 0, 0)),
                  h3(), h3(), h3(),
                  pl.BlockSpec((inner, nb), lambda i: (0, 0))],
        out_specs=[pl.BlockSpec((1, heads, hd, hd), lambda i: (i, 0, 0, 0)),
                   pl.BlockSpec((inner, nb), lambda i: (0, 0))],
        out_shape=[jax.ShapeDtypeStruct((nb, heads, hd, hd), F32),
                   jax.ShapeDtypeStruct((inner, nb), F32)],
        compiler_params=_cparams("arbitrary"),
    )(c0, q.reshape(nb, heads, hd), ksc.reshape(nb, heads, hd), dprep.reshape(nb, heads, hd), wev.T)

    kern = functools.partial(_mlstm_dec_post_kernel, heads=heads)
    g = pl.pallas_call(
        kern,
        name="mlstm_dec_post",
        grid=(1,),
        in_specs=[full((nb, inner))] * 7
                 + [pl.BlockSpec((nb, inner), lambda i: (0, 1)), full((1, inner)), full((1, inner))],
        out_specs=full((nb, inner)),
        out_shape=big,
        compiler_params=_cparams("arbitrary"),
    )(cq_t.T, v, dprep, wrep, denrep, og, xc, xmz, cst["mhw"], cst["skip"])
    return (g, c_new, nnew.reshape(nb, heads, hd), mnew[:, 0:heads],
            cnew.reshape(nb, CONV_K - 1, inner))


def _ssd_in_weights(in_proj, inner, conv_dim):
    ncol = inner + conv_dim
    pad = LANES - (in_proj.shape[1] - ncol)
    return jnp.pad(in_proj, ((0, 0), (0, pad))).astype(BF16)


def _ssd_layer(xp, xs, ssm0, conv0, prm, final_w=None):
    norm_w, in_proj, conv_w, conv_b, dt_bias, a_log, d_skip, gnorm_w, out_proj = prm
    bsz, seq, d = xp.shape
    nb = xs.shape[0]
    inner = out_proj.shape[0]
    heads = a_log.shape[0]
    conv_dim = conv_w.shape[1]
    w_in = _ssd_in_weights(in_proj, inner, conv_dim)
    w_out = out_proj.astype(BF16)
    xp2 = xp.reshape(bsz * seq, d)
    xs2 = xs.reshape(nb, d)

    zx_p = _norm_matmul(xp2, norm_w, w_in)
    y_p, h_p = _ssd_scan(zx_p, bsz, seq, inner, heads, conv_w, conv_b, dt_bias, a_log, d_skip)
    out_p = _ssd_post(y_p, zx_p, xp2, gnorm_w, w_out, final_w=final_w)
    conv_p = zx_p.reshape(bsz, seq, -1)[:, seq - (CONV_K - 1):, inner:inner + conv_dim]

    zx_s = _norm_matmul(xs2, norm_w, w_in)
    out_s, h_s, conv_s = _ssd_decode(zx_s, xs2, ssm0, conv0, inner, heads, prm, w_out, final_w=final_w)
    return (out_p.reshape(bsz, seq, d), out_s.reshape(nb, 1, d),
            h_p.reshape(bsz, heads, SSD_HEAD_DIM, SSD_STATE), conv_p, h_s, conv_s)


def _gmlp_layer(xp, xs, prm):
    norm_w, in_proj, v_ln_w, v_ln_b, spatial_w, spatial_b, out_proj = prm
    bsz, seq, d = xp.shape
    nb = xs.shape[0]
    inner = out_proj.shape[0]
    w_in = in_proj.astype(BF16)
    w_out = out_proj.astype(BF16)
    xp2 = xp.reshape(bsz * seq, d)
    xs2 = xs.reshape(nb, d)
    uvz_p = _norm_matmul(xp2, norm_w, w_in)
    out_p = _gmlp_prompt(uvz_p, xp2, inner, v_ln_w, v_ln_b, spatial_w, spatial_b, w_out)
    uvz_s = _norm_matmul(xs2, norm_w, w_in)
    out_s, vn_s = _gmlp_decode(uvz_s, xs2, inner, v_ln_w, v_ln_b, spatial_w, spatial_b, w_out)
    return out_p.reshape(bsz, seq, d), out_s.reshape(nb, 1, d), vn_s.reshape(nb, 1, inner)


def _mlstm_layer(xp, xs, c0, n0, m0, conv0, prm):
    norm_w, in_proj = prm[0], prm[1]
    out_proj = prm[-1]
    bsz, seq, d = xp.shape
    nb = xs.shape[0]
    inner = out_proj.shape[0]
    w_in = in_proj.astype(BF16)
    w_out = out_proj.astype(BF16)
    cst = _mlstm_consts(prm, inner)
    xp2 = xp.reshape(bsz * seq, d)
    xs2 = xs.reshape(nb, d)

    xmz_p = _norm_matmul(xp2, norm_w, w_in)
    g_p, c_p, n_p, m_p = _mlstm_scan(xmz_p, bsz, seq, inner, cst)
    out_p = _matmul_res(g_p, w_out, xp2)
    conv_p = xmz_p.reshape(bsz, seq, -1)[:, seq - (CONV_K - 1):, 0:inner]

    xmz_s = _norm_matmul(xs2, norm_w, w_in)
    g_s, c_s, n_s, m_s, conv_s = _mlstm_decode(xmz_s, c0, n0, m0, conv0, inner, cst)
    out_s = _matmul_res(g_s, w_out, xs2)
    return (out_p.reshape(bsz, seq, d), out_s.reshape(nb, 1, d),
            c_p, n_p, m_p.reshape(bsz, ML_HEADS), conv_p, c_s, n_s, m_s, conv_s)


def kernel(x_prompt, x_sample, state_l0_ssm, state_l0_conv, state_l2_C, state_l2_n, state_l2_m, state_l2_conv, state_l3_ssm, state_l3_conv, l0_norm_w, l0_in_proj, l0_conv_w, l0_conv_b, l0_dt_bias, l0_A_log, l0_D_skip, l0_gnorm_w, l0_out_proj, l1_norm_w, l1_in_proj, l1_v_ln_w, l1_v_ln_b, l1_spatial_w, l1_spatial_b, l1_out_proj, l2_norm_w, l2_in_proj, l2_conv_w, l2_conv_b, l2_w_q, l2_w_k, l2_w_v, l2_w_o, l2_b_o, l2_w_if, l2_b_if, l2_mh_norm_w, l2_skip, l2_out_proj, l3_norm_w, l3_in_proj, l3_conv_w, l3_conv_b, l3_dt_bias, l3_A_log, l3_D_skip, l3_gnorm_w, l3_out_proj, final_norm_w):
    p0 = (l0_norm_w, l0_in_proj, l0_conv_w, l0_conv_b, l0_dt_bias, l0_A_log, l0_D_skip, l0_gnorm_w, l0_out_proj)
    p1 = (l1_norm_w, l1_in_proj, l1_v_ln_w, l1_v_ln_b, l1_spatial_w, l1_spatial_b, l1_out_proj)
    p2 = (l2_norm_w, l2_in_proj, l2_conv_w, l2_conv_b, l2_w_q, l2_w_k, l2_w_v, l2_w_o, l2_b_o,
          l2_w_if, l2_b_if, l2_mh_norm_w, l2_skip, l2_out_proj)
    p3 = (l3_norm_w, l3_in_proj, l3_conv_w, l3_conv_b, l3_dt_bias, l3_A_log, l3_D_skip, l3_gnorm_w, l3_out_proj)

    hp, hs, p0_ssm, p0_conv, s0_ssm, s0_conv = _ssd_layer(x_prompt, x_sample, state_l0_ssm, state_l0_conv, p0)
    hp, hs, s1_v = _gmlp_layer(hp, hs, p1)
    hp, hs, p2_C, p2_n, p2_m, p2_conv, s2_C, s2_n, s2_m, s2_conv = _mlstm_layer(
        hp, hs, state_l2_C, state_l2_n, state_l2_m, state_l2_conv, p2)
    y_prompt, y_sample, p3_ssm, p3_conv, s3_ssm, s3_conv = _ssd_layer(
        hp, hs, state_l3_ssm, state_l3_conv, p3, final_w=final_norm_w)
    return (y_prompt, y_sample,
            p0_ssm, p0_conv, s0_ssm, s0_conv,
            s1_v,
            p2_C, p2_n, p2_m, p2_conv, s2_C, s2_n, s2_m, s2_conv,
            p3_ssm, p3_conv, s3_ssm, s3_conv)
```

```python
import functools
import math

import jax
import jax.numpy as jnp
from jax import lax
from jax.experimental import pallas as pl
from jax.experimental.pallas import tpu as pltpu

F32 = jnp.float32
BF16 = jnp.bfloat16
EPS = 1e-6
CONV_K = 4
CHUNK = 128
LANES = 128
MXU_TILE = 256
SUBLANES = 8
SSD_HEAD_DIM = 64
SSD_STATE = 128
SSD_GROUPS = 8
OUT_GROUPS = 2
ML_HEADS = 4
ML_BLOCK = 4
GM_GROUPS = 8
VMEM_LIMIT = 56 * 1024 * 1024
HI = lax.Precision.HIGHEST
LOG2E = 1.4426950408889634
NT_DIMS = (((1,), (1,)), ((), ()))


def _cparams(*sem):
    return pltpu.CompilerParams(dimension_semantics=sem, vmem_limit_bytes=VMEM_LIMIT)


def _sigmoid(x):
    return 1.0 / (1.0 + jnp.exp2(x * (-LOG2E)))


def _silu(x):
    h = 0.5 * x
    return h + h * jnp.tanh(h)


def _softplus(x):
    return jnp.maximum(x, 0.0) + jnp.log(1.0 + jnp.exp(-jnp.abs(x)))


def _bdot(a, b):
    return jnp.dot(a.astype(BF16), b.astype(BF16), preferred_element_type=F32)


def _bdot_nt(a, b):
    return lax.dot_general(a.astype(BF16), b.astype(BF16), NT_DIMS, preferred_element_type=F32)


def _hdot(a, b):
    return jnp.dot(a, b, precision=HI, preferred_element_type=F32)


def _rms(x, w):
    return x * lax.rsqrt(jnp.mean(x * x, axis=-1, keepdims=True) + EPS) * w


def _tril(n):
    r = lax.broadcasted_iota(jnp.int32, (n, n), 0)
    c = lax.broadcasted_iota(jnp.int32, (n, n), 1)
    return r >= c


def _expand_sel(n_in, width, n_out):
    r = lax.broadcasted_iota(jnp.int32, (n_in, n_out), 0)
    c = lax.broadcasted_iota(jnp.int32, (n_in, n_out), 1)
    return (c // width == r).astype(F32)


def _split_bf16(x):
    hi = x.astype(BF16)
    return hi, (x - hi.astype(F32)).astype(BF16)


def _expand(pieces, sel_bf16):
    return (jnp.dot(pieces[0], sel_bf16, preferred_element_type=F32)
            + jnp.dot(pieces[1], sel_bf16, preferred_element_type=F32))


def _norm_matmul_kernel(x_ref, nw_ref, w_ref, o_ref):
    o_ref[...] = _bdot(_rms(x_ref[...], nw_ref[...]), w_ref[...])


def _norm_matmul(x2d, norm_w, w_bf16):
    m, k = x2d.shape
    n = w_bf16.shape[1]
    tm = min(m, 256)
    return pl.pallas_call(
        _norm_matmul_kernel,
        name="norm_matmul",
        grid=(m // tm,),
        in_specs=[pl.BlockSpec((tm, k), lambda i: (i, 0)),
                  pl.BlockSpec((1, k), lambda i: (0, 0)),
                  pl.BlockSpec((k, n), lambda i: (0, 0))],
        out_specs=pl.BlockSpec((tm, n), lambda i: (i, 0)),
        out_shape=jax.ShapeDtypeStruct((m, n), F32),
        compiler_params=_cparams("parallel"),
    )(x2d, norm_w.reshape(1, k), w_bf16)


def _matmul_res_kernel(g_ref, w_ref, x_ref, o_ref):
    o_ref[...] = x_ref[...] + _bdot(g_ref[...], w_ref[...])


def _matmul_res(g2d, w_bf16, x2d):
    m, k = g2d.shape
    n = w_bf16.shape[1]
    tm = min(m, 512)
    return pl.pallas_call(
        _matmul_res_kernel,
        name="matmul_res",
        grid=(m // tm,),
        in_specs=[pl.BlockSpec((tm, k), lambda i: (i, 0)),
                  pl.BlockSpec((k, n), lambda i: (0, 0)),
                  pl.BlockSpec((tm, n), lambda i: (i, 0))],
        out_specs=pl.BlockSpec((tm, n), lambda i: (i, 0)),
        out_shape=jax.ShapeDtypeStruct((m, n), F32),
        compiler_params=_cparams("parallel"),
    )(g2d, w_bf16, x2d)


CONV_ROWS = CHUNK + SUBLANES
CONV_VREGS = CONV_ROWS // SUBLANES
CUR = slice(SUBLANES, CONV_ROWS)


def _conv_tail(buf_ref, is_first):
    lead = (slice(None),) * (len(buf_ref.shape) - 2)

    @pl.when(is_first)
    def _():
        buf_ref[lead + (slice(0, SUBLANES), slice(None))] = jnp.zeros(
            buf_ref.shape[:-2] + (SUBLANES, LANES), F32)

    @pl.when(jnp.logical_not(is_first))
    def _():
        buf_ref[lead + (slice(0, SUBLANES), slice(None))] = buf_ref[lead + (slice(CHUNK, CONV_ROWS), slice(None))]


def _conv_fill(buf_ref, src_ref, col0):
    t0 = col0 // LANES
    for j in range(src_ref.shape[1] // LANES):
        buf_ref[t0 + j, CUR, :] = src_ref[:, j * LANES:(j + 1) * LANES]


def _conv_silu(buf_ref, act_ref, cw_ref, cb_ref):
    nv = CONV_VREGS
    for j in range(buf_ref.shape[0]):
        cols = slice(j * LANES, (j + 1) * LANES)
        v = [buf_ref[j, pl.ds(a, SUBLANES, stride=nv), :] for a in range(nv)]
        wrap = {a: pltpu.roll(v[a], 1, 0) for a in range(nv - (CONV_K - 1), nv)}
        coef = [jnp.broadcast_to(cw_ref[k:k + 1, cols], (SUBLANES, LANES)) for k in range(CONV_K)]
        bias = jnp.broadcast_to(cb_ref[:, cols], (SUBLANES, LANES))
        for a in range(nv):
            acc = bias + coef[CONV_K - 1] * v[a]
            for back in range(1, CONV_K):
                tap = v[a - back] if a >= back else wrap[a - back + nv]
                acc = acc + coef[CONV_K - 1 - back] * tap
            act_ref[j, pl.ds(a, SUBLANES, stride=nv), :] = _silu(acc)


def _act_cols(act_ref, col0, width):
    t0 = col0 // LANES
    tiles = [act_ref[t0 + j, CUR, :] for j in range(width // LANES)]
    return tiles[0] if len(tiles) == 1 else jnp.concatenate(tiles, axis=1)


def _ssd_scan_kernel(*refs, inner, heads, has_final, streams):
    (xs_ref, bc_ref, dt_ref, z_ref, x_ref, cw_ref, cb_ref, dtb_ref, alog_ref, drep_ref, sel_ref,
     gnw_ref, wout_ref) = refs[:13]
    fin_ref = refs[13] if has_final else None
    o_ref, h_ref, buf_ref, act_ref = refs[13 + has_final:]
    c = pl.program_id(1)
    L = CHUNK
    gn = SSD_GROUPS * SSD_STATE
    rep = heads // SSD_GROUPS
    gw = rep * SSD_HEAD_DIM
    S = range(streams)

    @pl.when(c == 0)
    def _():
        h_ref[...] = jnp.zeros(h_ref.shape, F32)

    _conv_tail(buf_ref, c == 0)
    for s in S:
        _conv_fill(buf_ref.at[s], xs_ref.at[s], 0)
        _conv_fill(buf_ref.at[s], bc_ref.at[s], inner)
    for s in S:
        _conv_silu(buf_ref.at[s], act_ref.at[s], cw_ref, cb_ref)

    lane = lax.broadcasted_iota(jnp.int32, (1, LANES), 1)
    a_neg = jnp.where(lane < heads, -jnp.exp(alog_ref[...]), 0.0)
    tri = _tril(L)
    cum2, cum2_t, dt_t, dend, e_hi, e_lo, w_hi, w_lo = [], [], [], [], [], [], [], []
    for s in S:
        dt = _softplus(dt_ref[s] + dtb_ref[...])
        cm = _hdot(tri.astype(F32), dt * a_neg)
        cum2.append(cm * LOG2E)
        cum2_t.append(cum2[s].T)
        dt_t.append(dt.T)
        cum_end = cm[L - 1:L, :]
        dend.append(jnp.exp(cum_end))
        hi, lo = _split_bf16(jnp.exp(cm))
        e_hi.append(hi)
        e_lo.append(lo)
        hi, lo = _split_bf16(jnp.exp(cum_end - cm) * dt)
        w_hi.append(hi)
        w_lo.append(lo)
    e_pieces = (jnp.concatenate(e_hi, axis=0), jnp.concatenate(e_lo, axis=0))
    w_pieces = (jnp.concatenate(w_hi, axis=0), jnp.concatenate(w_lo, axis=0))

    lane_g = lax.broadcasted_iota(jnp.int32, (L, gw), 1) // SSD_HEAD_DIM
    pend = []
    for g in range(SSD_GROUPS):
        rows = slice(g * gw, (g + 1) * gw)
        e_all = _expand(e_pieces, sel_ref[:, rows])
        w_all = _expand(w_pieces, sel_ref[:, rows])
        yn = []
        for s in S:
            srows = slice(s * L, (s + 1) * L)
            aref = act_ref.at[s]
            bg = _act_cols(aref, inner + g * SSD_STATE, SSD_STATE).astype(BF16)
            cg = _act_cols(aref, inner + gn + g * SSD_STATE, SSD_STATE).astype(BF16)
            cb = lax.dot_general(cg, bg, NT_DIMS, preferred_element_type=F32)
            xg = _act_cols(aref, g * gw, gw)
            mixes = []
            for r in range(rep):
                h = g * rep + r
                seg = cum2[s][:, h:h + 1] - cum2_t[s][h:h + 1, :]
                decay = jnp.exp2(jnp.where(tri, seg, -jnp.inf))
                mixes.append((cb * decay * dt_t[s][h:h + 1, :]).astype(BF16))
            mixcat = jnp.concatenate(mixes, axis=1)
            xgb = xg.astype(BF16)
            xblk = jnp.concatenate([jnp.where(lane_g == r, xgb, jnp.zeros_like(xgb)) for r in range(rep)],
                                   axis=0)
            y = jnp.dot(mixcat, xblk, preferred_element_type=F32)
            hg = h_ref[s, 0, rows, :]
            yi = lax.dot_general(cg, hg.astype(BF16), NT_DIMS, preferred_element_type=F32)
            y = y + yi * e_all[srows, :] + drep_ref[:, rows] * xg
            upd = jnp.dot((xg * w_all[srows, :]).T.astype(BF16), bg, preferred_element_type=F32)
            for r in range(rep):
                h = g * rep + r
                hr = slice(r * SSD_HEAD_DIM, (r + 1) * SSD_HEAD_DIM)
                h_ref[s, 0, g * gw + r * SSD_HEAD_DIM:g * gw + (r + 1) * SSD_HEAD_DIM, :] = (
                    dend[s][:, h:h + 1] * hg[hr, :] + upd[hr, :])
            y = y * _silu(z_ref[s, :, rows])
            y = y * lax.rsqrt(jnp.mean(y * y, axis=-1, keepdims=True) + EPS) * gnw_ref[:, rows]
            yn.append(y.astype(BF16))
        pend.append(jnp.concatenate(yn, axis=0))
        if len(pend) == OUT_GROUPS:
            g0 = g + 1 - OUT_GROUPS
            part = jnp.dot(jnp.concatenate(pend, axis=1), wout_ref[g0 * gw:(g + 1) * gw, :],
                           preferred_element_type=F32)
            pend = []
            for s in S:
                srows = slice(s * L, (s + 1) * L)
                if g0 == 0:
                    o_ref[s] = x_ref[s] + part[srows, :]
                else:
                    o_ref[s] += part[srows, :]
    if has_final:
        for s in S:
            o_ref[s] = _rms(o_ref[s], fin_ref[...])


def _ssd_scan(zx, x2d, bsz, seq, inner, heads, conv_w, conv_b, dt_bias, a_log, d_skip, gnorm_w,
              w_out_bf16, final_w):
    nc = seq // CHUNK
    d = x2d.shape[1]
    gn = SSD_GROUPS * SSD_STATE
    conv_dim = inner + 2 * gn
    assert inner // SSD_GROUPS == (heads // SSD_GROUPS) * SSD_HEAD_DIM
    xs_blk = inner // inner
    bc_blk = (2 * inner) // (2 * gn)
    dt_blk = (inner + conv_dim) // LANES
    pad = LANES - heads
    dtb = jnp.pad(dt_bias, (0, pad)).reshape(1, LANES)
    alog = jnp.pad(a_log, (0, pad)).reshape(1, LANES)
    drep = jnp.repeat(d_skip, SSD_HEAD_DIM).reshape(1, inner)
    sel = (jnp.arange(inner)[None, :] // SSD_HEAD_DIM == jnp.arange(LANES)[:, None]).astype(BF16)
    has_final = final_w is not None
    streams = 2 if bsz % 2 == 0 else 1
    bh = bsz // streams
    kern = functools.partial(_ssd_scan_kernel, inner=inner, heads=heads, has_final=has_final,
                             streams=streams)
    row = lambda b, c: b * nc + c
    c2 = lambda shape: pl.BlockSpec(shape, lambda b, c: (0, 0))
    zx3 = zx.reshape(streams, bh * seq, zx.shape[1])
    x3 = x2d.reshape(streams, bh * seq, d)
    ins = [zx3, zx3, zx3, zx3, x3, conv_w, conv_b.reshape(1, conv_dim), dtb, alog, drep, sel,
           gnorm_w.reshape(1, inner), w_out_bf16]
    specs = [pl.BlockSpec((streams, CHUNK, inner), lambda b, c: (0, row(b, c), xs_blk)),
             pl.BlockSpec((streams, CHUNK, 2 * gn), lambda b, c: (0, row(b, c), bc_blk)),
             pl.BlockSpec((streams, CHUNK, LANES), lambda b, c: (0, row(b, c), dt_blk)),
             pl.BlockSpec((streams, CHUNK, inner), lambda b, c: (0, row(b, c), 0)),
             pl.BlockSpec((streams, CHUNK, d), lambda b, c: (0, row(b, c), 0)),
             c2((CONV_K, conv_dim)), c2((1, conv_dim)), c2((1, LANES)), c2((1, LANES)),
             c2((1, inner)), c2((LANES, inner)), c2((1, inner)), c2((inner, d))]
    if has_final:
        ins.append(final_w.reshape(1, d))
        specs.append(c2((1, d)))
    out, h = pl.pallas_call(
        kern,
        name="ssd_scan",
        grid=(bh, nc),
        in_specs=specs,
        out_specs=[pl.BlockSpec((streams, CHUNK, d), lambda b, c: (0, row(b, c), 0)),
                   pl.BlockSpec((streams, 1, inner, SSD_STATE), lambda b, c: (0, b, 0, 0))],
        out_shape=[jax.ShapeDtypeStruct((streams, bh * seq, d), F32),
                   jax.ShapeDtypeStruct((streams, bh, inner, SSD_STATE), F32)],
        scratch_shapes=[pltpu.VMEM((streams, conv_dim // LANES, CONV_ROWS, LANES), F32),
                        pltpu.VMEM((streams, conv_dim // LANES, CONV_ROWS, LANES), F32)],
        compiler_params=_cparams("parallel", "arbitrary"),
    )(*ins)
    return out.reshape(bsz * seq, d), h.reshape(bsz, inner, SSD_STATE)


def _ssd_post_kernel(*refs, has_add, has_final):
    y_ref, z_ref, x_ref, gw_ref, w_ref = refs[:5]
    pos = 5
    add_ref = fin_ref = None
    if has_add:
        add_ref = refs[pos]
        pos += 1
    if has_final:
        fin_ref = refs[pos]
        pos += 1
    o_ref = refs[pos]
    inner = y_ref.shape[1]
    gwid = inner // SSD_GROUPS
    parts = []
    for g in range(SSD_GROUPS):
        cols = slice(g * gwid, (g + 1) * gwid)
        y = y_ref[:, cols]
        if has_add:
            y = y + add_ref[:, cols]
        y = y * _silu(z_ref[:, cols])
        y = y * lax.rsqrt(jnp.mean(y * y, axis=-1, keepdims=True) + EPS)
        parts.append((y * gw_ref[:, cols]).astype(BF16))
    out = x_ref[...] + jnp.dot(jnp.concatenate(parts, axis=1), w_ref[...],
                               preferred_element_type=F32)
    if has_final:
        out = _rms(out, fin_ref[...])
    o_ref[...] = out


def _ssd_post(y, zx, x2d, gnorm_w, w_out_bf16, add=None, final_w=None):
    m, inner = y.shape
    d = x2d.shape[1]
    tm = min(m, 512)
    ins = [y, zx, x2d, gnorm_w.reshape(1, inner), w_out_bf16]
    specs = [pl.BlockSpec((tm, inner), lambda i: (i, 0)),
             pl.BlockSpec((tm, inner), lambda i: (i, 0)),
             pl.BlockSpec((tm, d), lambda i: (i, 0)),
             pl.BlockSpec((1, inner), lambda i: (0, 0)),
             pl.BlockSpec((inner, d), lambda i: (0, 0))]
    if add is not None:
        ins.append(add)
        specs.append(pl.BlockSpec((tm, inner), lambda i: (i, 0)))
    if final_w is not None:
        ins.append(final_w.reshape(1, d))
        specs.append(pl.BlockSpec((1, d), lambda i: (0, 0)))
    kern = functools.partial(_ssd_post_kernel, has_add=add is not None, has_final=final_w is not None)
    return pl.pallas_call(
        kern,
        name="ssd_post",
        grid=(m // tm,),
        in_specs=specs,
        out_specs=pl.BlockSpec((tm, d), lambda i: (i, 0)),
        out_shape=jax.ShapeDtypeStruct((m, d), F32),
        compiler_params=_cparams("parallel"),
    )(*ins)


def _ssd_dec_pre_kernel(xbc_ref, dt_ref, c0_ref, cw_ref, cb_ref, dtb_ref, alog_ref, drep_ref,
                        cnew_ref, dtx_ref, darep_ref, b_ref, c_ref, dx_ref, *, inner, heads):
    gn = SSD_GROUPS * SSD_STATE
    cd = inner + 2 * gn
    xnew = xbc_ref[...]
    acc = cb_ref[...] + cw_ref[CONV_K - 1:CONV_K, :] * xnew
    for k in range(CONV_K - 1):
        acc = acc + cw_ref[k:k + 1, :] * c0_ref[:, k * cd:(k + 1) * cd]
    for k in range(CONV_K - 2):
        cnew_ref[:, k * cd:(k + 1) * cd] = c0_ref[:, (k + 1) * cd:(k + 2) * cd]
    cnew_ref[:, (CONV_K - 2) * cd:(CONV_K - 1) * cd] = xnew
    act = _silu(acc)
    xs = act[:, 0:inner]
    b_ref[...] = act[:, inner:inner + gn]
    c_ref[...] = act[:, inner + gn:inner + 2 * gn]
    lane = lax.broadcasted_iota(jnp.int32, (1, LANES), 1)
    dt = _softplus(dt_ref[...] + dtb_ref[...])
    a_neg = jnp.where(lane < heads, -jnp.exp(alog_ref[...]), 0.0)
    da = jnp.exp(dt * a_neg)
    sel = _expand_sel(LANES, SSD_HEAD_DIM, inner)
    dtx_ref[...] = _hdot(dt, sel) * xs
    darep_ref[...] = _hdot(da, sel)
    dx_ref[...] = drep_ref[...] * xs


def _ssd_dec_state_kernel(h_ref, dtxT_ref, daT_ref, b_ref, c_ref, hn_ref, yT_ref, *, bb, rep):
    i = pl.program_id(0)
    gw = rep * SSD_HEAD_DIM
    nb = yT_ref.shape[1]
    lane = lax.broadcasted_iota(jnp.int32, (gw, nb), 1)

    def body(bi, carry):
        bglob = i * bb + bi
        bmat = b_ref[bi]
        cmat = c_ref[bi]
        for g in range(SSD_GROUPS):
            rows = slice(g * gw, (g + 1) * gw)
            msk = lane == bglob
            dtx = jnp.sum(jnp.where(msk, dtxT_ref[rows, :], 0.0), axis=1, keepdims=True)
            dac = jnp.sum(jnp.where(msk, daT_ref[rows, :], 0.0), axis=1, keepdims=True)
            hnew = dac * h_ref[bi, rows, :] + dtx * bmat[g:g + 1, :]
            hn_ref[bi, rows, :] = hnew
            ycol = jnp.sum(hnew * cmat[g:g + 1, :], axis=1, keepdims=True)
            yT_ref[rows, :] = jnp.where(msk, ycol, yT_ref[rows, :])
        return carry

    lax.fori_loop(0, bb, body, 0)


def _ssd_decode(zx, x2d, ssm0, conv0, inner, heads, prm, w_out_bf16, final_w=None):
    norm_w, in_proj, conv_w, conv_b, dt_bias, a_log, d_skip, gnorm_w, out_proj = prm
    nb = zx.shape[0]
    gn = SSD_GROUPS * SSD_STATE
    cd = inner + 2 * gn
    pad = LANES - heads
    dtb = jnp.pad(dt_bias, (0, pad)).reshape(1, LANES)
    alog = jnp.pad(a_log, (0, pad)).reshape(1, LANES)
    drep = jnp.repeat(d_skip, SSD_HEAD_DIM).reshape(1, inner)
    full = lambda shape: pl.BlockSpec(shape, lambda i: tuple(0 for _ in shape))
    kern = functools.partial(_ssd_dec_pre_kernel, inner=inner, heads=heads)
    cnew, dtx, darep, bact, cact, dx = pl.pallas_call(
        kern,
        name="ssd_dec_pre",
        grid=(1,),
        in_specs=[full((nb, cd)), full((nb, LANES)), full((nb, (CONV_K - 1) * cd)), full((CONV_K, cd)), full((1, cd)),
                  full((1, LANES)), full((1, LANES)), full((1, inner))],
        out_specs=[full((nb, (CONV_K - 1) * cd)), full((nb, inner)), full((nb, inner)),
                   full((nb, gn)), full((nb, gn)), full((nb, inner))],
        out_shape=[jax.ShapeDtypeStruct((nb, (CONV_K - 1) * cd), F32),
                   jax.ShapeDtypeStruct((nb, inner), F32), jax.ShapeDtypeStruct((nb, inner), F32),
                   jax.ShapeDtypeStruct((nb, gn), F32), jax.ShapeDtypeStruct((nb, gn), F32),
                   jax.ShapeDtypeStruct((nb, inner), F32)],
        compiler_params=_cparams("arbitrary"),
    )(zx[:, inner:inner + cd], zx[:, inner + cd:inner + cd + LANES],
      conv0.reshape(nb, (CONV_K - 1) * cd), conv_w, conv_b.reshape(1, cd), dtb, alog, drep)

    bb = 4 if nb % 4 == 0 else 1
    rep = heads // SSD_GROUPS
    kern = functools.partial(_ssd_dec_state_kernel, bb=bb, rep=rep)
    hnew, y_t = pl.pallas_call(
        kern,
        name="ssd_dec_state",
        grid=(nb // bb,),
        in_specs=[pl.BlockSpec((bb, inner, SSD_STATE), lambda i: (i, 0, 0)),
                  pl.BlockSpec((inner, nb), lambda i: (0, 0)),
                  pl.BlockSpec((inner, nb), lambda i: (0, 0)),
                  pl.BlockSpec((bb, SSD_GROUPS, SSD_STATE), lambda i: (i, 0, 0)),
                  pl.BlockSpec((bb, SSD_GROUPS, SSD_STATE), lambda i: (i, 0, 0))],
        out_specs=[pl.BlockSpec((bb, inner, SSD_STATE), lambda i: (i, 0, 0)),
                   pl.BlockSpec((inner, nb), lambda i: (0, 0))],
        out_shape=[jax.ShapeDtypeStruct((nb, inner, SSD_STATE), F32),
                   jax.ShapeDtypeStruct((inner, nb), F32)],
        compiler_params=_cparams("arbitrary"),
    )(ssm0.reshape(nb, inner, SSD_STATE), dtx.T, darep.T,
      bact.reshape(nb, SSD_GROUPS, SSD_STATE), cact.reshape(nb, SSD_GROUPS, SSD_STATE))

    out = _ssd_post(y_t.T, zx, x2d, gnorm_w, w_out_bf16, add=dx, final_w=final_w)
    return out, hnew.reshape(nb, heads, SSD_HEAD_DIM, SSD_STATE), cnew.reshape(nb, CONV_K - 1, cd)


def _layernorm(v, w, b):
    mu = jnp.mean(v, axis=-1, keepdims=True)
    vc = v - mu
    return vc * lax.rsqrt(jnp.mean(vc * vc, axis=-1, keepdims=True) + EPS) * w + b


def _gmlp_kernel(u_ref, v_ref, z_ref, x_ref, lw_ref, lb_ref, ws_ref, sbT_ref, w_ref, o_ref, *, nck):
    inner = v_ref.shape[1]
    gd = inner // GM_GROUPS
    tri = _tril(CHUNK)
    for ck in range(nck):
        rows = slice(ck * CHUNK, (ck + 1) * CHUNK)
        vn = _layernorm(v_ref[rows, :], lw_ref[...], lb_ref[...])
        parts = []
        for g in range(GM_GROUPS):
            cols = slice(g * gd, (g + 1) * gd)
            wg = jnp.where(tri, ws_ref[g], 0.0)
            mixed = _bdot(wg, vn[:, cols]) + sbT_ref[:, g:g + 1]
            parts.append((u_ref[rows, cols] * mixed * _silu(z_ref[rows, cols])).astype(BF16))
        o_ref[rows, :] = x_ref[rows, :] + jnp.dot(jnp.concatenate(parts, axis=1), w_ref[...],
                                                  preferred_element_type=F32)


def _gmlp_prompt(uvz, x2d, inner, v_ln_w, v_ln_b, spatial_w, spatial_b, w_out_bf16):
    m, d = x2d.shape
    nck = 2 if (m // CHUNK) % 2 == 0 else 1
    tm = nck * CHUNK
    sb_t = jnp.pad(spatial_b.T, ((0, 0), (0, LANES - GM_GROUPS)))
    kern = functools.partial(_gmlp_kernel, nck=nck)
    return pl.pallas_call(
        kern,
        name="gmlp_prompt",
        grid=(m // tm,),
        in_specs=[pl.BlockSpec((tm, inner), lambda i: (i, 0)),
                  pl.BlockSpec((tm, inner), lambda i: (i, 1)),
                  pl.BlockSpec((tm, inner), lambda i: (i, 2)),
                  pl.BlockSpec((tm, d), lambda i: (i, 0)),
                  pl.BlockSpec((1, inner), lambda i: (0, 0)),
                  pl.BlockSpec((1, inner), lambda i: (0, 0)),
                  pl.BlockSpec((GM_GROUPS, CHUNK, CHUNK), lambda i: (0, 0, 0)),
                  pl.BlockSpec((CHUNK, LANES), lambda i: (0, 0)),
                  pl.BlockSpec((inner, d), lambda i: (0, 0))],
        out_specs=pl.BlockSpec((tm, d), lambda i: (i, 0)),
        out_shape=jax.ShapeDtypeStruct((m, d), F32),
        compiler_params=_cparams("parallel"),
    )(uvz, uvz, uvz, x2d, v_ln_w.reshape(1, inner), v_ln_b.reshape(1, inner), spatial_w, sb_t,
      w_out_bf16)


def _gmlp_dec_kernel(u_ref, v_ref, z_ref, x_ref, lw_ref, lb_ref, w00_ref, sb0_ref, w_ref,
                     o_ref, vn_ref):
    vn = _layernorm(v_ref[...], lw_ref[...], lb_ref[...])
    vn_ref[...] = vn
    mixed = w00_ref[...] * vn + sb0_ref[...]
    g = u_ref[...] * mixed * _silu(z_ref[...])
    o_ref[...] = x_ref[...] + _bdot(g, w_ref[...])


def _gmlp_decode(uvz, x2d, inner, v_ln_w, v_ln_b, spatial_w, spatial_b, w_out_bf16):
    nb, d = x2d.shape
    gd = inner // GM_GROUPS
    w00 = jnp.repeat(spatial_w[:, 0, 0], gd).reshape(1, inner)
    sb0 = jnp.repeat(spatial_b[:, 0], gd).reshape(1, inner)
    vec = pl.BlockSpec((1, inner), lambda i: (0, 0))
    return pl.pallas_call(
        _gmlp_dec_kernel,
        name="gmlp_dec",
        grid=(1,),
        in_specs=[pl.BlockSpec((nb, inner), lambda i: (0, 0)),
                  pl.BlockSpec((nb, inner), lambda i: (0, 1)),
                  pl.BlockSpec((nb, inner), lambda i: (0, 2)),
                  pl.BlockSpec((nb, d), lambda i: (0, 0)),
                  vec, vec, vec, vec,
                  pl.BlockSpec((inner, d), lambda i: (0, 0))],
        out_specs=[pl.BlockSpec((nb, d), lambda i: (0, 0)),
                   pl.BlockSpec((nb, inner), lambda i: (0, 0))],
        out_shape=[jax.ShapeDtypeStruct((nb, d), F32), jax.ShapeDtypeStruct((nb, inner), F32)],
        compiler_params=_cparams("arbitrary"),
    )(uvz, uvz, uvz, x2d, v_ln_w.reshape(1, inner), v_ln_b.reshape(1, inner), w00, sb0, w_out_bf16)


def _blockdiag_coefs(w):
    n = w.shape[0]
    rows = []
    for d in range(-(ML_BLOCK - 1), ML_BLOCK):
        cols = []
        for i in range(ML_BLOCK):
            j = i + d
            cols.append(w[:, j, i] if 0 <= j < ML_BLOCK else jnp.zeros((n,), w.dtype))
        rows.append(jnp.stack(cols, axis=1).reshape(n * ML_BLOCK))
    return jnp.stack(rows, axis=0)


def _blockdiag_apply(x, coef_refs):
    width = x.shape[1]
    outs = [None] * len(coef_refs)
    for di, d in enumerate(range(-(ML_BLOCK - 1), ML_BLOCK)):
        xs = x if d == 0 else pltpu.roll(x, (-d) % width, 1)
        for n, cref in enumerate(coef_refs):
            t = xs * cref[di:di + 1, :]
            outs[n] = t if outs[n] is None else outs[n] + t
    return outs


def _mlstm_scan_kernel(xm_ref, z_ref, x_ref, cw_ref, cb_ref, wqk_ref, wvo_ref, bo_ref,
                       wifq_ref, wifk_ref, wifv_ref, bif_ref, mhw_ref, skip_ref, wout_ref,
                       out_ref, cst_ref, nst_ref, mst_ref,
                       buf_ref, act_ref, q_ref, k_ref, v_ref, o_ref, *, heads):
    c = pl.program_id(1)
    L = CHUNK
    inner = xm_ref.shape[1]
    hd = inner // heads
    scale = hd ** -0.5

    @pl.when(c == 0)
    def _():
        cst_ref[...] = jnp.zeros(cst_ref.shape, F32)
        nst_ref[...] = jnp.zeros(nst_ref.shape, F32)
        mst_ref[...] = jnp.zeros(mst_ref.shape, F32)

    _conv_tail(buf_ref, c == 0)
    _conv_fill(buf_ref, xm_ref, 0)
    _conv_silu(buf_ref, act_ref, cw_ref, cb_ref)

    tw = wqk_ref.shape[1]
    gates = bif_ref[...]
    for t in range(inner // tw):
        cols = slice(t * tw, (t + 1) * tw)
        qk = _bdot(_act_cols(act_ref, t * tw, tw), wqk_ref[t])
        vo = _bdot(xm_ref[:, cols], wvo_ref[t])
        q_ref[:, cols] = qk[:, 0:tw]
        k_ref[:, cols] = qk[:, tw:2 * tw]
        v_ref[:, cols] = vo[:, 0:tw]
        o_ref[:, cols] = _sigmoid(vo[:, tw:2 * tw] + bo_ref[:, cols])
        gates = (gates + _bdot(qk[:, 0:tw], wifq_ref[cols, :]) + _bdot(qk[:, tw:2 * tw], wifk_ref[cols, :])
                 + _bdot(vo[:, 0:tw], wifv_ref[cols, :]))
    lf = -_softplus(-gates)
    tri = _tril(L)
    bt = _hdot(tri.astype(F32), lf)
    g_t = gates.T
    b_t = bt.T
    mvec = mst_ref[0]
    hcols = [slice(h * hd, (h + 1) * hd) for h in range(heads)]
    qb = [q_ref[:, cols].astype(BF16) for cols in hcols]
    kf = [k_ref[:, cols] * scale for cols in hcols]
    kb = [k.astype(BF16) for k in kf]
    cmat = [cst_ref[0, h] for h in range(heads)]
    s_qk = [lax.dot_general(qb[h], kb[h], NT_DIMS, preferred_element_type=F32) for h in range(heads)]
    cq = [lax.dot_general(qb[h], cmat[h].astype(BF16), NT_DIMS, preferred_element_type=F32)
          for h in range(heads)]
    bcol, igcol, inter, mt, w, wi = [], [], [], [], [], []
    for h in range(heads):
        bcol.append(bt[:, heads + h:heads + h + 1])
        igcol.append(gates[:, h:h + 1])
        brow = b_t[heads + h:heads + h + 1, :]
        igrow = g_t[h:h + 1, :]
        d = jnp.where(tri, bcol[h] - brow + igrow, -jnp.inf)
        inter.append(bcol[h] + mvec[:, h:h + 1])
        mt.append(jnp.maximum(inter[h], jnp.max(d, axis=1, keepdims=True)))
        w.append(jnp.exp(d - mt[h]) * s_qk[h])
        wi.append(jnp.exp(inter[h] - mt[h]))
    num = [wi[h] * cq[h] + _bdot(w[h], v_ref[:, hcols[h]]) for h in range(heads)]
    for h in range(heads):
        cols = hcols[h]
        nrow = nst_ref[0, h:h + 1, :]
        nq = jnp.sum(q_ref[:, cols] * nrow, axis=1, keepdims=True)
        den = wi[h] * nq + jnp.sum(w[h], axis=1, keepdims=True)
        den = jnp.maximum(jnp.abs(den), jnp.exp(-mt[h]))
        hh = o_ref[:, cols] * (num[h] / den)
        mu = jnp.mean(hh, axis=1, keepdims=True)
        hc = hh - mu
        hn = hc * lax.rsqrt(jnp.mean(hc * hc, axis=1, keepdims=True) + EPS)
        hn = hn * mhw_ref[:, cols] + skip_ref[:, cols] * _act_cols(act_ref, h * hd, hd)
        part = _bdot(hn * _silu(z_ref[:, cols]), wout_ref[cols, :])
        if h == 0:
            out_ref[...] = x_ref[...] + part
        else:
            out_ref[...] += part
    for h in range(heads):
        cols = hcols[h]
        m_new = mt[h][L - 1:L, :]
        we = jnp.exp(bcol[h][L - 1:L, :] - bcol[h] + igcol[h] - m_new)
        dp = jnp.exp(inter[h][L - 1:L, :] - m_new)
        cst_ref[0, h] = dp * cmat[h] + jnp.dot((we * v_ref[:, cols]).T.astype(BF16), kb[h],
                                               preferred_element_type=F32)
        nst_ref[0, h:h + 1, :] = dp * nst_ref[0, h:h + 1, :] + jnp.sum(we * kf[h], axis=0, keepdims=True)
        mst_ref[0, :, h:h + 1] = m_new


def _blockdiag_tiles(w, tile):
    nt = w.shape[0] * ML_BLOCK // tile
    rows = jnp.tile(w.reshape(nt, tile, ML_BLOCK), (1, 1, tile // ML_BLOCK))
    blk = jnp.arange(tile) // ML_BLOCK
    return jnp.where(blk[:, None] == blk[None, :], rows, 0.0)


def _mlstm_consts(prm, inner):
    (norm_w, in_proj, conv_w, conv_b, w_q, w_k, w_v, w_o, b_o, w_if, b_if, mh_norm_w, skip,
     out_proj) = prm
    heads = ML_HEADS
    padn = LANES - 2 * heads
    wif = jnp.pad(w_if, ((0, 0), (0, padn))).astype(BF16)
    bif = jnp.pad(b_if, (0, padn)).reshape(1, LANES)
    return dict(
        conv_w=conv_w, conv_b=conv_b.reshape(1, inner),
        cq=_blockdiag_coefs(w_q), ck=_blockdiag_coefs(w_k), cv=_blockdiag_coefs(w_v),
        co=_blockdiag_coefs(w_o), bo=b_o.reshape(1, inner),
        wqk=jnp.concatenate([_blockdiag_tiles(w_q, MXU_TILE), _blockdiag_tiles(w_k, MXU_TILE)],
                            axis=2).astype(BF16),
        wvo=jnp.concatenate([_blockdiag_tiles(w_v, MXU_TILE), _blockdiag_tiles(w_o, MXU_TILE)],
                            axis=2).astype(BF16),
        wifq=wif[0:inner], wifk=wif[inner:2 * inner], wifv=wif[2 * inner:3 * inner], bif=bif,
        mhw=mh_norm_w.reshape(1, inner), skip=skip.reshape(1, inner))


def _mlstm_scan(xmz, x2d, bsz, seq, inner, cst, w_out_bf16):
    nc = seq // CHUNK
    d = x2d.shape[1]
    heads = ML_HEADS
    hd = inner // heads
    ntile = inner // MXU_TILE
    row = lambda b, c: b * nc + c
    c2 = lambda shape: pl.BlockSpec(shape, lambda b, c: (0, 0))
    c3 = lambda shape: pl.BlockSpec(shape, lambda b, c: (0, 0, 0))
    kern = functools.partial(_mlstm_scan_kernel, heads=heads)
    return pl.pallas_call(
        kern,
        name="mlstm_scan",
        grid=(bsz, nc),
        in_specs=[pl.BlockSpec((CHUNK, inner), lambda b, c: (row(b, c), 0)),
                  pl.BlockSpec((CHUNK, inner), lambda b, c: (row(b, c), 1)),
                  pl.BlockSpec((CHUNK, d), lambda b, c: (row(b, c), 0)),
                  c2((CONV_K, inner)), c2((1, inner)),
                  c3((ntile, MXU_TILE, 2 * MXU_TILE)), c3((ntile, MXU_TILE, 2 * MXU_TILE)),
                  c2((1, inner)),
                  c2((inner, LANES)), c2((inner, LANES)), c2((inner, LANES)), c2((1, LANES)),
                  c2((1, inner)), c2((1, inner)), c2((inner, d))],
        out_specs=[pl.BlockSpec((CHUNK, d), lambda b, c: (row(b, c), 0)),
                   pl.BlockSpec((1, heads, hd, hd), lambda b, c: (b, 0, 0, 0)),
                   pl.BlockSpec((1, heads, hd), lambda b, c: (b, 0, 0)),
                   pl.BlockSpec((1, 1, heads), lambda b, c: (b, 0, 0))],
        out_shape=[jax.ShapeDtypeStruct((bsz * seq, d), F32),
                   jax.ShapeDtypeStruct((bsz, heads, hd, hd), F32),
                   jax.ShapeDtypeStruct((bsz, heads, hd), F32),
                   jax.ShapeDtypeStruct((bsz, 1, heads), F32)],
        scratch_shapes=[pltpu.VMEM((inner // LANES, CONV_ROWS, LANES), F32)] * 2
                       + [pltpu.VMEM((CHUNK, inner), F32)] * 4,
        compiler_params=_cparams("parallel", "arbitrary"),
    )(xmz, xmz, x2d, cst["conv_w"], cst["conv_b"], cst["wqk"], cst["wvo"], cst["bo"],
      cst["wifq"], cst["wifk"], cst["wifv"], cst["bif"], cst["mhw"], cst["skip"], w_out_bf16)


def _mlstm_dec_pre_kernel(xm_ref, c0_ref, n0_ref, m0_ref, cw_ref, cb_ref, cq_ref, ck_ref, cv_ref,
                          co_ref, bo_ref, wifq_ref, wifk_ref, wifv_ref, bif_ref,
                          cnew_ref, q_ref, k_ref, v_ref, og_ref, xc_ref, wev_ref, dprep_ref,
                          wrep_ref, denrep_ref, nnew_ref, mnew_ref, *, heads):
    inner = xm_ref.shape[1]
    nb = xm_ref.shape[0]
    hd = inner // heads
    xm = xm_ref[...]
    acc = cb_ref[...] + cw_ref[CONV_K - 1:CONV_K, :] * xm
    for kk in range(CONV_K - 1):
        acc = acc + cw_ref[kk:kk + 1, :] * c0_ref[:, kk * inner:(kk + 1) * inner]
    for kk in range(CONV_K - 2):
        cnew_ref[:, kk * inner:(kk + 1) * inner] = c0_ref[:, (kk + 1) * inner:(kk + 2) * inner]
    cnew_ref[:, (CONV_K - 2) * inner:(CONV_K - 1) * inner] = xm
    xc = _silu(acc)
    q, k = _blockdiag_apply(xc, [cq_ref, ck_ref])
    v, o_pre = _blockdiag_apply(xm, [cv_ref, co_ref])
    gates = _bdot(q, wifq_ref[...]) + _bdot(k, wifk_ref[...]) + _bdot(v, wifv_ref[...]) + bif_ref[...]
    ksc = k * (hd ** -0.5)
    lane = lax.broadcasted_iota(jnp.int32, (nb, LANES), 1)
    ig = gates
    lf = pltpu.roll(-_softplus(-gates), LANES - heads, 1)
    n0 = n0_ref[...]
    qk = jnp.zeros((nb, LANES), F32)
    nq = jnp.zeros((nb, LANES), F32)
    for h in range(heads):
        cols = slice(h * hd, (h + 1) * hd)
        qk = jnp.where(lane == h, jnp.sum(q[:, cols] * ksc[:, cols], axis=1, keepdims=True), qk)
        nq = jnp.where(lane == h, jnp.sum(q[:, cols] * n0[:, cols], axis=1, keepdims=True), nq)
    inter = lf + m0_ref[...]
    mt = jnp.maximum(inter, ig)
    wi = jnp.exp(inter - mt)
    we = jnp.exp(ig - mt)
    w = we * qk
    den = jnp.maximum(jnp.abs(wi * nq + w), jnp.exp(-mt))
    mnew_ref[...] = mt
    sel = _expand_sel(LANES, hd, inner)
    wi_rep = _hdot(wi, sel)
    we_rep = _hdot(we, sel)
    dprep_ref[...] = wi_rep
    wrep_ref[...] = _hdot(w, sel)
    denrep_ref[...] = _hdot(den, sel)
    wev_ref[...] = we_rep * v
    nnew_ref[...] = wi_rep * n0 + we_rep * ksc
    q_ref[...] = q
    k_ref[...] = ksc
    v_ref[...] = v
    og_ref[...] = _sigmoid(o_pre + bo_ref[...])
    xc_ref[...] = xc


def _mlstm_dec_state_kernel(c_ref, q_ref, k_ref, dp_ref, wevT_ref, cn_ref, cqT_ref, *, heads):
    b = pl.program_id(0)
    hd = c_ref.shape[2]
    nb = cqT_ref.shape[1]
    lane = lax.broadcasted_iota(jnp.int32, (hd, nb), 1)
    msk = lane == b
    for h in range(heads):
        rows = slice(h * hd, (h + 1) * hd)
        cmat = c_ref[0, h]
        cqcol = jnp.sum(cmat * q_ref[0, h:h + 1, :], axis=1, keepdims=True)
        wev = jnp.sum(jnp.where(msk, wevT_ref[rows, :], 0.0), axis=1, keepdims=True)
        cn_ref[0, h] = dp_ref[0, h:h + 1, :] * cmat + wev * k_ref[0, h:h + 1, :]
        cqT_ref[rows, :] = jnp.where(msk, cqcol, cqT_ref[rows, :])


def _mlstm_dec_post_kernel(cq_ref, v_ref, dp_ref, w_ref, den_ref, og_ref, xc_ref, z_ref,
                           mhw_ref, skip_ref, g_ref, *, heads):
    inner = cq_ref.shape[1]
    hd = inner // heads
    for h in range(heads):
        cols = slice(h * hd, (h + 1) * hd)
        num = dp_ref[:, cols] * cq_ref[:, cols] + w_ref[:, cols] * v_ref[:, cols]
        hh = og_ref[:, cols] * (num / den_ref[:, cols])
        mu = jnp.mean(hh, axis=1, keepdims=True)
        hc = hh - mu
        hn = hc * lax.rsqrt(jnp.mean(hc * hc, axis=1, keepdims=True) + EPS)
        hn = hn * mhw_ref[:, cols] + skip_ref[:, cols] * xc_ref[:, cols]
        g_ref[:, cols] = hn * _silu(z_ref[:, cols])


def _mlstm_decode(xmz, c0, n0, m0, conv0, inner, cst):
    nb = xmz.shape[0]
    heads = ML_HEADS
    hd = inner // heads
    nco = 2 * ML_BLOCK - 1
    full = lambda shape: pl.BlockSpec(shape, lambda i: tuple(0 for _ in shape))
    m0p = jnp.pad(m0, ((0, 0), (0, LANES - heads)))
    kern = functools.partial(_mlstm_dec_pre_kernel, heads=heads)
    big = jax.ShapeDtypeStruct((nb, inner), F32)
    outs = pl.pallas_call(
        kern,
        name="mlstm_dec_pre",
        grid=(1,),
        in_specs=[pl.BlockSpec((nb, inner), lambda i: (0, 0)),
                  full((nb, (CONV_K - 1) * inner)), full((nb, inner)), full((nb, LANES)),
                  full((CONV_K, inner)), full((1, inner)),
                  full((nco, inner)), full((nco, inner)), full((nco, inner)), full((nco, inner)),
                  full((1, inner)),
                  full((inner, LANES)), full((inner, LANES)), full((inner, LANES)), full((1, LANES))],
        out_specs=[full((nb, (CONV_K - 1) * inner))] + [full((nb, inner))] * 10 + [full((nb, LANES))],
        out_shape=[jax.ShapeDtypeStruct((nb, (CONV_K - 1) * inner), F32)] + [big] * 10
                  + [jax.ShapeDtypeStruct((nb, LANES), F32)],
        compiler_params=_cparams("arbitrary"),
    )(xmz, conv0.reshape(nb, (CONV_K - 1) * inner), n0.reshape(nb, inner), m0p,
      cst["conv_w"], cst["conv_b"], cst["cq"], cst["ck"], cst["cv"], cst["co"], cst["bo"],
      cst["wifq"], cst["wifk"], cst["wifv"], cst["bif"])
    cnew, q, ksc, v, og, xc, wev, dprep, wrep, denrep, nnew, mnew = outs

    kern = functools.partial(_mlstm_dec_state_kernel, heads=heads)
    h3 = lambda: pl.BlockSpec((1, heads, hd), lambda i: (i, 0, 0))
    c_new, cq_t = pl.pallas_call(
        kern,
        name="mlstm_dec_state",
        grid=(nb,),
        in_specs=[pl.BlockSpec((1, heads, hd, hd), lambda i: (i, 0, 0, 0)),
                  h3(), h3(), h3(),
                  pl.BlockSpec((inner, nb), lambda i: (0, 0))],
        out_specs=[pl.BlockSpec((1, heads, hd, hd), lambda i: (i, 0, 0, 0)),
                   pl.BlockSpec((inner, nb), lambda i: (0, 0))],
        out_shape=[jax.ShapeDtypeStruct((nb, heads, hd, hd), F32),
                   jax.ShapeDtypeStruct((inner, nb), F32)],
        compiler_params=_cparams("arbitrary"),
    )(c0, q.reshape(nb, heads, hd), ksc.reshape(nb, heads, hd), dprep.reshape(nb, heads, hd), wev.T)

    kern = functools.partial(_mlstm_dec_post_kernel, heads=heads)
    g = pl.pallas_call(
        kern,
        name="mlstm_dec_post",
        grid=(1,),
        in_specs=[full((nb, inner))] * 7
                 + [pl.BlockSpec((nb, inner), lambda i: (0, 1)), full((1, inner)), full((1, inner))],
        out_specs=full((nb, inner)),
        out_shape=big,
        compiler_params=_cparams("arbitrary"),
    )(cq_t.T, v, dprep, wrep, denrep, og, xc, xmz, cst["mhw"], cst["skip"])
    return (g, c_new, nnew.reshape(nb, heads, hd), mnew[:, 0:heads],
            cnew.reshape(nb, CONV_K - 1, inner))


def _ssd_in_weights(in_proj, inner, conv_dim):
    ncol = inner + conv_dim
    pad = LANES - (in_proj.shape[1] - ncol)
    return jnp.pad(in_proj, ((0, 0), (0, pad))).astype(BF16)


def _ssd_layer(xp, xs, ssm0, conv0, prm, final_w=None):
    norm_w, in_proj, conv_w, conv_b, dt_bias, a_log, d_skip, gnorm_w, out_proj = prm
    bsz, seq, d = xp.shape
    nb = xs.shape[0]
    inner = out_proj.shape[0]
    heads = a_log.shape[0]
    conv_dim = conv_w.shape[1]
    w_in = _ssd_in_weights(in_proj, inner, conv_dim)
    w_out = out_proj.astype(BF16)
    xp2 = xp.reshape(bsz * seq, d)
    xs2 = xs.reshape(nb, d)

    zx_p = _norm_matmul(xp2, norm_w, w_in)
    out_p, h_p = _ssd_scan(zx_p, xp2, bsz, seq, inner, heads, conv_w, conv_b, dt_bias, a_log, d_skip,
                           gnorm_w, w_out, final_w)
    conv_p = zx_p.reshape(bsz, seq, -1)[:, seq - (CONV_K - 1):, inner:inner + conv_dim]

    zx_s = _norm_matmul(xs2, norm_w, w_in)
    out_s, h_s, conv_s = _ssd_decode(zx_s, xs2, ssm0, conv0, inner, heads, prm, w_out, final_w=final_w)
    return (out_p.reshape(bsz, seq, d), out_s.reshape(nb, 1, d),
            h_p.reshape(bsz, heads, SSD_HEAD_DIM, SSD_STATE), conv_p, h_s, conv_s)


def _gmlp_layer(xp, xs, prm):
    norm_w, in_proj, v_ln_w, v_ln_b, spatial_w, spatial_b, out_proj = prm
    bsz, seq, d = xp.shape
    nb = xs.shape[0]
    inner = out_proj.shape[0]
    w_in = in_proj.astype(BF16)
    w_out = out_proj.astype(BF16)
    xp2 = xp.reshape(bsz * seq, d)
    xs2 = xs.reshape(nb, d)
    uvz_p = _norm_matmul(xp2, norm_w, w_in)
    out_p = _gmlp_prompt(uvz_p, xp2, inner, v_ln_w, v_ln_b, spatial_w, spatial_b, w_out)
    uvz_s = _norm_matmul(xs2, norm_w, w_in)
    out_s, vn_s = _gmlp_decode(uvz_s, xs2, inner, v_ln_w, v_ln_b, spatial_w, spatial_b, w_out)
    return out_p.reshape(bsz, seq, d), out_s.reshape(nb, 1, d), vn_s.reshape(nb, 1, inner)


def _mlstm_layer(xp, xs, c0, n0, m0, conv0, prm):
    norm_w, in_proj = prm[0], prm[1]
    out_proj = prm[-1]
    bsz, seq, d = xp.shape
    nb = xs.shape[0]
    inner = out_proj.shape[0]
    w_in = in_proj.astype(BF16)
    w_out = out_proj.astype(BF16)
    cst = _mlstm_consts(prm, inner)
    xp2 = xp.reshape(bsz * seq, d)
    xs2 = xs.reshape(nb, d)

    xmz_p = _norm_matmul(xp2, norm_w, w_in)
    out_p, c_p, n_p, m_p = _mlstm_scan(xmz_p, xp2, bsz, seq, inner, cst, w_out)
    conv_p = xmz_p.reshape(bsz, seq, -1)[:, seq - (CONV_K - 1):, 0:inner]

    xmz_s = _norm_matmul(xs2, norm_w, w_in)
    g_s, c_s, n_s, m_s, conv_s = _mlstm_decode(xmz_s, c0, n0, m0, conv0, inner, cst)
    out_s = _matmul_res(g_s, w_out, xs2)
    return (out_p.reshape(bsz, seq, d), out_s.reshape(nb, 1, d),
            c_p, n_p, m_p.reshape(bsz, ML_HEADS), conv_p, c_s, n_s, m_s, conv_s)


def kernel(x_prompt, x_sample, state_l0_ssm, state_l0_conv, state_l2_C, state_l2_n, state_l2_m, state_l2_conv, state_l3_ssm, state_l3_conv, l0_norm_w, l0_in_proj, l0_conv_w, l0_conv_b, l0_dt_bias, l0_A_log, l0_D_skip, l0_gnorm_w, l0_out_proj, l1_norm_w, l1_in_proj, l1_v_ln_w, l1_v_ln_b, l1_spatial_w, l1_spatial_b, l1_out_proj, l2_norm_w, l2_in_proj, l2_conv_w, l2_conv_b, l2_w_q, l2_w_k, l2_w_v, l2_w_o, l2_b_o, l2_w_if, l2_b_if, l2_mh_norm_w, l2_skip, l2_out_proj, l3_norm_w, l3_in_proj, l3_conv_w, l3_conv_b, l3_dt_bias, l3_A_log, l3_D_skip, l3_gnorm_w, l3_out_proj, final_norm_w):
    p0 = (l0_norm_w, l0_in_proj, l0_conv_w, l0_conv_b, l0_dt_bias, l0_A_log, l0_D_skip, l0_gnorm_w, l0_out_proj)
    p1 = (l1_norm_w, l1_in_proj, l1_v_ln_w, l1_v_ln_b, l1_spatial_w, l1_spatial_b, l1_out_proj)
    p2 = (l2_norm_w, l2_in_proj, l2_conv_w, l2_conv_b, l2_w_q, l2_w_k, l2_w_v, l2_w_o, l2_b_o,
          l2_w_if, l2_b_if, l2_mh_norm_w, l2_skip, l2_out_proj)
    p3 = (l3_norm_w, l3_in_proj, l3_conv_w, l3_conv_b, l3_dt_bias, l3_A_log, l3_D_skip, l3_gnorm_w, l3_out_proj)

    hp, hs, p0_ssm, p0_conv, s0_ssm, s0_conv = _ssd_layer(x_prompt, x_sample, state_l0_ssm, state_l0_conv, p0)
    hp, hs, s1_v = _gmlp_layer(hp, hs, p1)
    hp, hs, p2_C, p2_n, p2_m, p2_conv, s2_C, s2_n, s2_m, s2_conv = _mlstm_layer(
        hp, hs, state_l2_C, state_l2_n, state_l2_m, state_l2_conv, p2)
    y_prompt, y_sample, p3_ssm, p3_conv, s3_ssm, s3_conv = _ssd_layer(
        hp, hs, state_l3_ssm, state_l3_conv, p3, final_w=final_norm_w)
    return (y_prompt, y_sample,
            p0_ssm, p0_conv, s0_ssm, s0_conv,
            s1_v,
            p2_C, p2_n, p2_m, p2_conv, s2_C, s2_n, s2_m, s2_conv,
            p3_ssm, p3_conv, s3_ssm, s3_conv)
```

```python
import functools
import math

import jax
import jax.numpy as jnp
from jax import lax
from jax.experimental import pallas as pl
from jax.experimental.pallas import tpu as pltpu

F32 = jnp.float32
BF16 = jnp.bfloat16
EPS = 1e-6
CONV_K = 4
CHUNK = 128
LANES = 128
MXU_TILE = 256
SUBLANES = 8
SSD_HEAD_DIM = 64
SSD_STATE = 128
SSD_GROUPS = 8
OUT_GROUPS = 2
ML_HEADS = 4
ML_BLOCK = 4
GM_GROUPS = 8
VMEM_LIMIT = 56 * 1024 * 1024
HI = lax.Precision.HIGHEST
LOG2E = 1.4426950408889634
NT_DIMS = (((1,), (1,)), ((), ()))


def _cparams(*sem):
    return pltpu.CompilerParams(dimension_semantics=sem, vmem_limit_bytes=VMEM_LIMIT)


def _sigmoid(x):
    return 1.0 / (1.0 + jnp.exp2(x * (-LOG2E)))


def _silu(x):
    h = 0.5 * x
    return h + h * jnp.tanh(h)


def _softplus(x):
    return jnp.maximum(x, 0.0) + jnp.log(1.0 + jnp.exp(-jnp.abs(x)))


def _bdot(a, b):
    return jnp.dot(a.astype(BF16), b.astype(BF16), preferred_element_type=F32)


def _bdot_nt(a, b):
    return lax.dot_general(a.astype(BF16), b.astype(BF16), NT_DIMS, preferred_element_type=F32)


def _hdot(a, b):
    return jnp.dot(a, b, precision=HI, preferred_element_type=F32)


def _rms(x, w):
    return x * lax.rsqrt(jnp.mean(x * x, axis=-1, keepdims=True) + EPS) * w


def _tril(n):
    r = lax.broadcasted_iota(jnp.int32, (n, n), 0)
    c = lax.broadcasted_iota(jnp.int32, (n, n), 1)
    return r >= c


def _expand_sel(n_in, width, n_out):
    r = lax.broadcasted_iota(jnp.int32, (n_in, n_out), 0)
    c = lax.broadcasted_iota(jnp.int32, (n_in, n_out), 1)
    return (c // width == r).astype(F32)


def _split_bf16(x):
    hi = x.astype(BF16)
    return hi, (x - hi.astype(F32)).astype(BF16)


def _expand(pieces, sel_bf16):
    return (jnp.dot(pieces[0], sel_bf16, preferred_element_type=F32)
            + jnp.dot(pieces[1], sel_bf16, preferred_element_type=F32))


def _norm_matmul_kernel(x_ref, nw_ref, w_ref, o_ref):
    o_ref[...] = _bdot(_rms(x_ref[...], nw_ref[...]), w_ref[...])


def _norm_matmul(x2d, norm_w, w_bf16):
    m, k = x2d.shape
    n = w_bf16.shape[1]
    tm = min(m, 256)
    return pl.pallas_call(
        _norm_matmul_kernel,
        name="norm_matmul",
        grid=(m // tm,),
        in_specs=[pl.BlockSpec((tm, k), lambda i: (i, 0)),
                  pl.BlockSpec((1, k), lambda i: (0, 0)),
                  pl.BlockSpec((k, n), lambda i: (0, 0))],
        out_specs=pl.BlockSpec((tm, n), lambda i: (i, 0)),
        out_shape=jax.ShapeDtypeStruct((m, n), F32),
        compiler_params=_cparams("parallel"),
    )(x2d, norm_w.reshape(1, k), w_bf16)


def _matmul_res_kernel(g_ref, w_ref, x_ref, o_ref):
    o_ref[...] = x_ref[...] + _bdot(g_ref[...], w_ref[...])


def _matmul_res(g2d, w_bf16, x2d):
    m, k = g2d.shape
    n = w_bf16.shape[1]
    tm = min(m, 512)
    return pl.pallas_call(
        _matmul_res_kernel,
        name="matmul_res",
        grid=(m // tm,),
        in_specs=[pl.BlockSpec((tm, k), lambda i: (i, 0)),
                  pl.BlockSpec((k, n), lambda i: (0, 0)),
                  pl.BlockSpec((tm, n), lambda i: (i, 0))],
        out_specs=pl.BlockSpec((tm, n), lambda i: (i, 0)),
        out_shape=jax.ShapeDtypeStruct((m, n), F32),
        compiler_params=_cparams("parallel"),
    )(g2d, w_bf16, x2d)


CONV_ROWS = CHUNK + SUBLANES
CONV_VREGS = CONV_ROWS // SUBLANES
CUR = slice(SUBLANES, CONV_ROWS)


def _conv_tail(buf_ref, is_first):
    lead = (slice(None),) * (len(buf_ref.shape) - 2)

    @pl.when(is_first)
    def _():
        buf_ref[lead + (slice(0, SUBLANES), slice(None))] = jnp.zeros(
            buf_ref.shape[:-2] + (SUBLANES, LANES), F32)

    @pl.when(jnp.logical_not(is_first))
    def _():
        buf_ref[lead + (slice(0, SUBLANES), slice(None))] = buf_ref[lead + (slice(CHUNK, CONV_ROWS), slice(None))]


def _conv_fill(buf_ref, src_ref, col0):
    t0 = col0 // LANES
    for j in range(src_ref.shape[1] // LANES):
        buf_ref[t0 + j, CUR, :] = src_ref[:, j * LANES:(j + 1) * LANES]


def _conv_silu(buf_ref, act_ref, cw_ref, cb_ref):
    nv = CONV_VREGS
    for j in range(buf_ref.shape[0]):
        cols = slice(j * LANES, (j + 1) * LANES)
        v = [buf_ref[j, pl.ds(a, SUBLANES, stride=nv), :] for a in range(nv)]
        wrap = {a: pltpu.roll(v[a], 1, 0) for a in range(nv - (CONV_K - 1), nv)}
        coef = [jnp.broadcast_to(cw_ref[k:k + 1, cols], (SUBLANES, LANES)) for k in range(CONV_K)]
        bias = jnp.broadcast_to(cb_ref[:, cols], (SUBLANES, LANES))
        for a in range(nv):
            acc = bias + coef[CONV_K - 1] * v[a]
            for back in range(1, CONV_K):
                tap = v[a - back] if a >= back else wrap[a - back + nv]
                acc = acc + coef[CONV_K - 1 - back] * tap
            act_ref[j, pl.ds(a, SUBLANES, stride=nv), :] = _silu(acc)


def _act_cols(act_ref, col0, width):
    t0 = col0 // LANES
    tiles = [act_ref[t0 + j, CUR, :] for j in range(width // LANES)]
    return tiles[0] if len(tiles) == 1 else jnp.concatenate(tiles, axis=1)


def _ssd_scan_kernel(*refs, inner, heads, has_final, streams):
    (xs_ref, bc_ref, dt_ref, z_ref, x_ref, cw_ref, cb_ref, dtb_ref, alog_ref, drep_ref, sel_ref,
     gnw_ref, wout_ref) = refs[:13]
    fin_ref = refs[13] if has_final else None
    o_ref, h_ref, buf_ref, act_ref = refs[13 + has_final:]
    c = pl.program_id(1)
    L = CHUNK
    gn = SSD_GROUPS * SSD_STATE
    rep = heads // SSD_GROUPS
    gw = rep * SSD_HEAD_DIM
    S = range(streams)

    @pl.when(c == 0)
    def _():
        h_ref[...] = jnp.zeros(h_ref.shape, F32)

    _conv_tail(buf_ref, c == 0)
    for s in S:
        _conv_fill(buf_ref.at[s], xs_ref.at[s], 0)
        _conv_fill(buf_ref.at[s], bc_ref.at[s], inner)
    for s in S:
        _conv_silu(buf_ref.at[s], act_ref.at[s], cw_ref, cb_ref)

    lane = lax.broadcasted_iota(jnp.int32, (1, LANES), 1)
    a_neg = jnp.where(lane < heads, -jnp.exp(alog_ref[...]), 0.0)
    tri = _tril(L)
    cum2, cum2_t, dt_t, dend, e_hi, e_lo, w_hi, w_lo = [], [], [], [], [], [], [], []
    for s in S:
        dt = _softplus(dt_ref[s] + dtb_ref[...])
        cm = _hdot(tri.astype(F32), dt * a_neg)
        cum2.append(cm * LOG2E)
        cum2_t.append(cum2[s].T)
        dt_t.append(dt.T)
        cum_end = cm[L - 1:L, :]
        dend.append(jnp.exp(cum_end))
        hi, lo = _split_bf16(jnp.exp(cm))
        e_hi.append(hi)
        e_lo.append(lo)
        hi, lo = _split_bf16(jnp.exp(cum_end - cm) * dt)
        w_hi.append(hi)
        w_lo.append(lo)
    e_pieces = (jnp.concatenate(e_hi, axis=0), jnp.concatenate(e_lo, axis=0))
    w_pieces = (jnp.concatenate(w_hi, axis=0), jnp.concatenate(w_lo, axis=0))

    lane_g = lax.broadcasted_iota(jnp.int32, (L, gw), 1) // SSD_HEAD_DIM
    pend = []
    for g in range(SSD_GROUPS):
        rows = slice(g * gw, (g + 1) * gw)
        e_all = _expand(e_pieces, sel_ref[:, rows])
        w_all = _expand(w_pieces, sel_ref[:, rows])
        yn = []
        for s in S:
            srows = slice(s * L, (s + 1) * L)
            aref = act_ref.at[s]
            bg = _act_cols(aref, inner + g * SSD_STATE, SSD_STATE).astype(BF16)
            cg = _act_cols(aref, inner + gn + g * SSD_STATE, SSD_STATE).astype(BF16)
            cb = lax.dot_general(cg, bg, NT_DIMS, preferred_element_type=F32)
            xg = _act_cols(aref, g * gw, gw)
            mixes = []
            for r in range(rep):
                h = g * rep + r
                seg = cum2[s][:, h:h + 1] - cum2_t[s][h:h + 1, :]
                decay = jnp.exp2(jnp.where(tri, seg, -jnp.inf))
                mixes.append((cb * decay * dt_t[s][h:h + 1, :]).astype(BF16))
            mixcat = jnp.concatenate(mixes, axis=1)
            xgb = xg.astype(BF16)
            xblk = jnp.concatenate([jnp.where(lane_g == r, xgb, jnp.zeros_like(xgb)) for r in range(rep)],
                                   axis=0)
            y = jnp.dot(mixcat, xblk, preferred_element_type=F32)
            hg = h_ref[s, 0, rows, :]
            yi = lax.dot_general(cg, hg.astype(BF16), NT_DIMS, preferred_element_type=F32)
            y = y + yi * e_all[srows, :] + drep_ref[:, rows] * xg
            upd = jnp.dot((xg * w_all[srows, :]).T.astype(BF16), bg, preferred_element_type=F32)
            for r in range(rep):
                h = g * rep + r
                hr = slice(r * SSD_HEAD_DIM, (r + 1) * SSD_HEAD_DIM)
                h_ref[s, 0, g * gw + r * SSD_HEAD_DIM:g * gw + (r + 1) * SSD_HEAD_DIM, :] = (
                    dend[s][:, h:h + 1] * hg[hr, :] + upd[hr, :])
            y = y * _silu(z_ref[s, :, rows])
            y = y * lax.rsqrt(jnp.mean(y * y, axis=-1, keepdims=True) + EPS) * gnw_ref[:, rows]
            yn.append(y.astype(BF16))
        pend.append(jnp.concatenate(yn, axis=0))
        if len(pend) == OUT_GROUPS:
            g0 = g + 1 - OUT_GROUPS
            part = jnp.dot(jnp.concatenate(pend, axis=1), wout_ref[g0 * gw:(g + 1) * gw, :],
                           preferred_element_type=F32)
            pend = []
            for s in S:
                srows = slice(s * L, (s + 1) * L)
                if g0 == 0:
                    o_ref[s] = x_ref[s] + part[srows, :]
                else:
                    o_ref[s] += part[srows, :]
    if has_final:
        for s in S:
            o_ref[s] = _rms(o_ref[s], fin_ref[...])


def _ssd_scan(zx, x2d, bsz, seq, inner, heads, conv_w, conv_b, dt_bias, a_log, d_skip, gnorm_w,
              w_out_bf16, final_w):
    nc = seq // CHUNK
    d = x2d.shape[1]
    gn = SSD_GROUPS * SSD_STATE
    conv_dim = inner + 2 * gn
    assert inner // SSD_GROUPS == (heads // SSD_GROUPS) * SSD_HEAD_DIM
    xs_blk = inner // inner
    bc_blk = (2 * inner) // (2 * gn)
    dt_blk = (inner + conv_dim) // LANES
    pad = LANES - heads
    dtb = jnp.pad(dt_bias, (0, pad)).reshape(1, LANES)
    alog = jnp.pad(a_log, (0, pad)).reshape(1, LANES)
    drep = jnp.repeat(d_skip, SSD_HEAD_DIM).reshape(1, inner)
    sel = (jnp.arange(inner)[None, :] // SSD_HEAD_DIM == jnp.arange(LANES)[:, None]).astype(BF16)
    has_final = final_w is not None
    streams = 2 if bsz % 2 == 0 else 1
    bh = bsz // streams
    kern = functools.partial(_ssd_scan_kernel, inner=inner, heads=heads, has_final=has_final,
                             streams=streams)
    row = lambda b, c: b * nc + c
    c2 = lambda shape: pl.BlockSpec(shape, lambda b, c: (0, 0))
    zx3 = zx.reshape(streams, bh * seq, zx.shape[1])
    x3 = x2d.reshape(streams, bh * seq, d)
    ins = [zx3, zx3, zx3, zx3, x3, conv_w, conv_b.reshape(1, conv_dim), dtb, alog, drep, sel,
           gnorm_w.reshape(1, inner), w_out_bf16]
    specs = [pl.BlockSpec((streams, CHUNK, inner), lambda b, c: (0, row(b, c), xs_blk)),
             pl.BlockSpec((streams, CHUNK, 2 * gn), lambda b, c: (0, row(b, c), bc_blk)),
             pl.BlockSpec((streams, CHUNK, LANES), lambda b, c: (0, row(b, c), dt_blk)),
             pl.BlockSpec((streams, CHUNK, inner), lambda b, c: (0, row(b, c), 0)),
             pl.BlockSpec((streams, CHUNK, d), lambda b, c: (0, row(b, c), 0)),
             c2((CONV_K, conv_dim)), c2((1, conv_dim)), c2((1, LANES)), c2((1, LANES)),
             c2((1, inner)), c2((LANES, inner)), c2((1, inner)), c2((inner, d))]
    if has_final:
        ins.append(final_w.reshape(1, d))
        specs.append(c2((1, d)))
    out, h = pl.pallas_call(
        kern,
        name="ssd_scan",
        grid=(bh, nc),
        in_specs=specs,
        out_specs=[pl.BlockSpec((streams, CHUNK, d), lambda b, c: (0, row(b, c), 0)),
                   pl.BlockSpec((streams, 1, inner, SSD_STATE), lambda b, c: (0, b, 0, 0))],
        out_shape=[jax.ShapeDtypeStruct((streams, bh * seq, d), F32),
                   jax.ShapeDtypeStruct((streams, bh, inner, SSD_STATE), F32)],
        scratch_shapes=[pltpu.VMEM((streams, conv_dim // LANES, CONV_ROWS, LANES), F32),
                        pltpu.VMEM((streams, conv_dim // LANES, CONV_ROWS, LANES), F32)],
        compiler_params=_cparams("parallel", "arbitrary"),
    )(*ins)
    return out.reshape(bsz * seq, d), h.reshape(bsz, inner, SSD_STATE)


def _ssd_post_kernel(*refs, has_add, has_final):
    y_ref, z_ref, x_ref, gw_ref, w_ref = refs[:5]
    pos = 5
    add_ref = fin_ref = None
    if has_add:
        add_ref = refs[pos]
        pos += 1
    if has_final:
        fin_ref = refs[pos]
        pos += 1
    o_ref = refs[pos]
    inner = y_ref.shape[1]
    gwid = inner // SSD_GROUPS
    parts = []
    for g in range(SSD_GROUPS):
        cols = slice(g * gwid, (g + 1) * gwid)
        y = y_ref[:, cols]
        if has_add:
            y = y + add_ref[:, cols]
        y = y * _silu(z_ref[:, cols])
        y = y * lax.rsqrt(jnp.mean(y * y, axis=-1, keepdims=True) + EPS)
        parts.append((y * gw_ref[:, cols]).astype(BF16))
    out = x_ref[...] + jnp.dot(jnp.concatenate(parts, axis=1), w_ref[...],
                               preferred_element_type=F32)
    if has_final:
        out = _rms(out, fin_ref[...])
    o_ref[...] = out


def _ssd_post(y, zx, x2d, gnorm_w, w_out_bf16, add=None, final_w=None):
    m, inner = y.shape
    d = x2d.shape[1]
    tm = min(m, 512)
    ins = [y, zx, x2d, gnorm_w.reshape(1, inner), w_out_bf16]
    specs = [pl.BlockSpec((tm, inner), lambda i: (i, 0)),
             pl.BlockSpec((tm, inner), lambda i: (i, 0)),
             pl.BlockSpec((tm, d), lambda i: (i, 0)),
             pl.BlockSpec((1, inner), lambda i: (0, 0)),
             pl.BlockSpec((inner, d), lambda i: (0, 0))]
    if add is not None:
        ins.append(add)
        specs.append(pl.BlockSpec((tm, inner), lambda i: (i, 0)))
    if final_w is not None:
        ins.append(final_w.reshape(1, d))
        specs.append(pl.BlockSpec((1, d), lambda i: (0, 0)))
    kern = functools.partial(_ssd_post_kernel, has_add=add is not None, has_final=final_w is not None)
    return pl.pallas_call(
        kern,
        name="ssd_post",
        grid=(m // tm,),
        in_specs=specs,
        out_specs=pl.BlockSpec((tm, d), lambda i: (i, 0)),
        out_shape=jax.ShapeDtypeStruct((m, d), F32),
        compiler_params=_cparams("parallel"),
    )(*ins)


def _ssd_dec_pre_kernel(xbc_ref, dt_ref, c0_ref, cw_ref, cb_ref, dtb_ref, alog_ref, drep_ref,
                        cnew_ref, dtx_ref, da_ref, b_ref, c_ref, dx_ref, *, inner, heads):
    gn = SSD_GROUPS * SSD_STATE
    cd = inner + 2 * gn
    xnew = xbc_ref[...]
    acc = cb_ref[...] + cw_ref[CONV_K - 1:CONV_K, :] * xnew
    for k in range(CONV_K - 1):
        acc = acc + cw_ref[k:k + 1, :] * c0_ref[:, k * cd:(k + 1) * cd]
    for k in range(CONV_K - 2):
        cnew_ref[:, k * cd:(k + 1) * cd] = c0_ref[:, (k + 1) * cd:(k + 2) * cd]
    cnew_ref[:, (CONV_K - 2) * cd:(CONV_K - 1) * cd] = xnew
    act = _silu(acc)
    xs = act[:, 0:inner]
    b_ref[...] = act[:, inner:inner + gn]
    c_ref[...] = act[:, inner + gn:inner + 2 * gn]
    lane = lax.broadcasted_iota(jnp.int32, (1, LANES), 1)
    dt = _softplus(dt_ref[...] + dtb_ref[...])
    a_neg = jnp.where(lane < heads, -jnp.exp(alog_ref[...]), 0.0)
    da_ref[...] = jnp.exp(dt * a_neg)
    dtx_ref[...] = _hdot(dt, _expand_sel(LANES, SSD_HEAD_DIM, inner)) * xs
    dx_ref[...] = drep_ref[...] * xs


def _ssd_dec_state_kernel(da_ref, h_ref, dtxT_ref, b_ref, c_ref, hn_ref, yT_ref, *, bb, rep):
    i = pl.program_id(0)
    gw = rep * SSD_HEAD_DIM
    nb = yT_ref.shape[1]
    lane = lax.broadcasted_iota(jnp.int32, (gw, nb), 1)

    def body(bi, carry):
        bglob = i * bb + bi
        bmat = b_ref[bi]
        cmat = c_ref[bi]
        for g in range(SSD_GROUPS):
            rows = slice(g * gw, (g + 1) * gw)
            msk = lane == bglob
            dtx = jnp.sum(jnp.where(msk, dtxT_ref[rows, :], 0.0), axis=1, keepdims=True)
            hdec = jnp.concatenate(
                [da_ref[bglob, g * rep + r] * h_ref[bi, g * gw + r * SSD_HEAD_DIM:g * gw + (r + 1) * SSD_HEAD_DIM, :]
                 for r in range(rep)], axis=0)
            hnew = hdec + dtx * bmat[g:g + 1, :]
            hn_ref[bi, rows, :] = hnew
            ycol = jnp.sum(hnew * cmat[g:g + 1, :], axis=1, keepdims=True)
            yT_ref[rows, :] = jnp.where(msk, ycol, yT_ref[rows, :])
        return carry

    lax.fori_loop(0, bb, body, 0)


def _ssd_decode(zx, x2d, ssm0, conv0, inner, heads, prm, w_out_bf16, final_w=None):
    norm_w, in_proj, conv_w, conv_b, dt_bias, a_log, d_skip, gnorm_w, out_proj = prm
    nb = zx.shape[0]
    gn = SSD_GROUPS * SSD_STATE
    cd = inner + 2 * gn
    pad = LANES - heads
    dtb = jnp.pad(dt_bias, (0, pad)).reshape(1, LANES)
    alog = jnp.pad(a_log, (0, pad)).reshape(1, LANES)
    drep = jnp.repeat(d_skip, SSD_HEAD_DIM).reshape(1, inner)
    full = lambda shape: pl.BlockSpec(shape, lambda i: tuple(0 for _ in shape))
    kern = functools.partial(_ssd_dec_pre_kernel, inner=inner, heads=heads)
    cnew, dtx, da, bact, cact, dx = pl.pallas_call(
        kern,
        name="ssd_dec_pre",
        grid=(1,),
        in_specs=[full((nb, cd)), full((nb, LANES)), full((nb, (CONV_K - 1) * cd)), full((CONV_K, cd)), full((1, cd)),
                  full((1, LANES)), full((1, LANES)), full((1, inner))],
        out_specs=[full((nb, (CONV_K - 1) * cd)), full((nb, inner)), full((nb, LANES)),
                   full((nb, gn)), full((nb, gn)), full((nb, inner))],
        out_shape=[jax.ShapeDtypeStruct((nb, (CONV_K - 1) * cd), F32),
                   jax.ShapeDtypeStruct((nb, inner), F32), jax.ShapeDtypeStruct((nb, LANES), F32),
                   jax.ShapeDtypeStruct((nb, gn), F32), jax.ShapeDtypeStruct((nb, gn), F32),
                   jax.ShapeDtypeStruct((nb, inner), F32)],
        compiler_params=_cparams("arbitrary"),
    )(zx[:, inner:inner + cd], zx[:, inner + cd:inner + cd + LANES],
      conv0.reshape(nb, (CONV_K - 1) * cd), conv_w, conv_b.reshape(1, cd), dtb, alog, drep)

    bb = 4 if nb % 4 == 0 else 1
    rep = heads // SSD_GROUPS
    kern = functools.partial(_ssd_dec_state_kernel, bb=bb, rep=rep)
    hnew, y_t = pl.pallas_call(
        kern,
        name="ssd_dec_state",
        grid=(nb // bb,),
        in_specs=[pl.BlockSpec(memory_space=pltpu.SMEM),
                  pl.BlockSpec((bb, inner, SSD_STATE), lambda i: (i, 0, 0)),
                  pl.BlockSpec((inner, nb), lambda i: (0, 0)),
                  pl.BlockSpec((bb, SSD_GROUPS, SSD_STATE), lambda i: (i, 0, 0)),
                  pl.BlockSpec((bb, SSD_GROUPS, SSD_STATE), lambda i: (i, 0, 0))],
        out_specs=[pl.BlockSpec((bb, inner, SSD_STATE), lambda i: (i, 0, 0)),
                   pl.BlockSpec((inner, nb), lambda i: (0, 0))],
        out_shape=[jax.ShapeDtypeStruct((nb, inner, SSD_STATE), F32),
                   jax.ShapeDtypeStruct((inner, nb), F32)],
        compiler_params=_cparams("arbitrary"),
    )(da[:, 0:heads], ssm0.reshape(nb, inner, SSD_STATE), dtx.T,
      bact.reshape(nb, SSD_GROUPS, SSD_STATE), cact.reshape(nb, SSD_GROUPS, SSD_STATE))

    out = _ssd_post(y_t.T, zx, x2d, gnorm_w, w_out_bf16, add=dx, final_w=final_w)
    return out, hnew.reshape(nb, heads, SSD_HEAD_DIM, SSD_STATE), cnew.reshape(nb, CONV_K - 1, cd)


def _layernorm(v, w, b):
    mu = jnp.mean(v, axis=-1, keepdims=True)
    vc = v - mu
    return vc * lax.rsqrt(jnp.mean(vc * vc, axis=-1, keepdims=True) + EPS) * w + b


def _gmlp_kernel(x_ref, nw_ref, win_ref, lw_ref, lb_ref, ws_ref, sbT_ref, w_ref, o_ref, *, nck):
    inner = w_ref.shape[0]
    gd = inner // GM_GROUPS
    tri = _tril(CHUNK)
    x = x_ref[...]
    xn = _rms(x, nw_ref[...]).astype(BF16)
    v = jnp.dot(xn, win_ref[:, inner:2 * inner], preferred_element_type=F32)
    vn = _layernorm(v, lw_ref[...], lb_ref[...]).astype(BF16)
    wmask = [jnp.where(tri, ws_ref[g], 0.0).astype(BF16) for g in range(GM_GROUPS)]
    parts = []
    for g in range(GM_GROUPS):
        cols = slice(g * gd, (g + 1) * gd)
        u = jnp.dot(xn, win_ref[:, cols], preferred_element_type=F32)
        z = jnp.dot(xn, win_ref[:, 2 * inner + g * gd:2 * inner + (g + 1) * gd],
                    preferred_element_type=F32)
        mixed = jnp.concatenate(
            [jnp.dot(wmask[g], vn[ck * CHUNK:(ck + 1) * CHUNK, cols], preferred_element_type=F32)
             for ck in range(nck)], axis=0) + jnp.concatenate([sbT_ref[:, g:g + 1]] * nck, axis=0)
        parts.append((u * mixed * _silu(z)).astype(BF16))
    o_ref[...] = x + jnp.dot(jnp.concatenate(parts, axis=1), w_ref[...], preferred_element_type=F32)


def _gmlp_prompt(x2d, norm_w, w_in_bf16, inner, v_ln_w, v_ln_b, spatial_w, spatial_b, w_out_bf16):
    m, d = x2d.shape
    nck = 2 if (m // CHUNK) % 2 == 0 else 1
    tm = nck * CHUNK
    sb_t = jnp.pad(spatial_b.T, ((0, 0), (0, LANES - GM_GROUPS)))
    kern = functools.partial(_gmlp_kernel, nck=nck)
    c2 = lambda shape: pl.BlockSpec(shape, lambda i: (0, 0))
    return pl.pallas_call(
        kern,
        name="gmlp_prompt",
        grid=(m // tm,),
        in_specs=[pl.BlockSpec((tm, d), lambda i: (i, 0)),
                  c2((1, d)), c2((d, 3 * inner)), c2((1, inner)), c2((1, inner)),
                  pl.BlockSpec((GM_GROUPS, CHUNK, CHUNK), lambda i: (0, 0, 0)),
                  c2((CHUNK, LANES)), c2((inner, d))],
        out_specs=pl.BlockSpec((tm, d), lambda i: (i, 0)),
        out_shape=jax.ShapeDtypeStruct((m, d), F32),
        compiler_params=_cparams("parallel"),
    )(x2d, norm_w.reshape(1, d), w_in_bf16, v_ln_w.reshape(1, inner), v_ln_b.reshape(1, inner),
      spatial_w, sb_t, w_out_bf16)


def _gmlp_dec_kernel(u_ref, v_ref, z_ref, x_ref, lw_ref, lb_ref, w00_ref, sb0_ref, w_ref,
                     o_ref, vn_ref):
    vn = _layernorm(v_ref[...], lw_ref[...], lb_ref[...])
    vn_ref[...] = vn
    mixed = w00_ref[...] * vn + sb0_ref[...]
    g = u_ref[...] * mixed * _silu(z_ref[...])
    o_ref[...] = x_ref[...] + _bdot(g, w_ref[...])


def _gmlp_decode(uvz, x2d, inner, v_ln_w, v_ln_b, spatial_w, spatial_b, w_out_bf16):
    nb, d = x2d.shape
    gd = inner // GM_GROUPS
    w00 = jnp.repeat(spatial_w[:, 0, 0], gd).reshape(1, inner)
    sb0 = jnp.repeat(spatial_b[:, 0], gd).reshape(1, inner)
    vec = pl.BlockSpec((1, inner), lambda i: (0, 0))
    return pl.pallas_call(
        _gmlp_dec_kernel,
        name="gmlp_dec",
        grid=(1,),
        in_specs=[pl.BlockSpec((nb, inner), lambda i: (0, 0)),
                  pl.BlockSpec((nb, inner), lambda i: (0, 1)),
                  pl.BlockSpec((nb, inner), lambda i: (0, 2)),
                  pl.BlockSpec((nb, d), lambda i: (0, 0)),
                  vec, vec, vec, vec,
                  pl.BlockSpec((inner, d), lambda i: (0, 0))],
        out_specs=[pl.BlockSpec((nb, d), lambda i: (0, 0)),
                   pl.BlockSpec((nb, inner), lambda i: (0, 0))],
        out_shape=[jax.ShapeDtypeStruct((nb, d), F32), jax.ShapeDtypeStruct((nb, inner), F32)],
        compiler_params=_cparams("arbitrary"),
    )(uvz, uvz, uvz, x2d, v_ln_w.reshape(1, inner), v_ln_b.reshape(1, inner), w00, sb0, w_out_bf16)


def _blockdiag_coefs(w):
    n = w.shape[0]
    rows = []
    for d in range(-(ML_BLOCK - 1), ML_BLOCK):
        cols = []
        for i in range(ML_BLOCK):
            j = i + d
            cols.append(w[:, j, i] if 0 <= j < ML_BLOCK else jnp.zeros((n,), w.dtype))
        rows.append(jnp.stack(cols, axis=1).reshape(n * ML_BLOCK))
    return jnp.stack(rows, axis=0)


def _blockdiag_apply(x, coef_refs):
    width = x.shape[1]
    outs = [None] * len(coef_refs)
    for di, d in enumerate(range(-(ML_BLOCK - 1), ML_BLOCK)):
        xs = x if d == 0 else pltpu.roll(x, (-d) % width, 1)
        for n, cref in enumerate(coef_refs):
            t = xs * cref[di:di + 1, :]
            outs[n] = t if outs[n] is None else outs[n] + t
    return outs


def _mlstm_scan_kernel(xm_ref, z_ref, x_ref, cw_ref, cb_ref, wqk_ref, wvo_ref, bo_ref,
                       wifq_ref, wifk_ref, wifv_ref, bif_ref, mhw_ref, skip_ref, wout_ref,
                       out_ref, cst_ref, nst_ref, mst_ref,
                       buf_ref, act_ref, q_ref, k_ref, v_ref, o_ref, *, heads):
    c = pl.program_id(1)
    L = CHUNK
    inner = xm_ref.shape[1]
    hd = inner // heads
    scale = hd ** -0.5

    @pl.when(c == 0)
    def _():
        cst_ref[...] = jnp.zeros(cst_ref.shape, F32)
        nst_ref[...] = jnp.zeros(nst_ref.shape, F32)
        mst_ref[...] = jnp.zeros(mst_ref.shape, F32)

    _conv_tail(buf_ref, c == 0)
    _conv_fill(buf_ref, xm_ref, 0)
    _conv_silu(buf_ref, act_ref, cw_ref, cb_ref)

    tw = wqk_ref.shape[1]
    gates = bif_ref[...]
    for t in range(inner // tw):
        cols = slice(t * tw, (t + 1) * tw)
        qk = _bdot(_act_cols(act_ref, t * tw, tw), wqk_ref[t])
        vo = _bdot(xm_ref[:, cols], wvo_ref[t])
        q_ref[:, cols] = qk[:, 0:tw]
        k_ref[:, cols] = qk[:, tw:2 * tw]
        v_ref[:, cols] = vo[:, 0:tw]
        o_ref[:, cols] = _sigmoid(vo[:, tw:2 * tw] + bo_ref[:, cols])
        gates = (gates + _bdot(qk[:, 0:tw], wifq_ref[cols, :]) + _bdot(qk[:, tw:2 * tw], wifk_ref[cols, :])
                 + _bdot(vo[:, 0:tw], wifv_ref[cols, :]))
    lf = -_softplus(-gates)
    tri = _tril(L)
    bt = _hdot(tri.astype(F32), lf)
    g_t = gates.T
    b_t = bt.T
    mvec = mst_ref[0]
    hcols = [slice(h * hd, (h + 1) * hd) for h in range(heads)]
    qb = [q_ref[:, cols].astype(BF16) for cols in hcols]
    kf = [k_ref[:, cols] * scale for cols in hcols]
    kb = [k.astype(BF16) for k in kf]
    cmat = [cst_ref[0, h] for h in range(heads)]
    s_qk = [lax.dot_general(qb[h], kb[h], NT_DIMS, preferred_element_type=F32) for h in range(heads)]
    cq = [lax.dot_general(qb[h], cmat[h].astype(BF16), NT_DIMS, preferred_element_type=F32)
          for h in range(heads)]
    bcol, igcol, inter, mt, w, wi = [], [], [], [], [], []
    for h in range(heads):
        bcol.append(bt[:, heads + h:heads + h + 1])
        igcol.append(gates[:, h:h + 1])
        brow = b_t[heads + h:heads + h + 1, :]
        igrow = g_t[h:h + 1, :]
        d = jnp.where(tri, bcol[h] - brow + igrow, -jnp.inf)
        inter.append(bcol[h] + mvec[:, h:h + 1])
        mt.append(jnp.maximum(inter[h], jnp.max(d, axis=1, keepdims=True)))
        w.append(jnp.exp(d - mt[h]) * s_qk[h])
        wi.append(jnp.exp(inter[h] - mt[h]))
    num = [wi[h] * cq[h] + _bdot(w[h], v_ref[:, hcols[h]]) for h in range(heads)]
    for h in range(heads):
        cols = hcols[h]
        nrow = nst_ref[0, h:h + 1, :]
        nq = jnp.sum(q_ref[:, cols] * nrow, axis=1, keepdims=True)
        den = wi[h] * nq + jnp.sum(w[h], axis=1, keepdims=True)
        den = jnp.maximum(jnp.abs(den), jnp.exp(-mt[h]))
        hh = o_ref[:, cols] * (num[h] / den)
        mu = jnp.mean(hh, axis=1, keepdims=True)
        hc = hh - mu
        hn = hc * lax.rsqrt(jnp.mean(hc * hc, axis=1, keepdims=True) + EPS)
        hn = hn * mhw_ref[:, cols] + skip_ref[:, cols] * _act_cols(act_ref, h * hd, hd)
        part = _bdot(hn * _silu(z_ref[:, cols]), wout_ref[cols, :])
        if h == 0:
            out_ref[...] = x_ref[...] + part
        else:
            out_ref[...] += part
    for h in range(heads):
        cols = hcols[h]
        m_new = mt[h][L - 1:L, :]
        we = jnp.exp(bcol[h][L - 1:L, :] - bcol[h] + igcol[h] - m_new)
        dp = jnp.exp(inter[h][L - 1:L, :] - m_new)
        cst_ref[0, h] = dp * cmat[h] + jnp.dot((we * v_ref[:, cols]).T.astype(BF16), kb[h],
                                               preferred_element_type=F32)
        nst_ref[0, h:h + 1, :] = dp * nst_ref[0, h:h + 1, :] + jnp.sum(we * kf[h], axis=0, keepdims=True)
        mst_ref[0, :, h:h + 1] = m_new


def _blockdiag_tiles(w, tile):
    nt = w.shape[0] * ML_BLOCK // tile
    rows = jnp.tile(w.reshape(nt, tile, ML_BLOCK), (1, 1, tile // ML_BLOCK))
    blk = jnp.arange(tile) // ML_BLOCK
    return jnp.where(blk[:, None] == blk[None, :], rows, 0.0)


def _mlstm_consts(prm, inner):
    (norm_w, in_proj, conv_w, conv_b, w_q, w_k, w_v, w_o, b_o, w_if, b_if, mh_norm_w, skip,
     out_proj) = prm
    heads = ML_HEADS
    padn = LANES - 2 * heads
    wif = jnp.pad(w_if, ((0, 0), (0, padn))).astype(BF16)
    bif = jnp.pad(b_if, (0, padn)).reshape(1, LANES)
    return dict(
        conv_w=conv_w, conv_b=conv_b.reshape(1, inner),
        cq=_blockdiag_coefs(w_q), ck=_blockdiag_coefs(w_k), cv=_blockdiag_coefs(w_v),
        co=_blockdiag_coefs(w_o), bo=b_o.reshape(1, inner),
        wqk=jnp.concatenate([_blockdiag_tiles(w_q, MXU_TILE), _blockdiag_tiles(w_k, MXU_TILE)],
                            axis=2).astype(BF16),
        wvo=jnp.concatenate([_blockdiag_tiles(w_v, MXU_TILE), _blockdiag_tiles(w_o, MXU_TILE)],
                            axis=2).astype(BF16),
        wifq=wif[0:inner], wifk=wif[inner:2 * inner], wifv=wif[2 * inner:3 * inner], bif=bif,
        mhw=mh_norm_w.reshape(1, inner), skip=skip.reshape(1, inner))


def _mlstm_scan(xmz, x2d, bsz, seq, inner, cst, w_out_bf16):
    nc = seq // CHUNK
    d = x2d.shape[1]
    heads = ML_HEADS
    hd = inner // heads
    ntile = inner // MXU_TILE
    row = lambda b, c: b * nc + c
    c2 = lambda shape: pl.BlockSpec(shape, lambda b, c: (0, 0))
    c3 = lambda shape: pl.BlockSpec(shape, lambda b, c: (0, 0, 0))
    kern = functools.partial(_mlstm_scan_kernel, heads=heads)
    return pl.pallas_call(
        kern,
        name="mlstm_scan",
        grid=(bsz, nc),
        in_specs=[pl.BlockSpec((CHUNK, inner), lambda b, c: (row(b, c), 0)),
                  pl.BlockSpec((CHUNK, inner), lambda b, c: (row(b, c), 1)),
                  pl.BlockSpec((CHUNK, d), lambda b, c: (row(b, c), 0)),
                  c2((CONV_K, inner)), c2((1, inner)),
                  c3((ntile, MXU_TILE, 2 * MXU_TILE)), c3((ntile, MXU_TILE, 2 * MXU_TILE)),
                  c2((1, inner)),
                  c2((inner, LANES)), c2((inner, LANES)), c2((inner, LANES)), c2((1, LANES)),
                  c2((1, inner)), c2((1, inner)), c2((inner, d))],
        out_specs=[pl.BlockSpec((CHUNK, d), lambda b, c: (row(b, c), 0)),
                   pl.BlockSpec((1, heads, hd, hd), lambda b, c: (b, 0, 0, 0)),
                   pl.BlockSpec((1, heads, hd), lambda b, c: (b, 0, 0)),
                   pl.BlockSpec((1, 1, heads), lambda b, c: (b, 0, 0))],
        out_shape=[jax.ShapeDtypeStruct((bsz * seq, d), F32),
                   jax.ShapeDtypeStruct((bsz, heads, hd, hd), F32),
                   jax.ShapeDtypeStruct((bsz, heads, hd), F32),
                   jax.ShapeDtypeStruct((bsz, 1, heads), F32)],
        scratch_shapes=[pltpu.VMEM((inner // LANES, CONV_ROWS, LANES), F32)] * 2
                       + [pltpu.VMEM((CHUNK, inner), F32)] * 4,
        compiler_params=_cparams("parallel", "arbitrary"),
    )(xmz, xmz, x2d, cst["conv_w"], cst["conv_b"], cst["wqk"], cst["wvo"], cst["bo"],
      cst["wifq"], cst["wifk"], cst["wifv"], cst["bif"], cst["mhw"], cst["skip"], w_out_bf16)


def _mlstm_dec_pre_kernel(xm_ref, c0_ref, n0_ref, m0_ref, cw_ref, cb_ref, cq_ref, ck_ref, cv_ref,
                          co_ref, bo_ref, wifq_ref, wifk_ref, wifv_ref, bif_ref,
                          cnew_ref, q_ref, k_ref, v_ref, og_ref, xc_ref, wev_ref, dprep_ref,
                          wrep_ref, denrep_ref, nnew_ref, mnew_ref, *, heads):
    inner = xm_ref.shape[1]
    nb = xm_ref.shape[0]
    hd = inner // heads
    xm = xm_ref[...]
    acc = cb_ref[...] + cw_ref[CONV_K - 1:CONV_K, :] * xm
    for kk in range(CONV_K - 1):
        acc = acc + cw_ref[kk:kk + 1, :] * c0_ref[:, kk * inner:(kk + 1) * inner]
    for kk in range(CONV_K - 2):
        cnew_ref[:, kk * inner:(kk + 1) * inner] = c0_ref[:, (kk + 1) * inner:(kk + 2) * inner]
    cnew_ref[:, (CONV_K - 2) * inner:(CONV_K - 1) * inner] = xm
    xc = _silu(acc)
    q, k = _blockdiag_apply(xc, [cq_ref, ck_ref])
    v, o_pre = _blockdiag_apply(xm, [cv_ref, co_ref])
    gates = _bdot(q, wifq_ref[...]) + _bdot(k, wifk_ref[...]) + _bdot(v, wifv_ref[...]) + bif_ref[...]
    ksc = k * (hd ** -0.5)
    lane = lax.broadcasted_iota(jnp.int32, (nb, LANES), 1)
    ig = gates
    lf = pltpu.roll(-_softplus(-gates), LANES - heads, 1)
    n0 = n0_ref[...]
    qk = jnp.zeros((nb, LANES), F32)
    nq = jnp.zeros((nb, LANES), F32)
    for h in range(heads):
        cols = slice(h * hd, (h + 1) * hd)
        qk = jnp.where(lane == h, jnp.sum(q[:, cols] * ksc[:, cols], axis=1, keepdims=True), qk)
        nq = jnp.where(lane == h, jnp.sum(q[:, cols] * n0[:, cols], axis=1, keepdims=True), nq)
    inter = lf + m0_ref[...]
    mt = jnp.maximum(inter, ig)
    wi = jnp.exp(inter - mt)
    we = jnp.exp(ig - mt)
    w = we * qk
    den = jnp.maximum(jnp.abs(wi * nq + w), jnp.exp(-mt))
    mnew_ref[...] = mt
    sel = _expand_sel(LANES, hd, inner)
    wi_rep = _hdot(wi, sel)
    we_rep = _hdot(we, sel)
    dprep_ref[...] = wi_rep
    wrep_ref[...] = _hdot(w, sel)
    denrep_ref[...] = _hdot(den, sel)
    wev_ref[...] = we_rep * v
    nnew_ref[...] = wi_rep * n0 + we_rep * ksc
    q_ref[...] = q
    k_ref[...] = ksc
    v_ref[...] = v
    og_ref[...] = _sigmoid(o_pre + bo_ref[...])
    xc_ref[...] = xc


def _mlstm_dec_state_kernel(c_ref, q_ref, k_ref, dp_ref, wevT_ref, cn_ref, cqT_ref, *, heads):
    b = pl.program_id(0)
    hd = c_ref.shape[2]
    nb = cqT_ref.shape[1]
    lane = lax.broadcasted_iota(jnp.int32, (hd, nb), 1)
    msk = lane == b
    for h in range(heads):
        rows = slice(h * hd, (h + 1) * hd)
        cmat = c_ref[0, h]
        cqcol = jnp.sum(cmat * q_ref[0, h:h + 1, :], axis=1, keepdims=True)
        wev = jnp.sum(jnp.where(msk, wevT_ref[rows, :], 0.0), axis=1, keepdims=True)
        cn_ref[0, h] = dp_ref[0, h:h + 1, :] * cmat + wev * k_ref[0, h:h + 1, :]
        cqT_ref[rows, :] = jnp.where(msk, cqcol, cqT_ref[rows, :])


def _mlstm_dec_post_kernel(cq_ref, v_ref, dp_ref, w_ref, den_ref, og_ref, xc_ref, z_ref,
                           mhw_ref, skip_ref, g_ref, *, heads):
    inner = cq_ref.shape[1]
    hd = inner // heads
    for h in range(heads):
        cols = slice(h * hd, (h + 1) * hd)
        num = dp_ref[:, cols] * cq_ref[:, cols] + w_ref[:, cols] * v_ref[:, cols]
        hh = og_ref[:, cols] * (num / den_ref[:, cols])
        mu = jnp.mean(hh, axis=1, keepdims=True)
        hc = hh - mu
        hn = hc * lax.rsqrt(jnp.mean(hc * hc, axis=1, keepdims=True) + EPS)
        hn = hn * mhw_ref[:, cols] + skip_ref[:, cols] * xc_ref[:, cols]
        g_ref[:, cols] = hn * _silu(z_ref[:, cols])


def _mlstm_decode(xmz, c0, n0, m0, conv0, inner, cst):
    nb = xmz.shape[0]
    heads = ML_HEADS
    hd = inner // heads
    nco = 2 * ML_BLOCK - 1
    full = lambda shape: pl.BlockSpec(shape, lambda i: tuple(0 for _ in shape))
    m0p = jnp.pad(m0, ((0, 0), (0, LANES - heads)))
    kern = functools.partial(_mlstm_dec_pre_kernel, heads=heads)
    big = jax.ShapeDtypeStruct((nb, inner), F32)
    outs = pl.pallas_call(
        kern,
        name="mlstm_dec_pre",
        grid=(1,),
        in_specs=[pl.BlockSpec((nb, inner), lambda i: (0, 0)),
                  full((nb, (CONV_K - 1) * inner)), full((nb, inner)), full((nb, LANES)),
                  full((CONV_K, inner)), full((1, inner)),
                  full((nco, inner)), full((nco, inner)), full((nco, inner)), full((nco, inner)),
                  full((1, inner)),
                  full((inner, LANES)), full((inner, LANES)), full((inner, LANES)), full((1, LANES))],
        out_specs=[full((nb, (CONV_K - 1) * inner))] + [full((nb, inner))] * 10 + [full((nb, LANES))],
        out_shape=[jax.ShapeDtypeStruct((nb, (CONV_K - 1) * inner), F32)] + [big] * 10
                  + [jax.ShapeDtypeStruct((nb, LANES), F32)],
        compiler_params=_cparams("arbitrary"),
    )(xmz, conv0.reshape(nb, (CONV_K - 1) * inner), n0.reshape(nb, inner), m0p,
      cst["conv_w"], cst["conv_b"], cst["cq"], cst["ck"], cst["cv"], cst["co"], cst["bo"],
      cst["wifq"], cst["wifk"], cst["wifv"], cst["bif"])
    cnew, q, ksc, v, og, xc, wev, dprep, wrep, denrep, nnew, mnew = outs

    kern = functools.partial(_mlstm_dec_state_kernel, heads=heads)
    h3 = lambda: pl.BlockSpec((1, heads, hd), lambda i: (i, 0, 0))
    c_new, cq_t = pl.pallas_call(
        kern,
        name="mlstm_dec_state",
        grid=(nb,),
        in_specs=[pl.BlockSpec((1, heads, hd, hd), lambda i: (i, 0, 0, 0)),
                  h3(), h3(), h3(),
                  pl.BlockSpec((inner, nb), lambda i: (0, 0))],
        out_specs=[pl.BlockSpec((1, heads, hd, hd), lambda i: (i, 0, 0, 0)),
                   pl.BlockSpec((inner, nb), lambda i: (0, 0))],
        out_shape=[jax.ShapeDtypeStruct((nb, heads, hd, hd), F32),
                   jax.ShapeDtypeStruct((inner, nb), F32)],
        compiler_params=_cparams("arbitrary"),
    )(c0, q.reshape(nb, heads, hd), ksc.reshape(nb, heads, hd), dprep.reshape(nb, heads, hd), wev.T)

    kern = functools.partial(_mlstm_dec_post_kernel, heads=heads)
    g = pl.pallas_call(
        kern,
        name="mlstm_dec_post",
        grid=(1,),
        in_specs=[full((nb, inner))] * 7
                 + [pl.BlockSpec((nb, inner), lambda i: (0, 1)), full((1, inner)), full((1, inner))],
        out_specs=full((nb, inner)),
        out_shape=big,
        compiler_params=_cparams("arbitrary"),
    )(cq_t.T, v, dprep, wrep, denrep, og, xc, xmz, cst["mhw"], cst["skip"])
    return (g, c_new, nnew.reshape(nb, heads, hd), mnew[:, 0:heads],
            cnew.reshape(nb, CONV_K - 1, inner))


def _ssd_in_weights(in_proj, inner, conv_dim):
    ncol = inner + conv_dim
    pad = LANES - (in_proj.shape[1] - ncol)
    return jnp.pad(in_proj.astype(BF16), ((0, 0), (0, pad)))


def _ssd_layer(xp, xs, ssm0, conv0, prm, final_w=None):
    norm_w, in_proj, conv_w, conv_b, dt_bias, a_log, d_skip, gnorm_w, out_proj = prm
    bsz, seq, d = xp.shape
    nb = xs.shape[0]
    inner = out_proj.shape[0]
    heads = a_log.shape[0]
    conv_dim = conv_w.shape[1]
    w_in = _ssd_in_weights(in_proj, inner, conv_dim)
    w_out = out_proj.astype(BF16)
    xp2 = xp.reshape(bsz * seq, d)
    xs2 = xs.reshape(nb, d)

    zx_p = _norm_matmul(xp2, norm_w, w_in)
    out_p, h_p = _ssd_scan(zx_p, xp2, bsz, seq, inner, heads, conv_w, conv_b, dt_bias, a_log, d_skip,
                           gnorm_w, w_out, final_w)
    conv_p = zx_p.reshape(bsz, seq, -1)[:, seq - (CONV_K - 1):, inner:inner + conv_dim]

    zx_s = _norm_matmul(xs2, norm_w, w_in)
    out_s, h_s, conv_s = _ssd_decode(zx_s, xs2, ssm0, conv0, inner, heads, prm, w_out, final_w=final_w)
    return (out_p.reshape(bsz, seq, d), out_s.reshape(nb, 1, d),
            h_p.reshape(bsz, heads, SSD_HEAD_DIM, SSD_STATE), conv_p, h_s, conv_s)


def _gmlp_layer(xp, xs, prm):
    norm_w, in_proj, v_ln_w, v_ln_b, spatial_w, spatial_b, out_proj = prm
    bsz, seq, d = xp.shape
    nb = xs.shape[0]
    inner = out_proj.shape[0]
    w_in = in_proj.astype(BF16)
    w_out = out_proj.astype(BF16)
    xp2 = xp.reshape(bsz * seq, d)
    xs2 = xs.reshape(nb, d)
    out_p = _gmlp_prompt(xp2, norm_w, w_in, inner, v_ln_w, v_ln_b, spatial_w, spatial_b, w_out)
    uvz_s = _norm_matmul(xs2, norm_w, w_in)
    out_s, vn_s = _gmlp_decode(uvz_s, xs2, inner, v_ln_w, v_ln_b, spatial_w, spatial_b, w_out)
    return out_p.reshape(bsz, seq, d), out_s.reshape(nb, 1, d), vn_s.reshape(nb, 1, inner)


def _mlstm_layer(xp, xs, c0, n0, m0, conv0, prm):
    norm_w, in_proj = prm[0], prm[1]
    out_proj = prm[-1]
    bsz, seq, d = xp.shape
    nb = xs.shape[0]
    inner = out_proj.shape[0]
    w_in = in_proj.astype(BF16)
    w_out = out_proj.astype(BF16)
    cst = _mlstm_consts(prm, inner)
    xp2 = xp.reshape(bsz * seq, d)
    xs2 = xs.reshape(nb, d)

    xmz_p = _norm_matmul(xp2, norm_w, w_in)
    out_p, c_p, n_p, m_p = _mlstm_scan(xmz_p, xp2, bsz, seq, inner, cst, w_out)
    conv_p = xmz_p.reshape(bsz, seq, -1)[:, seq - (CONV_K - 1):, 0:inner]

    xmz_s = _norm_matmul(xs2, norm_w, w_in)
    g_s, c_s, n_s, m_s, conv_s = _mlstm_decode(xmz_s, c0, n0, m0, conv0, inner, cst)
    out_s = _matmul_res(g_s, w_out, xs2)
    return (out_p.reshape(bsz, seq, d), out_s.reshape(nb, 1, d),
            c_p, n_p, m_p.reshape(bsz, ML_HEADS), conv_p, c_s, n_s, m_s, conv_s)


def kernel(x_prompt, x_sample, state_l0_ssm, state_l0_conv, state_l2_C, state_l2_n, state_l2_m, state_l2_conv, state_l3_ssm, state_l3_conv, l0_norm_w, l0_in_proj, l0_conv_w, l0_conv_b, l0_dt_bias, l0_A_log, l0_D_skip, l0_gnorm_w, l0_out_proj, l1_norm_w, l1_in_proj, l1_v_ln_w, l1_v_ln_b, l1_spatial_w, l1_spatial_b, l1_out_proj, l2_norm_w, l2_in_proj, l2_conv_w, l2_conv_b, l2_w_q, l2_w_k, l2_w_v, l2_w_o, l2_b_o, l2_w_if, l2_b_if, l2_mh_norm_w, l2_skip, l2_out_proj, l3_norm_w, l3_in_proj, l3_conv_w, l3_conv_b, l3_dt_bias, l3_A_log, l3_D_skip, l3_gnorm_w, l3_out_proj, final_norm_w):
    p0 = (l0_norm_w, l0_in_proj, l0_conv_w, l0_conv_b, l0_dt_bias, l0_A_log, l0_D_skip, l0_gnorm_w, l0_out_proj)
    p1 = (l1_norm_w, l1_in_proj, l1_v_ln_w, l1_v_ln_b, l1_spatial_w, l1_spatial_b, l1_out_proj)
    p2 = (l2_norm_w, l2_in_proj, l2_conv_w, l2_conv_b, l2_w_q, l2_w_k, l2_w_v, l2_w_o, l2_b_o,
          l2_w_if, l2_b_if, l2_mh_norm_w, l2_skip, l2_out_proj)
    p3 = (l3_norm_w, l3_in_proj, l3_conv_w, l3_conv_b, l3_dt_bias, l3_A_log, l3_D_skip, l3_gnorm_w, l3_out_proj)

    hp, hs, p0_ssm, p0_conv, s0_ssm, s0_conv = _ssd_layer(x_prompt, x_sample, state_l0_ssm, state_l0_conv, p0)
    hp, hs, s1_v = _gmlp_layer(hp, hs, p1)
    hp, hs, p2_C, p2_n, p2_m, p2_conv, s2_C, s2_n, s2_m, s2_conv = _mlstm_layer(
        hp, hs, state_l2_C, state_l2_n, state_l2_m, state_l2_conv, p2)
    y_prompt, y_sample, p3_ssm, p3_conv, s3_ssm, s3_conv = _ssd_layer(
        hp, hs, state_l3_ssm, state_l3_conv, p3, final_w=final_norm_w)
    return (y_prompt, y_sample,
            p0_ssm, p0_conv, s0_ssm, s0_conv,
            s1_v,
            p2_C, p2_n, p2_m, p2_conv, s2_C, s2_n, s2_m, s2_conv,
            p3_ssm, p3_conv, s3_ssm, s3_conv)
```

```python
import functools
import math

import jax
import jax.numpy as jnp
from jax import lax
from jax.experimental import pallas as pl
from jax.experimental.pallas import tpu as pltpu

F32 = jnp.float32
BF16 = jnp.bfloat16
EPS = 1e-6
CONV_K = 4
CHUNK = 128
LANES = 128
MXU_TILE = 256
SUBLANES = 8
SSD_HEAD_DIM = 64
SSD_STATE = 128
SSD_GROUPS = 8
OUT_GROUPS = 2
ML_HEADS = 4
ML_BLOCK = 4
GM_GROUPS = 8
VMEM_LIMIT = 56 * 1024 * 1024
HI = lax.Precision.HIGHEST
LOG2E = 1.4426950408889634
NT_DIMS = (((1,), (1,)), ((), ()))


def _cparams(*sem):
    return pltpu.CompilerParams(dimension_semantics=sem, vmem_limit_bytes=VMEM_LIMIT)


def _sigmoid(x):
    return 1.0 / (1.0 + jnp.exp2(x * (-LOG2E)))


def _silu(x):
    h = 0.5 * x
    return h + h * jnp.tanh(h)


def _softplus(x):
    return jnp.maximum(x, 0.0) + jnp.log(1.0 + jnp.exp(-jnp.abs(x)))


def _bdot(a, b):
    return jnp.dot(a.astype(BF16), b.astype(BF16), preferred_element_type=F32)


def _bdot_nt(a, b):
    return lax.dot_general(a.astype(BF16), b.astype(BF16), NT_DIMS, preferred_element_type=F32)


def _hdot(a, b):
    return jnp.dot(a, b, precision=HI, preferred_element_type=F32)


def _rms(x, w):
    return x * lax.rsqrt(jnp.mean(x * x, axis=-1, keepdims=True) + EPS) * w


def _tril(n):
    r = lax.broadcasted_iota(jnp.int32, (n, n), 0)
    c = lax.broadcasted_iota(jnp.int32, (n, n), 1)
    return r >= c


def _expand_sel(n_in, width, n_out):
    r = lax.broadcasted_iota(jnp.int32, (n_in, n_out), 0)
    c = lax.broadcasted_iota(jnp.int32, (n_in, n_out), 1)
    return (c // width == r).astype(F32)


def _split_bf16(x):
    hi = x.astype(BF16)
    return hi, (x - hi.astype(F32)).astype(BF16)


def _expand(pieces, sel_bf16):
    return (jnp.dot(pieces[0], sel_bf16, preferred_element_type=F32)
            + jnp.dot(pieces[1], sel_bf16, preferred_element_type=F32))


def _norm_matmul_kernel(x_ref, nw_ref, w_ref, o_ref):
    o_ref[...] = _bdot(_rms(x_ref[...], nw_ref[...]), w_ref[...])


def _norm_matmul(x2d, norm_w, w_bf16):
    m, k = x2d.shape
    n = w_bf16.shape[1]
    tm = min(m, 512)
    return pl.pallas_call(
        _norm_matmul_kernel,
        name="norm_matmul",
        grid=(m // tm,),
        in_specs=[pl.BlockSpec((tm, k), lambda i: (i, 0)),
                  pl.BlockSpec((1, k), lambda i: (0, 0)),
                  pl.BlockSpec((k, n), lambda i: (0, 0), pipeline_mode=pl.Buffered(1))],
        out_specs=pl.BlockSpec((tm, n), lambda i: (i, 0)),
        out_shape=jax.ShapeDtypeStruct((m, n), F32),
        compiler_params=_cparams("parallel"),
    )(x2d, norm_w.reshape(1, k), w_bf16)


def _matmul_res_kernel(g_ref, w_ref, x_ref, o_ref):
    o_ref[...] = x_ref[...] + _bdot(g_ref[...], w_ref[...])


def _matmul_res(g2d, w_bf16, x2d):
    m, k = g2d.shape
    n = w_bf16.shape[1]
    tm = min(m, 512)
    return pl.pallas_call(
        _matmul_res_kernel,
        name="matmul_res",
        grid=(m // tm,),
        in_specs=[pl.BlockSpec((tm, k), lambda i: (i, 0)),
                  pl.BlockSpec((k, n), lambda i: (0, 0)),
                  pl.BlockSpec((tm, n), lambda i: (i, 0))],
        out_specs=pl.BlockSpec((tm, n), lambda i: (i, 0)),
        out_shape=jax.ShapeDtypeStruct((m, n), F32),
        compiler_params=_cparams("parallel"),
    )(g2d, w_bf16, x2d)


CONV_ROWS = CHUNK + SUBLANES
CONV_VREGS = CONV_ROWS // SUBLANES
CUR = slice(SUBLANES, CONV_ROWS)


def _conv_tail(buf_ref, is_first):
    lead = (slice(None),) * (len(buf_ref.shape) - 2)

    @pl.when(is_first)
    def _():
        buf_ref[lead + (slice(0, SUBLANES), slice(None))] = jnp.zeros(
            buf_ref.shape[:-2] + (SUBLANES, LANES), F32)

    @pl.when(jnp.logical_not(is_first))
    def _():
        buf_ref[lead + (slice(0, SUBLANES), slice(None))] = buf_ref[lead + (slice(CHUNK, CONV_ROWS), slice(None))]


def _conv_fill(buf_ref, src_ref, col0):
    t0 = col0 // LANES
    for j in range(src_ref.shape[1] // LANES):
        buf_ref[t0 + j, CUR, :] = src_ref[:, j * LANES:(j + 1) * LANES]


def _conv_silu(buf_ref, act_ref, cw_ref, cb_ref):
    nv = CONV_VREGS
    for j in range(buf_ref.shape[0]):
        cols = slice(j * LANES, (j + 1) * LANES)
        v = [buf_ref[j, pl.ds(a, SUBLANES, stride=nv), :] for a in range(nv)]
        wrap = {a: pltpu.roll(v[a], 1, 0) for a in range(nv - (CONV_K - 1), nv)}
        coef = [jnp.broadcast_to(cw_ref[k:k + 1, cols], (SUBLANES, LANES)) for k in range(CONV_K)]
        bias = jnp.broadcast_to(cb_ref[:, cols], (SUBLANES, LANES))
        for a in range(nv):
            acc = bias + coef[CONV_K - 1] * v[a]
            for back in range(1, CONV_K):
                tap = v[a - back] if a >= back else wrap[a - back + nv]
                acc = acc + coef[CONV_K - 1 - back] * tap
            act_ref[j, pl.ds(a, SUBLANES, stride=nv), :] = _silu(acc)


def _act_cols(act_ref, col0, width):
    t0 = col0 // LANES
    tiles = [act_ref[t0 + j, CUR, :] for j in range(width // LANES)]
    return tiles[0] if len(tiles) == 1 else jnp.concatenate(tiles, axis=1)


def _ssd_scan_kernel(*refs, inner, heads, has_final, streams):
    (xs_ref, bc_ref, dt_ref, z_ref, x_ref, cw_ref, cb_ref, dtb_ref, alog_ref, drep_ref, sel_ref,
     gnw_ref, wout_ref) = refs[:13]
    fin_ref = refs[13] if has_final else None
    o_ref, h_ref, buf_ref, act_ref = refs[13 + has_final:]
    c = pl.program_id(1)
    L = CHUNK
    gn = SSD_GROUPS * SSD_STATE
    rep = heads // SSD_GROUPS
    gw = rep * SSD_HEAD_DIM
    S = range(streams)

    @pl.when(c == 0)
    def _():
        h_ref[...] = jnp.zeros(h_ref.shape, F32)

    _conv_tail(buf_ref, c == 0)
    for s in S:
        _conv_fill(buf_ref.at[s], xs_ref.at[s], 0)
        _conv_fill(buf_ref.at[s], bc_ref.at[s], inner)
    for s in S:
        _conv_silu(buf_ref.at[s], act_ref.at[s], cw_ref, cb_ref)

    lane = lax.broadcasted_iota(jnp.int32, (1, LANES), 1)
    a_neg = jnp.where(lane < heads, -jnp.exp(alog_ref[...]), 0.0)
    tri = _tril(L)
    cum2, cum2_t, dt_t, dend, e_hi, e_lo, w_hi, w_lo = [], [], [], [], [], [], [], []
    for s in S:
        dt = _softplus(dt_ref[s] + dtb_ref[...])
        cm = _hdot(tri.astype(F32), dt * a_neg)
        cum2.append(cm * LOG2E)
        cum2_t.append(cum2[s].T)
        dt_t.append(dt.T)
        cum_end = cm[L - 1:L, :]
        dend.append(jnp.exp(cum_end))
        hi, lo = _split_bf16(jnp.exp(cm))
        e_hi.append(hi)
        e_lo.append(lo)
        hi, lo = _split_bf16(jnp.exp(cum_end - cm) * dt)
        w_hi.append(hi)
        w_lo.append(lo)
    e_pieces = (jnp.concatenate(e_hi, axis=0), jnp.concatenate(e_lo, axis=0))
    w_pieces = (jnp.concatenate(w_hi, axis=0), jnp.concatenate(w_lo, axis=0))

    lane_g = lax.broadcasted_iota(jnp.int32, (L, gw), 1) // SSD_HEAD_DIM
    pend = []
    for g in range(SSD_GROUPS):
        rows = slice(g * gw, (g + 1) * gw)
        e_all = _expand(e_pieces, sel_ref[:, rows])
        w_all = _expand(w_pieces, sel_ref[:, rows])
        yn = []
        for s in S:
            srows = slice(s * L, (s + 1) * L)
            aref = act_ref.at[s]
            bg = _act_cols(aref, inner + g * SSD_STATE, SSD_STATE).astype(BF16)
            cg = _act_cols(aref, inner + gn + g * SSD_STATE, SSD_STATE).astype(BF16)
            cb = lax.dot_general(cg, bg, NT_DIMS, preferred_element_type=F32)
            xg = _act_cols(aref, g * gw, gw)
            mixes = []
            for r in range(rep):
                h = g * rep + r
                seg = cum2[s][:, h:h + 1] - cum2_t[s][h:h + 1, :]
                decay = jnp.exp2(jnp.where(tri, seg, -jnp.inf))
                mixes.append((cb * decay * dt_t[s][h:h + 1, :]).astype(BF16))
            mixcat = jnp.concatenate(mixes, axis=1)
            xgb = xg.astype(BF16)
            xblk = jnp.concatenate([jnp.where(lane_g == r, xgb, jnp.zeros_like(xgb)) for r in range(rep)],
                                   axis=0)
            y = jnp.dot(mixcat, xblk, preferred_element_type=F32)
            hg = h_ref[s, 0, rows, :]
            yi = lax.dot_general(cg, hg.astype(BF16), NT_DIMS, preferred_element_type=F32)
            y = y + yi * e_all[srows, :] + drep_ref[:, rows] * xg
            upd = jnp.dot((xg * w_all[srows, :]).T.astype(BF16), bg, preferred_element_type=F32)
            for r in range(rep):
                h = g * rep + r
                hr = slice(r * SSD_HEAD_DIM, (r + 1) * SSD_HEAD_DIM)
                h_ref[s, 0, g * gw + r * SSD_HEAD_DIM:g * gw + (r + 1) * SSD_HEAD_DIM, :] = (
                    dend[s][:, h:h + 1] * hg[hr, :] + upd[hr, :])
            y = y * _silu(z_ref[s, :, rows])
            y = y * lax.rsqrt(jnp.mean(y * y, axis=-1, keepdims=True) + EPS) * gnw_ref[:, rows]
            yn.append(y.astype(BF16))
        pend.append(jnp.concatenate(yn, axis=0))
        if len(pend) == OUT_GROUPS:
            g0 = g + 1 - OUT_GROUPS
            part = jnp.dot(jnp.concatenate(pend, axis=1), wout_ref[g0 * gw:(g + 1) * gw, :],
                           preferred_element_type=F32)
            pend = []
            for s in S:
                srows = slice(s * L, (s + 1) * L)
                if g0 == 0:
                    o_ref[s] = x_ref[s] + part[srows, :]
                else:
                    o_ref[s] += part[srows, :]
    if has_final:
        for s in S:
            o_ref[s] = _rms(o_ref[s], fin_ref[...])


def _ssd_scan(zx, x2d, bsz, seq, inner, heads, conv_w, conv_b, dt_bias, a_log, d_skip, gnorm_w,
              w_out_bf16, final_w):
    nc = seq // CHUNK
    d = x2d.shape[1]
    gn = SSD_GROUPS * SSD_STATE
    conv_dim = inner + 2 * gn
    assert inner // SSD_GROUPS == (heads // SSD_GROUPS) * SSD_HEAD_DIM
    xs_blk = inner // inner
    bc_blk = (2 * inner) // (2 * gn)
    dt_blk = (inner + conv_dim) // LANES
    pad = LANES - heads
    dtb = jnp.pad(dt_bias, (0, pad)).reshape(1, LANES)
    alog = jnp.pad(a_log, (0, pad)).reshape(1, LANES)
    drep = jnp.repeat(d_skip, SSD_HEAD_DIM).reshape(1, inner)
    sel = (jnp.arange(inner)[None, :] // SSD_HEAD_DIM == jnp.arange(LANES)[:, None]).astype(BF16)
    has_final = final_w is not None
    streams = 2 if bsz % 2 == 0 else 1
    bh = bsz // streams
    kern = functools.partial(_ssd_scan_kernel, inner=inner, heads=heads, has_final=has_final,
                             streams=streams)
    row = lambda b, c: b * nc + c
    c2 = lambda shape: pl.BlockSpec(shape, lambda b, c: (0, 0))
    zx3 = zx.reshape(streams, bh * seq, zx.shape[1])
    x3 = x2d.reshape(streams, bh * seq, d)
    ins = [zx3, zx3, zx3, zx3, x3, conv_w, conv_b.reshape(1, conv_dim), dtb, alog, drep, sel,
           gnorm_w.reshape(1, inner), w_out_bf16]
    specs = [pl.BlockSpec((streams, CHUNK, inner), lambda b, c: (0, row(b, c), xs_blk)),
             pl.BlockSpec((streams, CHUNK, 2 * gn), lambda b, c: (0, row(b, c), bc_blk)),
             pl.BlockSpec((streams, CHUNK, LANES), lambda b, c: (0, row(b, c), dt_blk)),
             pl.BlockSpec((streams, CHUNK, inner), lambda b, c: (0, row(b, c), 0)),
             pl.BlockSpec((streams, CHUNK, d), lambda b, c: (0, row(b, c), 0)),
             c2((CONV_K, conv_dim)), c2((1, conv_dim)), c2((1, LANES)), c2((1, LANES)),
             c2((1, inner)), c2((LANES, inner)), c2((1, inner)), c2((inner, d))]
    if has_final:
        ins.append(final_w.reshape(1, d))
        specs.append(c2((1, d)))
    out, h = pl.pallas_call(
        kern,
        name="ssd_scan",
        grid=(bh, nc),
        in_specs=specs,
        out_specs=[pl.BlockSpec((streams, CHUNK, d), lambda b, c: (0, row(b, c), 0)),
                   pl.BlockSpec((streams, 1, inner, SSD_STATE), lambda b, c: (0, b, 0, 0))],
        out_shape=[jax.ShapeDtypeStruct((streams, bh * seq, d), F32),
                   jax.ShapeDtypeStruct((streams, bh, inner, SSD_STATE), F32)],
        scratch_shapes=[pltpu.VMEM((streams, conv_dim // LANES, CONV_ROWS, LANES), F32),
                        pltpu.VMEM((streams, conv_dim // LANES, CONV_ROWS, LANES), F32)],
        compiler_params=_cparams("parallel", "arbitrary"),
    )(*ins)
    return out.reshape(bsz * seq, d), h.reshape(bsz, inner, SSD_STATE)


def _ssd_post_kernel(*refs, has_add, has_final):
    y_ref, z_ref, x_ref, gw_ref, w_ref = refs[:5]
    pos = 5
    add_ref = fin_ref = None
    if has_add:
        add_ref = refs[pos]
        pos += 1
    if has_final:
        fin_ref = refs[pos]
        pos += 1
    o_ref = refs[pos]
    inner = y_ref.shape[1]
    gwid = inner // SSD_GROUPS
    parts = []
    for g in range(SSD_GROUPS):
        cols = slice(g * gwid, (g + 1) * gwid)
        y = y_ref[:, cols]
        if has_add:
            y = y + add_ref[:, cols]
        y = y * _silu(z_ref[:, cols])
        y = y * lax.rsqrt(jnp.mean(y * y, axis=-1, keepdims=True) + EPS)
        parts.append((y * gw_ref[:, cols]).astype(BF16))
    out = x_ref[...] + jnp.dot(jnp.concatenate(parts, axis=1), w_ref[...],
                               preferred_element_type=F32)
    if has_final:
        out = _rms(out, fin_ref[...])
    o_ref[...] = out


def _ssd_post(y, zx, x2d, gnorm_w, w_out_bf16, add=None, final_w=None):
    m, inner = y.shape
    d = x2d.shape[1]
    tm = min(m, 512)
    ins = [y, zx, x2d, gnorm_w.reshape(1, inner), w_out_bf16]
    specs = [pl.BlockSpec((tm, inner), lambda i: (i, 0)),
             pl.BlockSpec((tm, inner), lambda i: (i, 0)),
             pl.BlockSpec((tm, d), lambda i: (i, 0)),
             pl.BlockSpec((1, inner), lambda i: (0, 0)),
             pl.BlockSpec((inner, d), lambda i: (0, 0))]
    if add is not None:
        ins.append(add)
        specs.append(pl.BlockSpec((tm, inner), lambda i: (i, 0)))
    if final_w is not None:
        ins.append(final_w.reshape(1, d))
        specs.append(pl.BlockSpec((1, d), lambda i: (0, 0)))
    kern = functools.partial(_ssd_post_kernel, has_add=add is not None, has_final=final_w is not None)
    return pl.pallas_call(
        kern,
        name="ssd_post",
        grid=(m // tm,),
        in_specs=specs,
        out_specs=pl.BlockSpec((tm, d), lambda i: (i, 0)),
        out_shape=jax.ShapeDtypeStruct((m, d), F32),
        compiler_params=_cparams("parallel"),
    )(*ins)


def _ssd_dec_pre_kernel(xbc_ref, dt_ref, c0_ref, cw_ref, cb_ref, dtb_ref, alog_ref, drep_ref,
                        cnew_ref, dtx_ref, da_ref, b_ref, c_ref, dx_ref, *, inner, heads):
    gn = SSD_GROUPS * SSD_STATE
    cd = inner + 2 * gn
    xnew = xbc_ref[...]
    acc = cb_ref[...] + cw_ref[CONV_K - 1:CONV_K, :] * xnew
    for k in range(CONV_K - 1):
        acc = acc + cw_ref[k:k + 1, :] * c0_ref[:, k * cd:(k + 1) * cd]
    for k in range(CONV_K - 2):
        cnew_ref[:, k * cd:(k + 1) * cd] = c0_ref[:, (k + 1) * cd:(k + 2) * cd]
    cnew_ref[:, (CONV_K - 2) * cd:(CONV_K - 1) * cd] = xnew
    act = _silu(acc)
    xs = act[:, 0:inner]
    b_ref[...] = act[:, inner:inner + gn]
    c_ref[...] = act[:, inner + gn:inner + 2 * gn]
    lane = lax.broadcasted_iota(jnp.int32, (1, LANES), 1)
    dt = _softplus(dt_ref[...] + dtb_ref[...])
    a_neg = jnp.where(lane < heads, -jnp.exp(alog_ref[...]), 0.0)
    da_ref[...] = jnp.exp(dt * a_neg)
    dtx_ref[...] = _hdot(dt, _expand_sel(LANES, SSD_HEAD_DIM, inner)) * xs
    dx_ref[...] = drep_ref[...] * xs


def _ssd_dec_state_kernel(da_ref, h_ref, dtxT_ref, b_ref, c_ref, hn_ref, yT_ref, *, bb, rep):
    i = pl.program_id(0)
    gw = rep * SSD_HEAD_DIM
    nb = yT_ref.shape[1]
    lane = lax.broadcasted_iota(jnp.int32, (gw, nb), 1)
    brow = lax.broadcasted_iota(jnp.int32, (nb, SSD_STATE), 0)

    @pl.when(i == 0)
    def _():
        yT_ref[...] = jnp.zeros(yT_ref.shape, F32)

    def body(bi, carry):
        bglob = i * bb + bi
        msk = lane == bglob
        pick = brow == bglob
        bmat = b_ref[bi]
        cmat = c_ref[bi]
        hnews = []
        for g in range(SSD_GROUPS):
            rows = slice(g * gw, (g + 1) * gw)
            dtx = jnp.sum(jnp.where(msk, dtxT_ref[rows, :], 0.0), axis=1, keepdims=True)
            hdec = jnp.concatenate(
                [da_ref[bglob, g * rep + r] * h_ref[bi, g * gw + r * SSD_HEAD_DIM:g * gw + (r + 1) * SSD_HEAD_DIM, :]
                 for r in range(rep)], axis=0)
            hnew = hdec + dtx * bmat[g:g + 1, :]
            hn_ref[bi, rows, :] = hnew
            hnews.append(hnew.astype(BF16))
        for g in range(SSD_GROUPS):
            rows = slice(g * gw, (g + 1) * gw)
            crow = jnp.where(pick, jnp.broadcast_to(cmat[g:g + 1, :], (nb, SSD_STATE)), 0.0).astype(BF16)
            yT_ref[rows, :] += lax.dot_general(hnews[g], crow, NT_DIMS, preferred_element_type=F32)
        return carry

    lax.fori_loop(0, bb, body, 0)


def _ssd_decode(zx, x2d, ssm0, conv0, inner, heads, prm, w_out_bf16, final_w=None):
    norm_w, in_proj, conv_w, conv_b, dt_bias, a_log, d_skip, gnorm_w, out_proj = prm
    nb = zx.shape[0]
    gn = SSD_GROUPS * SSD_STATE
    cd = inner + 2 * gn
    pad = LANES - heads
    dtb = jnp.pad(dt_bias, (0, pad)).reshape(1, LANES)
    alog = jnp.pad(a_log, (0, pad)).reshape(1, LANES)
    drep = jnp.repeat(d_skip, SSD_HEAD_DIM).reshape(1, inner)
    full = lambda shape: pl.BlockSpec(shape, lambda i: tuple(0 for _ in shape))
    kern = functools.partial(_ssd_dec_pre_kernel, inner=inner, heads=heads)
    cnew, dtx, da, bact, cact, dx = pl.pallas_call(
        kern,
        name="ssd_dec_pre",
        grid=(1,),
        in_specs=[full((nb, cd)), full((nb, LANES)), full((nb, (CONV_K - 1) * cd)), full((CONV_K, cd)), full((1, cd)),
                  full((1, LANES)), full((1, LANES)), full((1, inner))],
        out_specs=[full((nb, (CONV_K - 1) * cd)), full((nb, inner)), full((nb, LANES)),
                   full((nb, gn)), full((nb, gn)), full((nb, inner))],
        out_shape=[jax.ShapeDtypeStruct((nb, (CONV_K - 1) * cd), F32),
                   jax.ShapeDtypeStruct((nb, inner), F32), jax.ShapeDtypeStruct((nb, LANES), F32),
                   jax.ShapeDtypeStruct((nb, gn), F32), jax.ShapeDtypeStruct((nb, gn), F32),
                   jax.ShapeDtypeStruct((nb, inner), F32)],
        compiler_params=_cparams("arbitrary"),
    )(zx[:, inner:inner + cd], zx[:, inner + cd:inner + cd + LANES],
      conv0.reshape(nb, (CONV_K - 1) * cd), conv_w, conv_b.reshape(1, cd), dtb, alog, drep)

    bb = 4 if nb % 4 == 0 else 1
    rep = heads // SSD_GROUPS
    kern = functools.partial(_ssd_dec_state_kernel, bb=bb, rep=rep)
    hnew, y_t = pl.pallas_call(
        kern,
        name="ssd_dec_state",
        grid=(nb // bb,),
        in_specs=[pl.BlockSpec(memory_space=pltpu.SMEM),
                  pl.BlockSpec((bb, inner, SSD_STATE), lambda i: (i, 0, 0)),
                  pl.BlockSpec((inner, nb), lambda i: (0, 0)),
                  pl.BlockSpec((bb, SSD_GROUPS, SSD_STATE), lambda i: (i, 0, 0)),
                  pl.BlockSpec((bb, SSD_GROUPS, SSD_STATE), lambda i: (i, 0, 0))],
        out_specs=[pl.BlockSpec((bb, inner, SSD_STATE), lambda i: (i, 0, 0)),
                   pl.BlockSpec((inner, nb), lambda i: (0, 0))],
        out_shape=[jax.ShapeDtypeStruct((nb, inner, SSD_STATE), F32),
                   jax.ShapeDtypeStruct((inner, nb), F32)],
        compiler_params=_cparams("arbitrary"),
    )(da[:, 0:heads], ssm0.reshape(nb, inner, SSD_STATE), dtx.T,
      bact.reshape(nb, SSD_GROUPS, SSD_STATE), cact.reshape(nb, SSD_GROUPS, SSD_STATE))

    out = _ssd_post(y_t.T, zx, x2d, gnorm_w, w_out_bf16, add=dx, final_w=final_w)
    return out, hnew.reshape(nb, heads, SSD_HEAD_DIM, SSD_STATE), cnew.reshape(nb, CONV_K - 1, cd)


def _layernorm(v, w, b):
    mu = jnp.mean(v, axis=-1, keepdims=True)
    vc = v - mu
    return vc * lax.rsqrt(jnp.mean(vc * vc, axis=-1, keepdims=True) + EPS) * w + b


def _gmlp_kernel(x_ref, nw_ref, win_ref, lw_ref, lb_ref, ws_ref, sbT_ref, w_ref, o_ref, *, nck):
    inner = w_ref.shape[0]
    gd = inner // GM_GROUPS
    tri = _tril(CHUNK)
    x = x_ref[...]
    xn = _rms(x, nw_ref[...]).astype(BF16)
    v = jnp.dot(xn, win_ref[:, inner:2 * inner], preferred_element_type=F32)
    vn = _layernorm(v, lw_ref[...], lb_ref[...]).astype(BF16)
    wmask = [jnp.where(tri, ws_ref[g], 0.0).astype(BF16) for g in range(GM_GROUPS)]
    parts = []
    for g in range(GM_GROUPS):
        cols = slice(g * gd, (g + 1) * gd)
        u = jnp.dot(xn, win_ref[:, cols], preferred_element_type=F32)
        z = jnp.dot(xn, win_ref[:, 2 * inner + g * gd:2 * inner + (g + 1) * gd],
                    preferred_element_type=F32)
        mixed = jnp.concatenate(
            [jnp.dot(wmask[g], vn[ck * CHUNK:(ck + 1) * CHUNK, cols], preferred_element_type=F32)
             for ck in range(nck)], axis=0) + jnp.concatenate([sbT_ref[:, g:g + 1]] * nck, axis=0)
        parts.append((u * mixed * _silu(z)).astype(BF16))
    o_ref[...] = x + jnp.dot(jnp.concatenate(parts, axis=1), w_ref[...], preferred_element_type=F32)


def _gmlp_prompt(x2d, norm_w, w_in_bf16, inner, v_ln_w, v_ln_b, spatial_w, spatial_b, w_out_bf16):
    m, d = x2d.shape
    nck = 2 if (m // CHUNK) % 2 == 0 else 1
    tm = nck * CHUNK
    sb_t = jnp.pad(spatial_b.T, ((0, 0), (0, LANES - GM_GROUPS)))
    kern = functools.partial(_gmlp_kernel, nck=nck)
    c2 = lambda shape: pl.BlockSpec(shape, lambda i: (0, 0))
    return pl.pallas_call(
        kern,
        name="gmlp_prompt",
        grid=(m // tm,),
        in_specs=[pl.BlockSpec((tm, d), lambda i: (i, 0)),
                  c2((1, d)), c2((d, 3 * inner)), c2((1, inner)), c2((1, inner)),
                  pl.BlockSpec((GM_GROUPS, CHUNK, CHUNK), lambda i: (0, 0, 0)),
                  c2((CHUNK, LANES)), c2((inner, d))],
        out_specs=pl.BlockSpec((tm, d), lambda i: (i, 0)),
        out_shape=jax.ShapeDtypeStruct((m, d), F32),
        compiler_params=_cparams("parallel"),
    )(x2d, norm_w.reshape(1, d), w_in_bf16, v_ln_w.reshape(1, inner), v_ln_b.reshape(1, inner),
      spatial_w, sb_t, w_out_bf16)


def _gmlp_dec_kernel(u_ref, v_ref, z_ref, x_ref, lw_ref, lb_ref, w00_ref, sb0_ref, w_ref,
                     o_ref, vn_ref):
    vn = _layernorm(v_ref[...], lw_ref[...], lb_ref[...])
    vn_ref[...] = vn
    mixed = w00_ref[...] * vn + sb0_ref[...]
    g = u_ref[...] * mixed * _silu(z_ref[...])
    o_ref[...] = x_ref[...] + _bdot(g, w_ref[...])


def _gmlp_decode(uvz, x2d, inner, v_ln_w, v_ln_b, spatial_w, spatial_b, w_out_bf16):
    nb, d = x2d.shape
    gd = inner // GM_GROUPS
    w00 = jnp.repeat(spatial_w[:, 0, 0], gd).reshape(1, inner)
    sb0 = jnp.repeat(spatial_b[:, 0], gd).reshape(1, inner)
    vec = pl.BlockSpec((1, inner), lambda i: (0, 0))
    return pl.pallas_call(
        _gmlp_dec_kernel,
        name="gmlp_dec",
        grid=(1,),
        in_specs=[pl.BlockSpec((nb, inner), lambda i: (0, 0)),
                  pl.BlockSpec((nb, inner), lambda i: (0, 1)),
                  pl.BlockSpec((nb, inner), lambda i: (0, 2)),
                  pl.BlockSpec((nb, d), lambda i: (0, 0)),
                  vec, vec, vec, vec,
                  pl.BlockSpec((inner, d), lambda i: (0, 0))],
        out_specs=[pl.BlockSpec((nb, d), lambda i: (0, 0)),
                   pl.BlockSpec((nb, inner), lambda i: (0, 0))],
        out_shape=[jax.ShapeDtypeStruct((nb, d), F32), jax.ShapeDtypeStruct((nb, inner), F32)],
        compiler_params=_cparams("arbitrary"),
    )(uvz, uvz, uvz, x2d, v_ln_w.reshape(1, inner), v_ln_b.reshape(1, inner), w00, sb0, w_out_bf16)


def _blockdiag_coefs(w):
    n = w.shape[0]
    rows = []
    for d in range(-(ML_BLOCK - 1), ML_BLOCK):
        cols = []
        for i in range(ML_BLOCK):
            j = i + d
            cols.append(w[:, j, i] if 0 <= j < ML_BLOCK else jnp.zeros((n,), w.dtype))
        rows.append(jnp.stack(cols, axis=1).reshape(n * ML_BLOCK))
    return jnp.stack(rows, axis=0)


def _blockdiag_apply(x, coef_refs):
    width = x.shape[1]
    outs = [None] * len(coef_refs)
    for di, d in enumerate(range(-(ML_BLOCK - 1), ML_BLOCK)):
        xs = x if d == 0 else pltpu.roll(x, (-d) % width, 1)
        for n, cref in enumerate(coef_refs):
            t = xs * cref[di:di + 1, :]
            outs[n] = t if outs[n] is None else outs[n] + t
    return outs


def _mlstm_scan_kernel(xm_ref, z_ref, x_ref, cw_ref, cb_ref, wqk_ref, wvo_ref, bo_ref,
                       wifq_ref, wifk_ref, wifv_ref, bif_ref, mhw_ref, skip_ref, wout_ref,
                       out_ref, cst_ref, nst_ref, mst_ref,
                       buf_ref, act_ref, q_ref, k_ref, v_ref, o_ref, *, heads):
    c = pl.program_id(1)
    L = CHUNK
    inner = xm_ref.shape[1]
    hd = inner // heads
    scale = hd ** -0.5

    @pl.when(c == 0)
    def _():
        cst_ref[...] = jnp.zeros(cst_ref.shape, F32)
        nst_ref[...] = jnp.zeros(nst_ref.shape, F32)
        mst_ref[...] = jnp.zeros(mst_ref.shape, F32)

    _conv_tail(buf_ref, c == 0)
    _conv_fill(buf_ref, xm_ref, 0)
    _conv_silu(buf_ref, act_ref, cw_ref, cb_ref)

    tw = wqk_ref.shape[1]
    gates = bif_ref[...]
    for t in range(inner // tw):
        cols = slice(t * tw, (t + 1) * tw)
        qk = _bdot(_act_cols(act_ref, t * tw, tw), wqk_ref[t])
        vo = _bdot(xm_ref[:, cols], wvo_ref[t])
        q_ref[:, cols] = qk[:, 0:tw]
        k_ref[:, cols] = qk[:, tw:2 * tw]
        v_ref[:, cols] = vo[:, 0:tw]
        o_ref[:, cols] = _sigmoid(vo[:, tw:2 * tw] + bo_ref[:, cols])
        gates = (gates + _bdot(qk[:, 0:tw], wifq_ref[cols, :]) + _bdot(qk[:, tw:2 * tw], wifk_ref[cols, :])
                 + _bdot(vo[:, 0:tw], wifv_ref[cols, :]))
    lf = -_softplus(-gates)
    tri = _tril(L)
    bt = _hdot(tri.astype(F32), lf)
    g_t = gates.T
    b_t = bt.T
    mvec = mst_ref[0]
    hcols = [slice(h * hd, (h + 1) * hd) for h in range(heads)]
    qb = [q_ref[:, cols].astype(BF16) for cols in hcols]
    kf = [k_ref[:, cols] * scale for cols in hcols]
    kb = [k.astype(BF16) for k in kf]
    cmat = [cst_ref[0, h] for h in range(heads)]
    s_qk = [lax.dot_general(qb[h], kb[h], NT_DIMS, preferred_element_type=F32) for h in range(heads)]
    cq = [lax.dot_general(qb[h], cmat[h].astype(BF16), NT_DIMS, preferred_element_type=F32)
          for h in range(heads)]
    bcol, igcol, inter, mt, w, wi = [], [], [], [], [], []
    for h in range(heads):
        bcol.append(bt[:, heads + h:heads + h + 1])
        igcol.append(gates[:, h:h + 1])
        brow = b_t[heads + h:heads + h + 1, :]
        igrow = g_t[h:h + 1, :]
        d = jnp.where(tri, bcol[h] - brow + igrow, -jnp.inf)
        inter.append(bcol[h] + mvec[:, h:h + 1])
        mt.append(jnp.maximum(inter[h], jnp.max(d, axis=1, keepdims=True)))
        w.append(jnp.exp(d - mt[h]) * s_qk[h])
        wi.append(jnp.exp(inter[h] - mt[h]))
    num = [wi[h] * cq[h] + _bdot(w[h], v_ref[:, hcols[h]]) for h in range(heads)]
    for h in range(heads):
        cols = hcols[h]
        nrow = nst_ref[0, h:h + 1, :]
        nq = jnp.sum(q_ref[:, cols] * nrow, axis=1, keepdims=True)
        den = wi[h] * nq + jnp.sum(w[h], axis=1, keepdims=True)
        den = jnp.maximum(jnp.abs(den), jnp.exp(-mt[h]))
        hh = o_ref[:, cols] * (num[h] / den)
        mu = jnp.mean(hh, axis=1, keepdims=True)
        hc = hh - mu
        hn = hc * lax.rsqrt(jnp.mean(hc * hc, axis=1, keepdims=True) + EPS)
        hn = hn * mhw_ref[:, cols] + skip_ref[:, cols] * _act_cols(act_ref, h * hd, hd)
        part = _bdot(hn * _silu(z_ref[:, cols]), wout_ref[cols, :])
        if h == 0:
            out_ref[...] = x_ref[...] + part
        else:
            out_ref[...] += part
    for h in range(heads):
        cols = hcols[h]
        m_new = mt[h][L - 1:L, :]
        we = jnp.exp(bcol[h][L - 1:L, :] - bcol[h] + igcol[h] - m_new)
        dp = jnp.exp(inter[h][L - 1:L, :] - m_new)
        cst_ref[0, h] = dp * cmat[h] + jnp.dot((we * v_ref[:, cols]).T.astype(BF16), kb[h],
                                               preferred_element_type=F32)
        nst_ref[0, h:h + 1, :] = dp * nst_ref[0, h:h + 1, :] + jnp.sum(we * kf[h], axis=0, keepdims=True)
        mst_ref[0, :, h:h + 1] = m_new


def _blockdiag_tiles(w, tile):
    nt = w.shape[0] * ML_BLOCK // tile
    rows = jnp.tile(w.reshape(nt, tile, ML_BLOCK), (1, 1, tile // ML_BLOCK))
    blk = jnp.arange(tile) // ML_BLOCK
    return jnp.where(blk[:, None] == blk[None, :], rows, 0.0)


def _mlstm_consts(prm, inner):
    (norm_w, in_proj, conv_w, conv_b, w_q, w_k, w_v, w_o, b_o, w_if, b_if, mh_norm_w, skip,
     out_proj) = prm
    heads = ML_HEADS
    padn = LANES - 2 * heads
    wif = jnp.pad(w_if, ((0, 0), (0, padn))).astype(BF16)
    bif = jnp.pad(b_if, (0, padn)).reshape(1, LANES)
    return dict(
        conv_w=conv_w, conv_b=conv_b.reshape(1, inner),
        cq=_blockdiag_coefs(w_q), ck=_blockdiag_coefs(w_k), cv=_blockdiag_coefs(w_v),
        co=_blockdiag_coefs(w_o), bo=b_o.reshape(1, inner),
        wqk=jnp.concatenate([_blockdiag_tiles(w_q, MXU_TILE), _blockdiag_tiles(w_k, MXU_TILE)],
                            axis=2).astype(BF16),
        wvo=jnp.concatenate([_blockdiag_tiles(w_v, MXU_TILE), _blockdiag_tiles(w_o, MXU_TILE)],
                            axis=2).astype(BF16),
        wifq=wif[0:inner], wifk=wif[inner:2 * inner], wifv=wif[2 * inner:3 * inner], bif=bif,
        mhw=mh_norm_w.reshape(1, inner), skip=skip.reshape(1, inner))


def _mlstm_scan(xmz, x2d, bsz, seq, inner, cst, w_out_bf16):
    nc = seq // CHUNK
    d = x2d.shape[1]
    heads = ML_HEADS
    hd = inner // heads
    ntile = inner // MXU_TILE
    row = lambda b, c: b * nc + c
    c2 = lambda shape: pl.BlockSpec(shape, lambda b, c: (0, 0))
    c3 = lambda shape: pl.BlockSpec(shape, lambda b, c: (0, 0, 0))
    kern = functools.partial(_mlstm_scan_kernel, heads=heads)
    return pl.pallas_call(
        kern,
        name="mlstm_scan",
        grid=(bsz, nc),
        in_specs=[pl.BlockSpec((CHUNK, inner), lambda b, c: (row(b, c), 0)),
                  pl.BlockSpec((CHUNK, inner), lambda b, c: (row(b, c), 1)),
                  pl.BlockSpec((CHUNK, d), lambda b, c: (row(b, c), 0)),
                  c2((CONV_K, inner)), c2((1, inner)),
                  c3((ntile, MXU_TILE, 2 * MXU_TILE)), c3((ntile, MXU_TILE, 2 * MXU_TILE)),
                  c2((1, inner)),
                  c2((inner, LANES)), c2((inner, LANES)), c2((inner, LANES)), c2((1, LANES)),
                  c2((1, inner)), c2((1, inner)), c2((inner, d))],
        out_specs=[pl.BlockSpec((CHUNK, d), lambda b, c: (row(b, c), 0)),
                   pl.BlockSpec((1, heads, hd, hd), lambda b, c: (b, 0, 0, 0)),
                   pl.BlockSpec((1, heads, hd), lambda b, c: (b, 0, 0)),
                   pl.BlockSpec((1, 1, heads), lambda b, c: (b, 0, 0))],
        out_shape=[jax.ShapeDtypeStruct((bsz * seq, d), F32),
                   jax.ShapeDtypeStruct((bsz, heads, hd, hd), F32),
                   jax.ShapeDtypeStruct((bsz, heads, hd), F32),
                   jax.ShapeDtypeStruct((bsz, 1, heads), F32)],
        scratch_shapes=[pltpu.VMEM((inner // LANES, CONV_ROWS, LANES), F32)] * 2
                       + [pltpu.VMEM((CHUNK, inner), F32)] * 4,
        compiler_params=_cparams("parallel", "arbitrary"),
    )(xmz, xmz, x2d, cst["conv_w"], cst["conv_b"], cst["wqk"], cst["wvo"], cst["bo"],
      cst["wifq"], cst["wifk"], cst["wifv"], cst["bif"], cst["mhw"], cst["skip"], w_out_bf16)


def _mlstm_dec_pre_kernel(xm_ref, c0_ref, n0_ref, m0_ref, cw_ref, cb_ref, cq_ref, ck_ref, cv_ref,
                          co_ref, bo_ref, wifq_ref, wifk_ref, wifv_ref, bif_ref,
                          cnew_ref, q_ref, k_ref, v_ref, og_ref, xc_ref, wev_ref, dprep_ref,
                          wrep_ref, denrep_ref, nnew_ref, mnew_ref, *, heads):
    inner = xm_ref.shape[1]
    nb = xm_ref.shape[0]
    hd = inner // heads
    xm = xm_ref[...]
    acc = cb_ref[...] + cw_ref[CONV_K - 1:CONV_K, :] * xm
    for kk in range(CONV_K - 1):
        acc = acc + cw_ref[kk:kk + 1, :] * c0_ref[:, kk * inner:(kk + 1) * inner]
    for kk in range(CONV_K - 2):
        cnew_ref[:, kk * inner:(kk + 1) * inner] = c0_ref[:, (kk + 1) * inner:(kk + 2) * inner]
    cnew_ref[:, (CONV_K - 2) * inner:(CONV_K - 1) * inner] = xm
    xc = _silu(acc)
    q, k = _blockdiag_apply(xc, [cq_ref, ck_ref])
    v, o_pre = _blockdiag_apply(xm, [cv_ref, co_ref])
    gates = _bdot(q, wifq_ref[...]) + _bdot(k, wifk_ref[...]) + _bdot(v, wifv_ref[...]) + bif_ref[...]
    ksc = k * (hd ** -0.5)
    lane = lax.broadcasted_iota(jnp.int32, (nb, LANES), 1)
    ig = gates
    lf = pltpu.roll(-_softplus(-gates), LANES - heads, 1)
    n0 = n0_ref[...]
    qk = jnp.zeros((nb, LANES), F32)
    nq = jnp.zeros((nb, LANES), F32)
    for h in range(heads):
        cols = slice(h * hd, (h + 1) * hd)
        qk = jnp.where(lane == h, jnp.sum(q[:, cols] * ksc[:, cols], axis=1, keepdims=True), qk)
        nq = jnp.where(lane == h, jnp.sum(q[:, cols] * n0[:, cols], axis=1, keepdims=True), nq)
    inter = lf + m0_ref[...]
    mt = jnp.maximum(inter, ig)
    wi = jnp.exp(inter - mt)
    we = jnp.exp(ig - mt)
    w = we * qk
    den = jnp.maximum(jnp.abs(wi * nq + w), jnp.exp(-mt))
    mnew_ref[...] = mt
    sel = _expand_sel(LANES, hd, inner)
    wi_rep = _hdot(wi, sel)
    we_rep = _hdot(we, sel)
    dprep_ref[...] = wi_rep
    wrep_ref[...] = _hdot(w, sel)
    denrep_ref[...] = _hdot(den, sel)
    wev_ref[...] = we_rep * v
    nnew_ref[...] = wi_rep * n0 + we_rep * ksc
    q_ref[...] = q
    k_ref[...] = ksc
    v_ref[...] = v
    og_ref[...] = _sigmoid(o_pre + bo_ref[...])
    xc_ref[...] = xc


def _mlstm_dec_state_kernel(c_ref, q_ref, k_ref, dp_ref, wevT_ref, cn_ref, cqT_ref, *, heads):
    b = pl.program_id(0)
    hd = c_ref.shape[2]
    nb = cqT_ref.shape[1]
    lane = lax.broadcasted_iota(jnp.int32, (hd, nb), 1)
    msk = lane == b
    for h in range(heads):
        rows = slice(h * hd, (h + 1) * hd)
        cmat = c_ref[0, h]
        cqcol = jnp.sum(cmat * q_ref[0, h:h + 1, :], axis=1, keepdims=True)
        wev = jnp.sum(jnp.where(msk, wevT_ref[rows, :], 0.0), axis=1, keepdims=True)
        cn_ref[0, h] = dp_ref[0, h:h + 1, :] * cmat + wev * k_ref[0, h:h + 1, :]
        cqT_ref[rows, :] = jnp.where(msk, cqcol, cqT_ref[rows, :])


def _mlstm_dec_post_kernel(cq_ref, v_ref, dp_ref, w_ref, den_ref, og_ref, xc_ref, z_ref,
                           mhw_ref, skip_ref, g_ref, *, heads):
    inner = cq_ref.shape[1]
    hd = inner // heads
    for h in range(heads):
        cols = slice(h * hd, (h + 1) * hd)
        num = dp_ref[:, cols] * cq_ref[:, cols] + w_ref[:, cols] * v_ref[:, cols]
        hh = og_ref[:, cols] * (num / den_ref[:, cols])
        mu = jnp.mean(hh, axis=1, keepdims=True)
        hc = hh - mu
        hn = hc * lax.rsqrt(jnp.mean(hc * hc, axis=1, keepdims=True) + EPS)
        hn = hn * mhw_ref[:, cols] + skip_ref[:, cols] * xc_ref[:, cols]
        g_ref[:, cols] = hn * _silu(z_ref[:, cols])


def _mlstm_decode(xmz, c0, n0, m0, conv0, inner, cst):
    nb = xmz.shape[0]
    heads = ML_HEADS
    hd = inner // heads
    nco = 2 * ML_BLOCK - 1
    full = lambda shape: pl.BlockSpec(shape, lambda i: tuple(0 for _ in shape))
    m0p = jnp.pad(m0, ((0, 0), (0, LANES - heads)))
    kern = functools.partial(_mlstm_dec_pre_kernel, heads=heads)
    big = jax.ShapeDtypeStruct((nb, inner), F32)
    outs = pl.pallas_call(
        kern,
        name="mlstm_dec_pre",
        grid=(1,),
        in_specs=[pl.BlockSpec((nb, inner), lambda i: (0, 0)),
                  full((nb, (CONV_K - 1) * inner)), full((nb, inner)), full((nb, LANES)),
                  full((CONV_K, inner)), full((1, inner)),
                  full((nco, inner)), full((nco, inner)), full((nco, inner)), full((nco, inner)),
                  full((1, inner)),
                  full((inner, LANES)), full((inner, LANES)), full((inner, LANES)), full((1, LANES))],
        out_specs=[full((nb, (CONV_K - 1) * inner))] + [full((nb, inner))] * 10 + [full((nb, LANES))],
        out_shape=[jax.ShapeDtypeStruct((nb, (CONV_K - 1) * inner), F32)] + [big] * 10
                  + [jax.ShapeDtypeStruct((nb, LANES), F32)],
        compiler_params=_cparams("arbitrary"),
    )(xmz, conv0.reshape(nb, (CONV_K - 1) * inner), n0.reshape(nb, inner), m0p,
      cst["conv_w"], cst["conv_b"], cst["cq"], cst["ck"], cst["cv"], cst["co"], cst["bo"],
      cst["wifq"], cst["wifk"], cst["wifv"], cst["bif"])
    cnew, q, ksc, v, og, xc, wev, dprep, wrep, denrep, nnew, mnew = outs

    kern = functools.partial(_mlstm_dec_state_kernel, heads=heads)
    h3 = lambda: pl.BlockSpec((1, heads, hd), lambda i: (i, 0, 0))
    c_new, cq_t = pl.pallas_call(
        kern,
        name="mlstm_dec_state",
        grid=(nb,),
        in_specs=[pl.BlockSpec((1, heads, hd, hd), lambda i: (i, 0, 0, 0)),
                  h3(), h3(), h3(),
                  pl.BlockSpec((inner, nb), lambda i: (0, 0))],
        out_specs=[pl.BlockSpec((1, heads, hd, hd), lambda i: (i, 0, 0, 0)),
                   pl.BlockSpec((inner, nb), lambda i: (0, 0))],
        out_shape=[jax.ShapeDtypeStruct((nb, heads, hd, hd), F32),
                   jax.ShapeDtypeStruct((inner, nb), F32)],
        compiler_params=_cparams("arbitrary"),
    )(c0, q.reshape(nb, heads, hd), ksc.reshape(nb, heads, hd), dprep.reshape(nb, heads, hd), wev.T)

    kern = functools.partial(_mlstm_dec_post_kernel, heads=heads)
    g = pl.pallas_call(
        kern,
        name="mlstm_dec_post",
        grid=(1,),
        in_specs=[full((nb, inner))] * 7
                 + [pl.BlockSpec((nb, inner), lambda i: (0, 1)), full((1, inner)), full((1, inner))],
        out_specs=full((nb, inner)),
        out_shape=big,
        compiler_params=_cparams("arbitrary"),
    )(cq_t.T, v, dprep, wrep, denrep, og, xc, xmz, cst["mhw"], cst["skip"])
    return (g, c_new, nnew.reshape(nb, heads, hd), mnew[:, 0:heads],
            cnew.reshape(nb, CONV_K - 1, inner))


def _ssd_in_weights(in_proj, inner, conv_dim):
    ncol = inner + conv_dim
    pad = LANES - (in_proj.shape[1] - ncol)
    return jnp.pad(in_proj.astype(BF16), ((0, 0), (0, pad)))


def _ssd_layer(xp, xs, ssm0, conv0, prm, final_w=None):
    norm_w, in_proj, conv_w, conv_b, dt_bias, a_log, d_skip, gnorm_w, out_proj = prm
    bsz, seq, d = xp.shape
    nb = xs.shape[0]
    inner = out_proj.shape[0]
    heads = a_log.shape[0]
    conv_dim = conv_w.shape[1]
    w_in = _ssd_in_weights(in_proj, inner, conv_dim)
    w_out = out_proj.astype(BF16)
    xp2 = xp.reshape(bsz * seq, d)
    xs2 = xs.reshape(nb, d)

    zx_p = _norm_matmul(xp2, norm_w, w_in)
    out_p, h_p = _ssd_scan(zx_p, xp2, bsz, seq, inner, heads, conv_w, conv_b, dt_bias, a_log, d_skip,
                           gnorm_w, w_out, final_w)
    conv_p = zx_p.reshape(bsz, seq, -1)[:, seq - (CONV_K - 1):, inner:inner + conv_dim]

    zx_s = _norm_matmul(xs2, norm_w, w_in)
    out_s, h_s, conv_s = _ssd_decode(zx_s, xs2, ssm0, conv0, inner, heads, prm, w_out, final_w=final_w)
    return (out_p.reshape(bsz, seq, d), out_s.reshape(nb, 1, d),
            h_p.reshape(bsz, heads, SSD_HEAD_DIM, SSD_STATE), conv_p, h_s, conv_s)


def _gmlp_layer(xp, xs, prm):
    norm_w, in_proj, v_ln_w, v_ln_b, spatial_w, spatial_b, out_proj = prm
    bsz, seq, d = xp.shape
    nb = xs.shape[0]
    inner = out_proj.shape[0]
    w_in = in_proj.astype(BF16)
    w_out = out_proj.astype(BF16)
    xp2 = xp.reshape(bsz * seq, d)
    xs2 = xs.reshape(nb, d)
    out_p = _gmlp_prompt(xp2, norm_w, w_in, inner, v_ln_w, v_ln_b, spatial_w, spatial_b, w_out)
    uvz_s = _norm_matmul(xs2, norm_w, w_in)
    out_s, vn_s = _gmlp_decode(uvz_s, xs2, inner, v_ln_w, v_ln_b, spatial_w, spatial_b, w_out)
    return out_p.reshape(bsz, seq, d), out_s.reshape(nb, 1, d), vn_s.reshape(nb, 1, inner)


def _mlstm_layer(xp, xs, c0, n0, m0, conv0, prm):
    norm_w, in_proj = prm[0], prm[1]
    out_proj = prm[-1]
    bsz, seq, d = xp.shape
    nb = xs.shape[0]
    inner = out_proj.shape[0]
    w_in = in_proj.astype(BF16)
    w_out = out_proj.astype(BF16)
    cst = _mlstm_consts(prm, inner)
    xp2 = xp.reshape(bsz * seq, d)
    xs2 = xs.reshape(nb, d)

    xmz_p = _norm_matmul(xp2, norm_w, w_in)
    out_p, c_p, n_p, m_p = _mlstm_scan(xmz_p, xp2, bsz, seq, inner, cst, w_out)
    conv_p = xmz_p.reshape(bsz, seq, -1)[:, seq - (CONV_K - 1):, 0:inner]

    xmz_s = _norm_matmul(xs2, norm_w, w_in)
    g_s, c_s, n_s, m_s, conv_s = _mlstm_decode(xmz_s, c0, n0, m0, conv0, inner, cst)
    out_s = _matmul_res(g_s, w_out, xs2)
    return (out_p.reshape(bsz, seq, d), out_s.reshape(nb, 1, d),
            c_p, n_p, m_p.reshape(bsz, ML_HEADS), conv_p, c_s, n_s, m_s, conv_s)


def kernel(x_prompt, x_sample, state_l0_ssm, state_l0_conv, state_l2_C, state_l2_n, state_l2_m, state_l2_conv, state_l3_ssm, state_l3_conv, l0_norm_w, l0_in_proj, l0_conv_w, l0_conv_b, l0_dt_bias, l0_A_log, l0_D_skip, l0_gnorm_w, l0_out_proj, l1_norm_w, l1_in_proj, l1_v_ln_w, l1_v_ln_b, l1_spatial_w, l1_spatial_b, l1_out_proj, l2_norm_w, l2_in_proj, l2_conv_w, l2_conv_b, l2_w_q, l2_w_k, l2_w_v, l2_w_o, l2_b_o, l2_w_if, l2_b_if, l2_mh_norm_w, l2_skip, l2_out_proj, l3_norm_w, l3_in_proj, l3_conv_w, l3_conv_b, l3_dt_bias, l3_A_log, l3_D_skip, l3_gnorm_w, l3_out_proj, final_norm_w):
    p0 = (l0_norm_w, l0_in_proj, l0_conv_w, l0_conv_b, l0_dt_bias, l0_A_log, l0_D_skip, l0_gnorm_w, l0_out_proj)
    p1 = (l1_norm_w, l1_in_proj, l1_v_ln_w, l1_v_ln_b, l1_spatial_w, l1_spatial_b, l1_out_proj)
    p2 = (l2_norm_w, l2_in_proj, l2_conv_w, l2_conv_b, l2_w_q, l2_w_k, l2_w_v, l2_w_o, l2_b_o,
          l2_w_if, l2_b_if, l2_mh_norm_w, l2_skip, l2_out_proj)
    p3 = (l3_norm_w, l3_in_proj, l3_conv_w, l3_conv_b, l3_dt_bias, l3_A_log, l3_D_skip, l3_gnorm_w, l3_out_proj)

    hp, hs, p0_ssm, p0_conv, s0_ssm, s0_conv = _ssd_layer(x_prompt, x_sample, state_l0_ssm, state_l0_conv, p0)
    hp, hs, s1_v = _gmlp_layer(hp, hs, p1)
    hp, hs, p2_C, p2_n, p2_m, p2_conv, s2_C, s2_n, s2_m, s2_conv = _mlstm_layer(
        hp, hs, state_l2_C, state_l2_n, state_l2_m, state_l2_conv, p2)
    y_prompt, y_sample, p3_ssm, p3_conv, s3_ssm, s3_conv = _ssd_layer(
        hp, hs, state_l3_ssm, state_l3_conv, p3, final_w=final_norm_w)
    return (y_prompt, y_sample,
            p0_ssm, p0_conv, s0_ssm, s0_conv,
            s1_v,
            p2_C, p2_n, p2_m, p2_conv, s2_C, s2_n, s2_m, s2_conv,
            p3_ssm, p3_conv, s3_ssm, s3_conv)
```

```python
import functools
import math

import jax
import jax.numpy as jnp
from jax import lax
from jax.experimental import pallas as pl
from jax.experimental.pallas import tpu as pltpu

F32 = jnp.float32
BF16 = jnp.bfloat16
EPS = 1e-6
CONV_K = 4
CHUNK = 128
LANES = 128
MXU_TILE = 256
SUBLANES = 8
SSD_HEAD_DIM = 64
SSD_STATE = 128
SSD_GROUPS = 8
OUT_GROUPS = 2
ML_HEADS = 4
ML_BLOCK = 4
GM_GROUPS = 8
VMEM_LIMIT = 56 * 1024 * 1024
HI = lax.Precision.HIGHEST
LOG2E = 1.4426950408889634
NT_DIMS = (((1,), (1,)), ((), ()))


def _cparams(*sem):
    return pltpu.CompilerParams(dimension_semantics=sem, vmem_limit_bytes=VMEM_LIMIT)


def _sigmoid(x):
    return 1.0 / (1.0 + jnp.exp2(x * (-LOG2E)))


def _silu(x):
    h = 0.5 * x
    return h + h * jnp.tanh(h)


def _softplus(x):
    return jnp.maximum(x, 0.0) + jnp.log(1.0 + jnp.exp(-jnp.abs(x)))


def _bdot(a, b):
    return jnp.dot(a.astype(BF16), b.astype(BF16), preferred_element_type=F32)


def _bdot_nt(a, b):
    return lax.dot_general(a.astype(BF16), b.astype(BF16), NT_DIMS, preferred_element_type=F32)


def _hdot(a, b):
    return jnp.dot(a, b, precision=HI, preferred_element_type=F32)


def _rms(x, w):
    return x * lax.rsqrt(jnp.mean(x * x, axis=-1, keepdims=True) + EPS) * w


def _tril(n):
    r = lax.broadcasted_iota(jnp.int32, (n, n), 0)
    c = lax.broadcasted_iota(jnp.int32, (n, n), 1)
    return r >= c


def _expand_sel(n_in, width, n_out):
    r = lax.broadcasted_iota(jnp.int32, (n_in, n_out), 0)
    c = lax.broadcasted_iota(jnp.int32, (n_in, n_out), 1)
    return (c // width == r).astype(F32)


def _split_bf16(x):
    hi = x.astype(BF16)
    return hi, (x - hi.astype(F32)).astype(BF16)


def _expand(pieces, sel_bf16):
    return (jnp.dot(pieces[0], sel_bf16, preferred_element_type=F32)
            + jnp.dot(pieces[1], sel_bf16, preferred_element_type=F32))


def _norm_matmul_kernel(x_ref, nw_ref, w_ref, o_ref):
    o_ref[...] = _bdot(_rms(x_ref[...], nw_ref[...]), w_ref[...])


def _norm_matmul(x2d, norm_w, w_bf16):
    m, k = x2d.shape
    n = w_bf16.shape[1]
    tm = min(m, 512)
    return pl.pallas_call(
        _norm_matmul_kernel,
        name="norm_matmul",
        grid=(m // tm,),
        in_specs=[pl.BlockSpec((tm, k), lambda i: (i, 0)),
                  pl.BlockSpec((1, k), lambda i: (0, 0)),
                  pl.BlockSpec((k, n), lambda i: (0, 0), pipeline_mode=pl.Buffered(1))],
        out_specs=pl.BlockSpec((tm, n), lambda i: (i, 0)),
        out_shape=jax.ShapeDtypeStruct((m, n), F32),
        compiler_params=_cparams("parallel"),
    )(x2d, norm_w.reshape(1, k), w_bf16)


def _matmul_res_kernel(g_ref, w_ref, x_ref, o_ref):
    o_ref[...] = x_ref[...] + _bdot(g_ref[...], w_ref[...])


def _matmul_res(g2d, w_bf16, x2d):
    m, k = g2d.shape
    n = w_bf16.shape[1]
    tm = min(m, 512)
    return pl.pallas_call(
        _matmul_res_kernel,
        name="matmul_res",
        grid=(m // tm,),
        in_specs=[pl.BlockSpec((tm, k), lambda i: (i, 0)),
                  pl.BlockSpec((k, n), lambda i: (0, 0)),
                  pl.BlockSpec((tm, n), lambda i: (i, 0))],
        out_specs=pl.BlockSpec((tm, n), lambda i: (i, 0)),
        out_shape=jax.ShapeDtypeStruct((m, n), F32),
        compiler_params=_cparams("parallel"),
    )(g2d, w_bf16, x2d)


CONV_ROWS = CHUNK + SUBLANES
CONV_VREGS = CONV_ROWS // SUBLANES
CUR = slice(SUBLANES, CONV_ROWS)


def _conv_tail(buf_ref, is_first):
    lead = (slice(None),) * (len(buf_ref.shape) - 2)

    @pl.when(is_first)
    def _():
        buf_ref[lead + (slice(0, SUBLANES), slice(None))] = jnp.zeros(
            buf_ref.shape[:-2] + (SUBLANES, LANES), F32)

    @pl.when(jnp.logical_not(is_first))
    def _():
        buf_ref[lead + (slice(0, SUBLANES), slice(None))] = buf_ref[lead + (slice(CHUNK, CONV_ROWS), slice(None))]


def _conv_fill(buf_ref, src_ref, col0):
    t0 = col0 // LANES
    for j in range(src_ref.shape[1] // LANES):
        buf_ref[t0 + j, CUR, :] = src_ref[:, j * LANES:(j + 1) * LANES]


def _conv_silu(buf_ref, act_ref, cw_ref, cb_ref):
    nv = CONV_VREGS
    for j in range(buf_ref.shape[0]):
        cols = slice(j * LANES, (j + 1) * LANES)
        v = [buf_ref[j, pl.ds(a, SUBLANES, stride=nv), :] for a in range(nv)]
        wrap = {a: pltpu.roll(v[a], 1, 0) for a in range(nv - (CONV_K - 1), nv)}
        coef = [jnp.broadcast_to(cw_ref[k:k + 1, cols], (SUBLANES, LANES)) for k in range(CONV_K)]
        bias = jnp.broadcast_to(cb_ref[:, cols], (SUBLANES, LANES))
        for a in range(nv):
            acc = bias + coef[CONV_K - 1] * v[a]
            for back in range(1, CONV_K):
                tap = v[a - back] if a >= back else wrap[a - back + nv]
                acc = acc + coef[CONV_K - 1 - back] * tap
            act_ref[j, pl.ds(a, SUBLANES, stride=nv), :] = _silu(acc)


def _act_cols(act_ref, col0, width):
    t0 = col0 // LANES
    tiles = [act_ref[t0 + j, CUR, :] for j in range(width // LANES)]
    return tiles[0] if len(tiles) == 1 else jnp.concatenate(tiles, axis=1)


def _ssd_scan_kernel(*refs, inner, heads, has_final, streams):
    (xs_ref, bc_ref, dt_ref, z_ref, x_ref, cw_ref, cb_ref, dtb_ref, alog_ref, drep_ref, sel_ref,
     gnw_ref, wout_ref) = refs[:13]
    fin_ref = refs[13] if has_final else None
    o_ref, h_ref, buf_ref, act_ref = refs[13 + has_final:]
    c = pl.program_id(1)
    L = CHUNK
    gn = SSD_GROUPS * SSD_STATE
    rep = heads // SSD_GROUPS
    gw = rep * SSD_HEAD_DIM
    S = range(streams)

    @pl.when(c == 0)
    def _():
        h_ref[...] = jnp.zeros(h_ref.shape, F32)

    _conv_tail(buf_ref, c == 0)
    for s in S:
        _conv_fill(buf_ref.at[s], xs_ref.at[s], 0)
        _conv_fill(buf_ref.at[s], bc_ref.at[s], inner)
    for s in S:
        _conv_silu(buf_ref.at[s], act_ref.at[s], cw_ref, cb_ref)

    lane = lax.broadcasted_iota(jnp.int32, (1, LANES), 1)
    a_neg = jnp.where(lane < heads, -jnp.exp(alog_ref[...]), 0.0)
    tri = _tril(L)
    cum2, cum2_t, dt_t, dend, e_hi, e_lo, w_hi, w_lo = [], [], [], [], [], [], [], []
    for s in S:
        dt = _softplus(dt_ref[s] + dtb_ref[...])
        cm = _hdot(tri.astype(F32), dt * a_neg)
        cum2.append(cm * LOG2E)
        cum2_t.append(cum2[s].T)
        dt_t.append(dt.T)
        cum_end = cm[L - 1:L, :]
        dend.append(jnp.exp(cum_end))
        hi, lo = _split_bf16(jnp.exp(cm))
        e_hi.append(hi)
        e_lo.append(lo)
        hi, lo = _split_bf16(jnp.exp(cum_end - cm) * dt)
        w_hi.append(hi)
        w_lo.append(lo)
    e_pieces = (jnp.concatenate(e_hi, axis=0), jnp.concatenate(e_lo, axis=0))
    w_pieces = (jnp.concatenate(w_hi, axis=0), jnp.concatenate(w_lo, axis=0))

    lane_g = lax.broadcasted_iota(jnp.int32, (L, gw), 1) // SSD_HEAD_DIM
    pend = []
    for g in range(SSD_GROUPS):
        rows = slice(g * gw, (g + 1) * gw)
        e_all = _expand(e_pieces, sel_ref[:, rows])
        w_all = _expand(w_pieces, sel_ref[:, rows])
        srow = [slice(s * L, (s + 1) * L) for s in S]
        bg = [_act_cols(act_ref.at[s], inner + g * SSD_STATE, SSD_STATE).astype(BF16) for s in S]
        cg = [_act_cols(act_ref.at[s], inner + gn + g * SSD_STATE, SSD_STATE).astype(BF16) for s in S]
        xg = [_act_cols(act_ref.at[s], g * gw, gw) for s in S]
        hg = [h_ref[s, 0, rows, :] for s in S]
        cb = [lax.dot_general(cg[s], bg[s], NT_DIMS, preferred_element_type=F32) for s in S]
        yi = [lax.dot_general(cg[s], hg[s].astype(BF16), NT_DIMS, preferred_element_type=F32) for s in S]
        mixcat, xblk = [], []
        for s in S:
            mixes = []
            for r in range(rep):
                h = g * rep + r
                seg = cum2[s][:, h:h + 1] - cum2_t[s][h:h + 1, :]
                decay = jnp.exp2(jnp.where(tri, seg, -jnp.inf))
                mixes.append((cb[s] * decay * dt_t[s][h:h + 1, :]).astype(BF16))
            mixcat.append(jnp.concatenate(mixes, axis=1))
            xgb = xg[s].astype(BF16)
            xblk.append(jnp.concatenate(
                [jnp.where(lane_g == r, xgb, jnp.zeros_like(xgb)) for r in range(rep)], axis=0))
        y = [jnp.dot(mixcat[s], xblk[s], preferred_element_type=F32) for s in S]
        upd = [jnp.dot((xg[s] * w_all[srow[s], :]).T.astype(BF16), bg[s], preferred_element_type=F32)
               for s in S]
        yn = []
        for s in S:
            yy = y[s] + yi[s] * e_all[srow[s], :] + drep_ref[:, rows] * xg[s]
            yy = yy * _silu(z_ref[s, :, rows])
            yy = yy * lax.rsqrt(jnp.mean(yy * yy, axis=-1, keepdims=True) + EPS) * gnw_ref[:, rows]
            yn.append(yy.astype(BF16))
        for s in S:
            for r in range(rep):
                h = g * rep + r
                hr = slice(r * SSD_HEAD_DIM, (r + 1) * SSD_HEAD_DIM)
                h_ref[s, 0, g * gw + r * SSD_HEAD_DIM:g * gw + (r + 1) * SSD_HEAD_DIM, :] = (
                    dend[s][:, h:h + 1] * hg[s][hr, :] + upd[s][hr, :])
        pend.append(jnp.concatenate(yn, axis=0))
        if len(pend) == OUT_GROUPS:
            g0 = g + 1 - OUT_GROUPS
            part = jnp.dot(jnp.concatenate(pend, axis=1), wout_ref[g0 * gw:(g + 1) * gw, :],
                           preferred_element_type=F32)
            pend = []
            for s in S:
                srows = slice(s * L, (s + 1) * L)
                if g0 == 0:
                    o_ref[s] = x_ref[s] + part[srows, :]
                else:
                    o_ref[s] += part[srows, :]
    if has_final:
        for s in S:
            o_ref[s] = _rms(o_ref[s], fin_ref[...])


def _ssd_scan(zx, x2d, bsz, seq, inner, heads, conv_w, conv_b, dt_bias, a_log, d_skip, gnorm_w,
              w_out_bf16, final_w):
    nc = seq // CHUNK
    d = x2d.shape[1]
    gn = SSD_GROUPS * SSD_STATE
    conv_dim = inner + 2 * gn
    assert inner // SSD_GROUPS == (heads // SSD_GROUPS) * SSD_HEAD_DIM
    xs_blk = inner // inner
    bc_blk = (2 * inner) // (2 * gn)
    dt_blk = (inner + conv_dim) // LANES
    pad = LANES - heads
    dtb = jnp.pad(dt_bias, (0, pad)).reshape(1, LANES)
    alog = jnp.pad(a_log, (0, pad)).reshape(1, LANES)
    drep = jnp.repeat(d_skip, SSD_HEAD_DIM).reshape(1, inner)
    sel = (jnp.arange(inner)[None, :] // SSD_HEAD_DIM == jnp.arange(LANES)[:, None]).astype(BF16)
    has_final = final_w is not None
    streams = 2 if bsz % 2 == 0 else 1
    bh = bsz // streams
    kern = functools.partial(_ssd_scan_kernel, inner=inner, heads=heads, has_final=has_final,
                             streams=streams)
    row = lambda b, c: b * nc + c
    c2 = lambda shape: pl.BlockSpec(shape, lambda b, c: (0, 0))
    zx3 = zx.reshape(streams, bh * seq, zx.shape[1])
    x3 = x2d.reshape(streams, bh * seq, d)
    ins = [zx3, zx3, zx3, zx3, x3, conv_w, conv_b.reshape(1, conv_dim), dtb, alog, drep, sel,
           gnorm_w.reshape(1, inner), w_out_bf16]
    specs = [pl.BlockSpec((streams, CHUNK, inner), lambda b, c: (0, row(b, c), xs_blk)),
             pl.BlockSpec((streams, CHUNK, 2 * gn), lambda b, c: (0, row(b, c), bc_blk)),
             pl.BlockSpec((streams, CHUNK, LANES), lambda b, c: (0, row(b, c), dt_blk)),
             pl.BlockSpec((streams, CHUNK, inner), lambda b, c: (0, row(b, c), 0)),
             pl.BlockSpec((streams, CHUNK, d), lambda b, c: (0, row(b, c), 0)),
             c2((CONV_K, conv_dim)), c2((1, conv_dim)), c2((1, LANES)), c2((1, LANES)),
             c2((1, inner)), c2((LANES, inner)), c2((1, inner)), c2((inner, d))]
    if has_final:
        ins.append(final_w.reshape(1, d))
        specs.append(c2((1, d)))
    out, h = pl.pallas_call(
        kern,
        name="ssd_scan",
        grid=(bh, nc),
        in_specs=specs,
        out_specs=[pl.BlockSpec((streams, CHUNK, d), lambda b, c: (0, row(b, c), 0)),
                   pl.BlockSpec((streams, 1, inner, SSD_STATE), lambda b, c: (0, b, 0, 0))],
        out_shape=[jax.ShapeDtypeStruct((streams, bh * seq, d), F32),
                   jax.ShapeDtypeStruct((streams, bh, inner, SSD_STATE), F32)],
        scratch_shapes=[pltpu.VMEM((streams, conv_dim // LANES, CONV_ROWS, LANES), F32),
                        pltpu.VMEM((streams, conv_dim // LANES, CONV_ROWS, LANES), F32)],
        compiler_params=_cparams("parallel", "arbitrary"),
    )(*ins)
    return out.reshape(bsz * seq, d), h.reshape(bsz, inner, SSD_STATE)


def _ssd_post_kernel(*refs, has_add, has_final):
    y_ref, z_ref, x_ref, gw_ref, w_ref = refs[:5]
    pos = 5
    add_ref = fin_ref = None
    if has_add:
        add_ref = refs[pos]
        pos += 1
    if has_final:
        fin_ref = refs[pos]
        pos += 1
    o_ref = refs[pos]
    inner = y_ref.shape[1]
    gwid = inner // SSD_GROUPS
    parts = []
    for g in range(SSD_GROUPS):
        cols = slice(g * gwid, (g + 1) * gwid)
        y = y_ref[:, cols]
        if has_add:
            y = y + add_ref[:, cols]
        y = y * _silu(z_ref[:, cols])
        y = y * lax.rsqrt(jnp.mean(y * y, axis=-1, keepdims=True) + EPS)
        parts.append((y * gw_ref[:, cols]).astype(BF16))
    out = x_ref[...] + jnp.dot(jnp.concatenate(parts, axis=1), w_ref[...],
                               preferred_element_type=F32)
    if has_final:
        out = _rms(out, fin_ref[...])
    o_ref[...] = out


def _ssd_post(y, zx, x2d, gnorm_w, w_out_bf16, add=None, final_w=None):
    m, inner = y.shape
    d = x2d.shape[1]
    tm = min(m, 512)
    ins = [y, zx, x2d, gnorm_w.reshape(1, inner), w_out_bf16]
    specs = [pl.BlockSpec((tm, inner), lambda i: (i, 0)),
             pl.BlockSpec((tm, inner), lambda i: (i, 0)),
             pl.BlockSpec((tm, d), lambda i: (i, 0)),
             pl.BlockSpec((1, inner), lambda i: (0, 0)),
             pl.BlockSpec((inner, d), lambda i: (0, 0))]
    if add is not None:
        ins.append(add)
        specs.append(pl.BlockSpec((tm, inner), lambda i: (i, 0)))
    if final_w is not None:
        ins.append(final_w.reshape(1, d))
        specs.append(pl.BlockSpec((1, d), lambda i: (0, 0)))
    kern = functools.partial(_ssd_post_kernel, has_add=add is not None, has_final=final_w is not None)
    return pl.pallas_call(
        kern,
        name="ssd_post",
        grid=(m // tm,),
        in_specs=specs,
        out_specs=pl.BlockSpec((tm, d), lambda i: (i, 0)),
        out_shape=jax.ShapeDtypeStruct((m, d), F32),
        compiler_params=_cparams("parallel"),
    )(*ins)


def _ssd_dec_pre_kernel(xbc_ref, dt_ref, c0_ref, cw_ref, cb_ref, dtb_ref, alog_ref, drep_ref,
                        cnew_ref, dtx_ref, da_ref, b_ref, c_ref, dx_ref, *, inner, heads):
    gn = SSD_GROUPS * SSD_STATE
    cd = inner + 2 * gn
    xnew = xbc_ref[...]
    acc = cb_ref[...] + cw_ref[CONV_K - 1:CONV_K, :] * xnew
    for k in range(CONV_K - 1):
        acc = acc + cw_ref[k:k + 1, :] * c0_ref[:, k * cd:(k + 1) * cd]
    for k in range(CONV_K - 2):
        cnew_ref[:, k * cd:(k + 1) * cd] = c0_ref[:, (k + 1) * cd:(k + 2) * cd]
    cnew_ref[:, (CONV_K - 2) * cd:(CONV_K - 1) * cd] = xnew
    act = _silu(acc)
    xs = act[:, 0:inner]
    b_ref[...] = act[:, inner:inner + gn]
    c_ref[...] = act[:, inner + gn:inner + 2 * gn]
    lane = lax.broadcasted_iota(jnp.int32, (1, LANES), 1)
    dt = _softplus(dt_ref[...] + dtb_ref[...])
    a_neg = jnp.where(lane < heads, -jnp.exp(alog_ref[...]), 0.0)
    da_ref[...] = jnp.exp(dt * a_neg)
    dtx_ref[...] = _hdot(dt, _expand_sel(LANES, SSD_HEAD_DIM, inner)) * xs
    dx_ref[...] = drep_ref[...] * xs


def _ssd_dec_state_kernel(da_ref, h_ref, dtxT_ref, b_ref, c_ref, hn_ref, yT_ref, *, bb, rep):
    i = pl.program_id(0)
    gw = rep * SSD_HEAD_DIM
    nb = yT_ref.shape[1]
    lane = lax.broadcasted_iota(jnp.int32, (gw, nb), 1)
    brow = lax.broadcasted_iota(jnp.int32, (nb, SSD_STATE), 0)

    @pl.when(i == 0)
    def _():
        yT_ref[...] = jnp.zeros(yT_ref.shape, F32)

    def body(bi, carry):
        bglob = i * bb + bi
        msk = lane == bglob
        pick = brow == bglob
        bmat = b_ref[bi]
        cmat = c_ref[bi]
        hnews = []
        for g in range(SSD_GROUPS):
            rows = slice(g * gw, (g + 1) * gw)
            dtx = jnp.sum(jnp.where(msk, dtxT_ref[rows, :], 0.0), axis=1, keepdims=True)
            hdec = jnp.concatenate(
                [da_ref[bglob, g * rep + r] * h_ref[bi, g * gw + r * SSD_HEAD_DIM:g * gw + (r + 1) * SSD_HEAD_DIM, :]
                 for r in range(rep)], axis=0)
            hnew = hdec + dtx * bmat[g:g + 1, :]
            hn_ref[bi, rows, :] = hnew
            hnews.append(hnew.astype(BF16))
        for g in range(SSD_GROUPS):
            rows = slice(g * gw, (g + 1) * gw)
            crow = jnp.where(pick, jnp.broadcast_to(cmat[g:g + 1, :], (nb, SSD_STATE)), 0.0).astype(BF16)
            yT_ref[rows, :] += lax.dot_general(hnews[g], crow, NT_DIMS, preferred_element_type=F32)
        return carry

    lax.fori_loop(0, bb, body, 0)


def _ssd_decode(zx, x2d, ssm0, conv0, inner, heads, prm, w_out_bf16, final_w=None):
    norm_w, in_proj, conv_w, conv_b, dt_bias, a_log, d_skip, gnorm_w, out_proj = prm
    nb = zx.shape[0]
    gn = SSD_GROUPS * SSD_STATE
    cd = inner + 2 * gn
    pad = LANES - heads
    dtb = jnp.pad(dt_bias, (0, pad)).reshape(1, LANES)
    alog = jnp.pad(a_log, (0, pad)).reshape(1, LANES)
    drep = jnp.repeat(d_skip, SSD_HEAD_DIM).reshape(1, inner)
    full = lambda shape: pl.BlockSpec(shape, lambda i: tuple(0 for _ in shape))
    kern = functools.partial(_ssd_dec_pre_kernel, inner=inner, heads=heads)
    cnew, dtx, da, bact, cact, dx = pl.pallas_call(
        kern,
        name="ssd_dec_pre",
        grid=(1,),
        in_specs=[full((nb, cd)), full((nb, LANES)), full((nb, (CONV_K - 1) * cd)), full((CONV_K, cd)), full((1, cd)),
                  full((1, LANES)), full((1, LANES)), full((1, inner))],
        out_specs=[full((nb, (CONV_K - 1) * cd)), full((nb, inner)), full((nb, LANES)),
                   full((nb, gn)), full((nb, gn)), full((nb, inner))],
        out_shape=[jax.ShapeDtypeStruct((nb, (CONV_K - 1) * cd), F32),
                   jax.ShapeDtypeStruct((nb, inner), F32), jax.ShapeDtypeStruct((nb, LANES), F32),
                   jax.ShapeDtypeStruct((nb, gn), F32), jax.ShapeDtypeStruct((nb, gn), F32),
                   jax.ShapeDtypeStruct((nb, inner), F32)],
        compiler_params=_cparams("arbitrary"),
    )(zx[:, inner:inner + cd], zx[:, inner + cd:inner + cd + LANES],
      conv0.reshape(nb, (CONV_K - 1) * cd), conv_w, conv_b.reshape(1, cd), dtb, alog, drep)

    bb = 8 if nb % 8 == 0 else 1
    rep = heads // SSD_GROUPS
    kern = functools.partial(_ssd_dec_state_kernel, bb=bb, rep=rep)
    hnew, y_t = pl.pallas_call(
        kern,
        name="ssd_dec_state",
        grid=(nb // bb,),
        in_specs=[pl.BlockSpec(memory_space=pltpu.SMEM),
                  pl.BlockSpec((bb, inner, SSD_STATE), lambda i: (i, 0, 0)),
                  pl.BlockSpec((inner, nb), lambda i: (0, 0)),
                  pl.BlockSpec((bb, SSD_GROUPS, SSD_STATE), lambda i: (i, 0, 0)),
                  pl.BlockSpec((bb, SSD_GROUPS, SSD_STATE), lambda i: (i, 0, 0))],
        out_specs=[pl.BlockSpec((bb, inner, SSD_STATE), lambda i: (i, 0, 0)),
                   pl.BlockSpec((inner, nb), lambda i: (0, 0))],
        out_shape=[jax.ShapeDtypeStruct((nb, inner, SSD_STATE), F32),
                   jax.ShapeDtypeStruct((inner, nb), F32)],
        compiler_params=_cparams("arbitrary"),
    )(da[:, 0:heads], ssm0.reshape(nb, inner, SSD_STATE), dtx.T,
      bact.reshape(nb, SSD_GROUPS, SSD_STATE), cact.reshape(nb, SSD_GROUPS, SSD_STATE))

    out = _ssd_post(y_t.T, zx, x2d, gnorm_w, w_out_bf16, add=dx, final_w=final_w)
    return out, hnew.reshape(nb, heads, SSD_HEAD_DIM, SSD_STATE), cnew.reshape(nb, CONV_K - 1, cd)


def _layernorm(v, w, b):
    mu = jnp.mean(v, axis=-1, keepdims=True)
    vc = v - mu
    return vc * lax.rsqrt(jnp.mean(vc * vc, axis=-1, keepdims=True) + EPS) * w + b


def _gmlp_kernel(x_ref, nw_ref, win_ref, lw_ref, lb_ref, ws_ref, sbT_ref, w_ref, o_ref, *, nck):
    inner = w_ref.shape[0]
    gd = inner // GM_GROUPS
    tri = _tril(CHUNK)
    x = x_ref[...]
    xn = _rms(x, nw_ref[...]).astype(BF16)
    v = jnp.dot(xn, win_ref[:, inner:2 * inner], preferred_element_type=F32)
    vn = _layernorm(v, lw_ref[...], lb_ref[...]).astype(BF16)
    wmask = [jnp.where(tri, ws_ref[g], 0.0).astype(BF16) for g in range(GM_GROUPS)]
    parts = []
    for g in range(GM_GROUPS):
        cols = slice(g * gd, (g + 1) * gd)
        u = jnp.dot(xn, win_ref[:, cols], preferred_element_type=F32)
        z = jnp.dot(xn, win_ref[:, 2 * inner + g * gd:2 * inner + (g + 1) * gd],
                    preferred_element_type=F32)
        mixed = jnp.concatenate(
            [jnp.dot(wmask[g], vn[ck * CHUNK:(ck + 1) * CHUNK, cols], preferred_element_type=F32)
             for ck in range(nck)], axis=0) + jnp.concatenate([sbT_ref[:, g:g + 1]] * nck, axis=0)
        parts.append((u * mixed * _silu(z)).astype(BF16))
    o_ref[...] = x + jnp.dot(jnp.concatenate(parts, axis=1), w_ref[...], preferred_element_type=F32)


def _gmlp_prompt(x2d, norm_w, w_in_bf16, inner, v_ln_w, v_ln_b, spatial_w, spatial_b, w_out_bf16):
    m, d = x2d.shape
    nck = 2 if (m // CHUNK) % 2 == 0 else 1
    tm = nck * CHUNK
    sb_t = jnp.pad(spatial_b.T, ((0, 0), (0, LANES - GM_GROUPS)))
    kern = functools.partial(_gmlp_kernel, nck=nck)
    c2 = lambda shape: pl.BlockSpec(shape, lambda i: (0, 0))
    return pl.pallas_call(
        kern,
        name="gmlp_prompt",
        grid=(m // tm,),
        in_specs=[pl.BlockSpec((tm, d), lambda i: (i, 0)),
                  c2((1, d)), c2((d, 3 * inner)), c2((1, inner)), c2((1, inner)),
                  pl.BlockSpec((GM_GROUPS, CHUNK, CHUNK), lambda i: (0, 0, 0)),
                  c2((CHUNK, LANES)), c2((inner, d))],
        out_specs=pl.BlockSpec((tm, d), lambda i: (i, 0)),
        out_shape=jax.ShapeDtypeStruct((m, d), F32),
        compiler_params=_cparams("parallel"),
    )(x2d, norm_w.reshape(1, d), w_in_bf16, v_ln_w.reshape(1, inner), v_ln_b.reshape(1, inner),
      spatial_w, sb_t, w_out_bf16)


def _gmlp_dec_kernel(u_ref, v_ref, z_ref, x_ref, lw_ref, lb_ref, w00_ref, sb0_ref, w_ref,
                     o_ref, vn_ref):
    vn = _layernorm(v_ref[...], lw_ref[...], lb_ref[...])
    vn_ref[...] = vn
    mixed = w00_ref[...] * vn + sb0_ref[...]
    g = u_ref[...] * mixed * _silu(z_ref[...])
    o_ref[...] = x_ref[...] + _bdot(g, w_ref[...])


def _gmlp_decode(uvz, x2d, inner, v_ln_w, v_ln_b, spatial_w, spatial_b, w_out_bf16):
    nb, d = x2d.shape
    gd = inner // GM_GROUPS
    w00 = jnp.repeat(spatial_w[:, 0, 0], gd).reshape(1, inner)
    sb0 = jnp.repeat(spatial_b[:, 0], gd).reshape(1, inner)
    vec = pl.BlockSpec((1, inner), lambda i: (0, 0))
    return pl.pallas_call(
        _gmlp_dec_kernel,
        name="gmlp_dec",
        grid=(1,),
        in_specs=[pl.BlockSpec((nb, inner), lambda i: (0, 0)),
                  pl.BlockSpec((nb, inner), lambda i: (0, 1)),
                  pl.BlockSpec((nb, inner), lambda i: (0, 2)),
                  pl.BlockSpec((nb, d), lambda i: (0, 0)),
                  vec, vec, vec, vec,
                  pl.BlockSpec((inner, d), lambda i: (0, 0))],
        out_specs=[pl.BlockSpec((nb, d), lambda i: (0, 0)),
                   pl.BlockSpec((nb, inner), lambda i: (0, 0))],
        out_shape=[jax.ShapeDtypeStruct((nb, d), F32), jax.ShapeDtypeStruct((nb, inner), F32)],
        compiler_params=_cparams("arbitrary"),
    )(uvz, uvz, uvz, x2d, v_ln_w.reshape(1, inner), v_ln_b.reshape(1, inner), w00, sb0, w_out_bf16)


def _blockdiag_coefs(w):
    n = w.shape[0]
    rows = []
    for d in range(-(ML_BLOCK - 1), ML_BLOCK):
        cols = []
        for i in range(ML_BLOCK):
            j = i + d
            cols.append(w[:, j, i] if 0 <= j < ML_BLOCK else jnp.zeros((n,), w.dtype))
        rows.append(jnp.stack(cols, axis=1).reshape(n * ML_BLOCK))
    return jnp.stack(rows, axis=0)


def _blockdiag_apply(x, coef_refs):
    width = x.shape[1]
    outs = [None] * len(coef_refs)
    for di, d in enumerate(range(-(ML_BLOCK - 1), ML_BLOCK)):
        xs = x if d == 0 else pltpu.roll(x, (-d) % width, 1)
        for n, cref in enumerate(coef_refs):
            t = xs * cref[di:di + 1, :]
            outs[n] = t if outs[n] is None else outs[n] + t
    return outs


def _mlstm_scan_kernel(xm_ref, z_ref, x_ref, cw_ref, cb_ref, wqk_ref, wvo_ref, bo_ref,
                       wifq_ref, wifk_ref, wifv_ref, bif_ref, mhw_ref, skip_ref, wout_ref,
                       out_ref, cst_ref, nst_ref, mst_ref,
                       buf_ref, act_ref, q_ref, k_ref, v_ref, o_ref, *, heads, streams):
    c = pl.program_id(1)
    L = CHUNK
    inner = xm_ref.shape[2]
    hd = inner // heads
    scale = hd ** -0.5
    S = range(streams)
    srow = [slice(s * L, (s + 1) * L) for s in S]

    @pl.when(c == 0)
    def _():
        cst_ref[...] = jnp.zeros(cst_ref.shape, F32)
        nst_ref[...] = jnp.zeros(nst_ref.shape, F32)
        mst_ref[...] = jnp.zeros(mst_ref.shape, F32)

    _conv_tail(buf_ref, c == 0)
    for s in S:
        _conv_fill(buf_ref.at[s], xm_ref.at[s], 0)
    for s in S:
        _conv_silu(buf_ref.at[s], act_ref.at[s], cw_ref, cb_ref)

    tw = wqk_ref.shape[1]
    gates = bif_ref[...]
    for t in range(inner // tw):
        cols = slice(t * tw, (t + 1) * tw)
        xc_t = jnp.concatenate([_act_cols(act_ref.at[s], t * tw, tw) for s in S], axis=0)
        xm_t = jnp.concatenate([xm_ref[s, :, cols] for s in S], axis=0)
        qk = _bdot(xc_t, wqk_ref[t])
        vo = _bdot(xm_t, wvo_ref[t])
        q_ref[:, cols] = qk[:, 0:tw]
        k_ref[:, cols] = qk[:, tw:2 * tw]
        v_ref[:, cols] = vo[:, 0:tw]
        o_ref[:, cols] = _sigmoid(vo[:, tw:2 * tw] + bo_ref[:, cols])
        gates = (gates + _bdot(qk[:, 0:tw], wifq_ref[cols, :]) + _bdot(qk[:, tw:2 * tw], wifk_ref[cols, :])
                 + _bdot(vo[:, 0:tw], wifv_ref[cols, :]))
    lf = -_softplus(-gates)
    tri = _tril(L)
    trif = tri.astype(F32)
    hcols = [slice(h * hd, (h + 1) * hd) for h in range(heads)]
    pairs = [(s, h) for s in S for h in range(heads)]
    bt = [_hdot(trif, lf[srow[s], :]) for s in S]
    g_t = [gates[srow[s], :].T for s in S]
    b_t = [bt[s].T for s in S]
    qb = {p: q_ref[srow[p[0]], hcols[p[1]]].astype(BF16) for p in pairs}
    kf = {p: k_ref[srow[p[0]], hcols[p[1]]] * scale for p in pairs}
    kb = {p: kf[p].astype(BF16) for p in pairs}
    cmat = {p: cst_ref[p[0], 0, p[1]] for p in pairs}
    s_qk = {p: lax.dot_general(qb[p], kb[p], NT_DIMS, preferred_element_type=F32) for p in pairs}
    cq = {p: lax.dot_general(qb[p], cmat[p].astype(BF16), NT_DIMS, preferred_element_type=F32)
          for p in pairs}
    bcol, igcol, inter, mt, w, wi = {}, {}, {}, {}, {}, {}
    for p in pairs:
        s, h = p
        bcol[p] = bt[s][:, heads + h:heads + h + 1]
        igcol[p] = gates[srow[s], h:h + 1]
        brow = b_t[s][heads + h:heads + h + 1, :]
        igrow = g_t[s][h:h + 1, :]
        d = jnp.where(tri, bcol[p] - brow + igrow, -jnp.inf)
        inter[p] = bcol[p] + mst_ref[s, 0, :, h:h + 1]
        mt[p] = jnp.maximum(inter[p], jnp.max(d, axis=1, keepdims=True))
        w[p] = jnp.exp(d - mt[p]) * s_qk[p]
        wi[p] = jnp.exp(inter[p] - mt[p])
    num = {p: wi[p] * cq[p] + _bdot(w[p], v_ref[srow[p[0]], hcols[p[1]]]) for p in pairs}
    for h in range(heads):
        cols = hcols[h]
        gs = []
        for s in S:
            p = (s, h)
            nrow = nst_ref[s, 0, h:h + 1, :]
            nq = jnp.sum(q_ref[srow[s], cols] * nrow, axis=1, keepdims=True)
            den = wi[p] * nq + jnp.sum(w[p], axis=1, keepdims=True)
            den = jnp.maximum(jnp.abs(den), jnp.exp(-mt[p]))
            hh = o_ref[srow[s], cols] * (num[p] / den)
            mu = jnp.mean(hh, axis=1, keepdims=True)
            hc = hh - mu
            hn = hc * lax.rsqrt(jnp.mean(hc * hc, axis=1, keepdims=True) + EPS)
            hn = hn * mhw_ref[:, cols] + skip_ref[:, cols] * _act_cols(act_ref.at[s], h * hd, hd)
            gs.append((hn * _silu(z_ref[s, :, cols])).astype(BF16))
        part = jnp.dot(jnp.concatenate(gs, axis=0), wout_ref[cols, :], preferred_element_type=F32)
        for s in S:
            if h == 0:
                out_ref[s] = x_ref[s] + part[srow[s], :]
            else:
                out_ref[s] += part[srow[s], :]
    for p in pairs:
        s, h = p
        cols = hcols[h]
        m_new = mt[p][L - 1:L, :]
        we = jnp.exp(bcol[p][L - 1:L, :] - bcol[p] + igcol[p] - m_new)
        dp = jnp.exp(inter[p][L - 1:L, :] - m_new)
        cst_ref[s, 0, h] = dp * cmat[p] + jnp.dot((we * v_ref[srow[s], cols]).T.astype(BF16), kb[p],
                                                  preferred_element_type=F32)
        nst_ref[s, 0, h:h + 1, :] = dp * nst_ref[s, 0, h:h + 1, :] + jnp.sum(we * kf[p], axis=0, keepdims=True)
        mst_ref[s, 0, :, h:h + 1] = m_new


def _blockdiag_tiles(w, tile):
    nt = w.shape[0] * ML_BLOCK // tile
    rows = jnp.tile(w.reshape(nt, tile, ML_BLOCK), (1, 1, tile // ML_BLOCK))
    blk = jnp.arange(tile) // ML_BLOCK
    return jnp.where(blk[:, None] == blk[None, :], rows, 0.0)


def _mlstm_consts(prm, inner):
    (norm_w, in_proj, conv_w, conv_b, w_q, w_k, w_v, w_o, b_o, w_if, b_if, mh_norm_w, skip,
     out_proj) = prm
    heads = ML_HEADS
    padn = LANES - 2 * heads
    wif = jnp.pad(w_if, ((0, 0), (0, padn))).astype(BF16)
    bif = jnp.pad(b_if, (0, padn)).reshape(1, LANES)
    return dict(
        conv_w=conv_w, conv_b=conv_b.reshape(1, inner),
        cq=_blockdiag_coefs(w_q), ck=_blockdiag_coefs(w_k), cv=_blockdiag_coefs(w_v),
        co=_blockdiag_coefs(w_o), bo=b_o.reshape(1, inner),
        wqk=jnp.concatenate([_blockdiag_tiles(w_q, MXU_TILE), _blockdiag_tiles(w_k, MXU_TILE)],
                            axis=2).astype(BF16),
        wvo=jnp.concatenate([_blockdiag_tiles(w_v, MXU_TILE), _blockdiag_tiles(w_o, MXU_TILE)],
                            axis=2).astype(BF16),
        wifq=wif[0:inner], wifk=wif[inner:2 * inner], wifv=wif[2 * inner:3 * inner], bif=bif,
        mhw=mh_norm_w.reshape(1, inner), skip=skip.reshape(1, inner))


def _mlstm_scan(xmz, x2d, bsz, seq, inner, cst, w_out_bf16):
    nc = seq // CHUNK
    d = x2d.shape[1]
    heads = ML_HEADS
    hd = inner // heads
    ntile = inner // MXU_TILE
    streams = 2 if bsz % 2 == 0 else 1
    bh = bsz // streams
    row = lambda b, c: b * nc + c
    once = pl.Buffered(1)
    c2 = lambda shape: pl.BlockSpec(shape, lambda b, c: (0, 0), pipeline_mode=once)
    c3 = lambda shape: pl.BlockSpec(shape, lambda b, c: (0, 0, 0), pipeline_mode=once)
    kern = functools.partial(_mlstm_scan_kernel, heads=heads, streams=streams)
    xmz3 = xmz.reshape(streams, bh * seq, xmz.shape[1])
    x3 = x2d.reshape(streams, bh * seq, d)
    out, c_p, n_p, m_p = pl.pallas_call(
        kern,
        name="mlstm_scan",
        grid=(bh, nc),
        in_specs=[pl.BlockSpec((streams, CHUNK, inner), lambda b, c: (0, row(b, c), 0)),
                  pl.BlockSpec((streams, CHUNK, inner), lambda b, c: (0, row(b, c), 1)),
                  pl.BlockSpec((streams, CHUNK, d), lambda b, c: (0, row(b, c), 0)),
                  c2((CONV_K, inner)), c2((1, inner)),
                  c3((ntile, MXU_TILE, 2 * MXU_TILE)), c3((ntile, MXU_TILE, 2 * MXU_TILE)),
                  c2((1, inner)),
                  c2((inner, LANES)), c2((inner, LANES)), c2((inner, LANES)), c2((1, LANES)),
                  c2((1, inner)), c2((1, inner)), c2((inner, d))],
        out_specs=[pl.BlockSpec((streams, CHUNK, d), lambda b, c: (0, row(b, c), 0)),
                   pl.BlockSpec((streams, 1, heads, hd, hd), lambda b, c: (0, b, 0, 0, 0)),
                   pl.BlockSpec((streams, 1, heads, hd), lambda b, c: (0, b, 0, 0)),
                   pl.BlockSpec((streams, 1, 1, heads), lambda b, c: (0, b, 0, 0))],
        out_shape=[jax.ShapeDtypeStruct((streams, bh * seq, d), F32),
                   jax.ShapeDtypeStruct((streams, bh, heads, hd, hd), F32),
                   jax.ShapeDtypeStruct((streams, bh, heads, hd), F32),
                   jax.ShapeDtypeStruct((streams, bh, 1, heads), F32)],
        scratch_shapes=[pltpu.VMEM((streams, inner // LANES, CONV_ROWS, LANES), F32)] * 2
                       + [pltpu.VMEM((streams * CHUNK, inner), F32)] * 4,
        compiler_params=_cparams("parallel", "arbitrary"),
    )(xmz3, xmz3, x3, cst["conv_w"], cst["conv_b"], cst["wqk"], cst["wvo"], cst["bo"],
      cst["wifq"], cst["wifk"], cst["wifv"], cst["bif"], cst["mhw"], cst["skip"], w_out_bf16)
    return (out.reshape(bsz * seq, d), c_p.reshape(bsz, heads, hd, hd), n_p.reshape(bsz, heads, hd),
            m_p.reshape(bsz, 1, heads))


def _mlstm_dec_pre_kernel(xm_ref, c0_ref, n0_ref, m0_ref, cw_ref, cb_ref, cq_ref, ck_ref, cv_ref,
                          co_ref, bo_ref, wifq_ref, wifk_ref, wifv_ref, bif_ref,
                          cnew_ref, q_ref, k_ref, v_ref, og_ref, xc_ref, wev_ref, dprep_ref,
                          wrep_ref, denrep_ref, nnew_ref, mnew_ref, *, heads):
    inner = xm_ref.shape[1]
    nb = xm_ref.shape[0]
    hd = inner // heads
    xm = xm_ref[...]
    acc = cb_ref[...] + cw_ref[CONV_K - 1:CONV_K, :] * xm
    for kk in range(CONV_K - 1):
        acc = acc + cw_ref[kk:kk + 1, :] * c0_ref[:, kk * inner:(kk + 1) * inner]
    for kk in range(CONV_K - 2):
        cnew_ref[:, kk * inner:(kk + 1) * inner] = c0_ref[:, (kk + 1) * inner:(kk + 2) * inner]
    cnew_ref[:, (CONV_K - 2) * inner:(CONV_K - 1) * inner] = xm
    xc = _silu(acc)
    q, k = _blockdiag_apply(xc, [cq_ref, ck_ref])
    v, o_pre = _blockdiag_apply(xm, [cv_ref, co_ref])
    gates = _bdot(q, wifq_ref[...]) + _bdot(k, wifk_ref[...]) + _bdot(v, wifv_ref[...]) + bif_ref[...]
    ksc = k * (hd ** -0.5)
    lane = lax.broadcasted_iota(jnp.int32, (nb, LANES), 1)
    ig = gates
    lf = pltpu.roll(-_softplus(-gates), LANES - heads, 1)
    n0 = n0_ref[...]
    qk = jnp.zeros((nb, LANES), F32)
    nq = jnp.zeros((nb, LANES), F32)
    for h in range(heads):
        cols = slice(h * hd, (h + 1) * hd)
        qk = jnp.where(lane == h, jnp.sum(q[:, cols] * ksc[:, cols], axis=1, keepdims=True), qk)
        nq = jnp.where(lane == h, jnp.sum(q[:, cols] * n0[:, cols], axis=1, keepdims=True), nq)
    inter = lf + m0_ref[...]
    mt = jnp.maximum(inter, ig)
    wi = jnp.exp(inter - mt)
    we = jnp.exp(ig - mt)
    w = we * qk
    den = jnp.maximum(jnp.abs(wi * nq + w), jnp.exp(-mt))
    mnew_ref[...] = mt
    sel = _expand_sel(LANES, hd, inner)
    wi_rep = _hdot(wi, sel)
    we_rep = _hdot(we, sel)
    dprep_ref[...] = wi_rep
    wrep_ref[...] = _hdot(w, sel)
    denrep_ref[...] = _hdot(den, sel)
    wev_ref[...] = we_rep * v
    nnew_ref[...] = wi_rep * n0 + we_rep * ksc
    q_ref[...] = q
    k_ref[...] = ksc
    v_ref[...] = v
    og_ref[...] = _sigmoid(o_pre + bo_ref[...])
    xc_ref[...] = xc


def _mlstm_dec_state_kernel(c_ref, q_ref, k_ref, dp_ref, wevT_ref, cn_ref, cqT_ref, *, heads, bb):
    i = pl.program_id(0)
    hd = c_ref.shape[2]
    nb = cqT_ref.shape[1]
    lane = lax.broadcasted_iota(jnp.int32, (hd, nb), 1)
    for bi in range(bb):
        msk = lane == i * bb + bi
        for h in range(heads):
            rows = slice(h * hd, (h + 1) * hd)
            cmat = c_ref[bi, h]
            cqcol = jnp.sum(cmat * q_ref[bi, h:h + 1, :], axis=1, keepdims=True)
            wev = jnp.sum(jnp.where(msk, wevT_ref[rows, :], 0.0), axis=1, keepdims=True)
            cn_ref[bi, h] = dp_ref[bi, h:h + 1, :] * cmat + wev * k_ref[bi, h:h + 1, :]
            cqT_ref[rows, :] = jnp.where(msk, cqcol, cqT_ref[rows, :])


def _mlstm_dec_post_kernel(cq_ref, v_ref, dp_ref, w_ref, den_ref, og_ref, xc_ref, z_ref,
                           mhw_ref, skip_ref, g_ref, *, heads):
    inner = cq_ref.shape[1]
    hd = inner // heads
    for h in range(heads):
        cols = slice(h * hd, (h + 1) * hd)
        num = dp_ref[:, cols] * cq_ref[:, cols] + w_ref[:, cols] * v_ref[:, cols]
        hh = og_ref[:, cols] * (num / den_ref[:, cols])
        mu = jnp.mean(hh, axis=1, keepdims=True)
        hc = hh - mu
        hn = hc * lax.rsqrt(jnp.mean(hc * hc, axis=1, keepdims=True) + EPS)
        hn = hn * mhw_ref[:, cols] + skip_ref[:, cols] * xc_ref[:, cols]
        g_ref[:, cols] = hn * _silu(z_ref[:, cols])


def _mlstm_decode(xmz, c0, n0, m0, conv0, inner, cst):
    nb = xmz.shape[0]
    heads = ML_HEADS
    hd = inner // heads
    nco = 2 * ML_BLOCK - 1
    full = lambda shape: pl.BlockSpec(shape, lambda i: tuple(0 for _ in shape))
    m0p = jnp.pad(m0, ((0, 0), (0, LANES - heads)))
    kern = functools.partial(_mlstm_dec_pre_kernel, heads=heads)
    big = jax.ShapeDtypeStruct((nb, inner), F32)
    outs = pl.pallas_call(
        kern,
        name="mlstm_dec_pre",
        grid=(1,),
        in_specs=[pl.BlockSpec((nb, inner), lambda i: (0, 0)),
                  full((nb, (CONV_K - 1) * inner)), full((nb, inner)), full((nb, LANES)),
                  full((CONV_K, inner)), full((1, inner)),
                  full((nco, inner)), full((nco, inner)), full((nco, inner)), full((nco, inner)),
                  full((1, inner)),
                  full((inner, LANES)), full((inner, LANES)), full((inner, LANES)), full((1, LANES))],
        out_specs=[full((nb, (CONV_K - 1) * inner))] + [full((nb, inner))] * 10 + [full((nb, LANES))],
        out_shape=[jax.ShapeDtypeStruct((nb, (CONV_K - 1) * inner), F32)] + [big] * 10
                  + [jax.ShapeDtypeStruct((nb, LANES), F32)],
        compiler_params=_cparams("arbitrary"),
    )(xmz, conv0.reshape(nb, (CONV_K - 1) * inner), n0.reshape(nb, inner), m0p,
      cst["conv_w"], cst["conv_b"], cst["cq"], cst["ck"], cst["cv"], cst["co"], cst["bo"],
      cst["wifq"], cst["wifk"], cst["wifv"], cst["bif"])
    cnew, q, ksc, v, og, xc, wev, dprep, wrep, denrep, nnew, mnew = outs

    bb = 2 if nb % 2 == 0 else 1
    kern = functools.partial(_mlstm_dec_state_kernel, heads=heads, bb=bb)
    h3 = lambda: pl.BlockSpec((bb, heads, hd), lambda i: (i, 0, 0))
    c_new, cq_t = pl.pallas_call(
        kern,
        name="mlstm_dec_state",
        grid=(nb // bb,),
        in_specs=[pl.BlockSpec((bb, heads, hd, hd), lambda i: (i, 0, 0, 0)),
                  h3(), h3(), h3(),
                  pl.BlockSpec((inner, nb), lambda i: (0, 0))],
        out_specs=[pl.BlockSpec((bb, heads, hd, hd), lambda i: (i, 0, 0, 0)),
                   pl.BlockSpec((inner, nb), lambda i: (0, 0))],
        out_shape=[jax.ShapeDtypeStruct((nb, heads, hd, hd), F32),
                   jax.ShapeDtypeStruct((inner, nb), F32)],
        compiler_params=_cparams("arbitrary"),
    )(c0, q.reshape(nb, heads, hd), ksc.reshape(nb, heads, hd), dprep.reshape(nb, heads, hd), wev.T)

    kern = functools.partial(_mlstm_dec_post_kernel, heads=heads)
    g = pl.pallas_call(
        kern,
        name="mlstm_dec_post",
        grid=(1,),
        in_specs=[full((nb, inner))] * 7
                 + [pl.BlockSpec((nb, inner), lambda i: (0, 1)), full((1, inner)), full((1, inner))],
        out_specs=full((nb, inner)),
        out_shape=big,
        compiler_params=_cparams("arbitrary"),
    )(cq_t.T, v, dprep, wrep, denrep, og, xc, xmz, cst["mhw"], cst["skip"])
    return (g, c_new, nnew.reshape(nb, heads, hd), mnew[:, 0:heads],
            cnew.reshape(nb, CONV_K - 1, inner))


def _ssd_in_weights(in_proj, inner, conv_dim):
    ncol = inner + conv_dim
    pad = LANES - (in_proj.shape[1] - ncol)
    return jnp.pad(in_proj.astype(BF16), ((0, 0), (0, pad)))


def _ssd_layer(xp, xs, ssm0, conv0, prm, final_w=None):
    norm_w, in_proj, conv_w, conv_b, dt_bias, a_log, d_skip, gnorm_w, out_proj = prm
    bsz, seq, d = xp.shape
    nb = xs.shape[0]
    inner = out_proj.shape[0]
    heads = a_log.shape[0]
    conv_dim = conv_w.shape[1]
    w_in = _ssd_in_weights(in_proj, inner, conv_dim)
    w_out = out_proj.astype(BF16)
    xp2 = xp.reshape(bsz * seq, d)
    xs2 = xs.reshape(nb, d)

    zx_p = _norm_matmul(xp2, norm_w, w_in)
    out_p, h_p = _ssd_scan(zx_p, xp2, bsz, seq, inner, heads, conv_w, conv_b, dt_bias, a_log, d_skip,
                           gnorm_w, w_out, final_w)
    conv_p = zx_p.reshape(bsz, seq, -1)[:, seq - (CONV_K - 1):, inner:inner + conv_dim]

    zx_s = _norm_matmul(xs2, norm_w, w_in)
    out_s, h_s, conv_s = _ssd_decode(zx_s, xs2, ssm0, conv0, inner, heads, prm, w_out, final_w=final_w)
    return (out_p.reshape(bsz, seq, d), out_s.reshape(nb, 1, d),
            h_p.reshape(bsz, heads, SSD_HEAD_DIM, SSD_STATE), conv_p, h_s, conv_s)


def _gmlp_layer(xp, xs, prm):
    norm_w, in_proj, v_ln_w, v_ln_b, spatial_w, spatial_b, out_proj = prm
    bsz, seq, d = xp.shape
    nb = xs.shape[0]
    inner = out_proj.shape[0]
    w_in = in_proj.astype(BF16)
    w_out = out_proj.astype(BF16)
    xp2 = xp.reshape(bsz * seq, d)
    xs2 = xs.reshape(nb, d)
    out_p = _gmlp_prompt(xp2, norm_w, w_in, inner, v_ln_w, v_ln_b, spatial_w, spatial_b, w_out)
    uvz_s = _norm_matmul(xs2, norm_w, w_in)
    out_s, vn_s = _gmlp_decode(uvz_s, xs2, inner, v_ln_w, v_ln_b, spatial_w, spatial_b, w_out)
    return out_p.reshape(bsz, seq, d), out_s.reshape(nb, 1, d), vn_s.reshape(nb, 1, inner)


def _mlstm_layer(xp, xs, c0, n0, m0, conv0, prm):
    norm_w, in_proj = prm[0], prm[1]
    out_proj = prm[-1]
    bsz, seq, d = xp.shape
    nb = xs.shape[0]
    inner = out_proj.shape[0]
    w_in = in_proj.astype(BF16)
    w_out = out_proj.astype(BF16)
    cst = _mlstm_consts(prm, inner)
    xp2 = xp.reshape(bsz * seq, d)
    xs2 = xs.reshape(nb, d)

    xmz_p = _norm_matmul(xp2, norm_w, w_in)
    out_p, c_p, n_p, m_p = _mlstm_scan(xmz_p, xp2, bsz, seq, inner, cst, w_out)
    conv_p = xmz_p.reshape(bsz, seq, -1)[:, seq - (CONV_K - 1):, 0:inner]

    xmz_s = _norm_matmul(xs2, norm_w, w_in)
    g_s, c_s, n_s, m_s, conv_s = _mlstm_decode(xmz_s, c0, n0, m0, conv0, inner, cst)
    out_s = _matmul_res(g_s, w_out, xs2)
    return (out_p.reshape(bsz, seq, d), out_s.reshape(nb, 1, d),
            c_p, n_p, m_p.reshape(bsz, ML_HEADS), conv_p, c_s, n_s, m_s, conv_s)


def kernel(x_prompt, x_sample, state_l0_ssm, state_l0_conv, state_l2_C, state_l2_n, state_l2_m, state_l2_conv, state_l3_ssm, state_l3_conv, l0_norm_w, l0_in_proj, l0_conv_w, l0_conv_b, l0_dt_bias, l0_A_log, l0_D_skip, l0_gnorm_w, l0_out_proj, l1_norm_w, l1_in_proj, l1_v_ln_w, l1_v_ln_b, l1_spatial_w, l1_spatial_b, l1_out_proj, l2_norm_w, l2_in_proj, l2_conv_w, l2_conv_b, l2_w_q, l2_w_k, l2_w_v, l2_w_o, l2_b_o, l2_w_if, l2_b_if, l2_mh_norm_w, l2_skip, l2_out_proj, l3_norm_w, l3_in_proj, l3_conv_w, l3_conv_b, l3_dt_bias, l3_A_log, l3_D_skip, l3_gnorm_w, l3_out_proj, final_norm_w):
    p0 = (l0_norm_w, l0_in_proj, l0_conv_w, l0_conv_b, l0_dt_bias, l0_A_log, l0_D_skip, l0_gnorm_w, l0_out_proj)
    p1 = (l1_norm_w, l1_in_proj, l1_v_ln_w, l1_v_ln_b, l1_spatial_w, l1_spatial_b, l1_out_proj)
    p2 = (l2_norm_w, l2_in_proj, l2_conv_w, l2_conv_b, l2_w_q, l2_w_k, l2_w_v, l2_w_o, l2_b_o,
          l2_w_if, l2_b_if, l2_mh_norm_w, l2_skip, l2_out_proj)
    p3 = (l3_norm_w, l3_in_proj, l3_conv_w, l3_conv_b, l3_dt_bias, l3_A_log, l3_D_skip, l3_gnorm_w, l3_out_proj)

    hp, hs, p0_ssm, p0_conv, s0_ssm, s0_conv = _ssd_layer(x_prompt, x_sample, state_l0_ssm, state_l0_conv, p0)
    hp, hs, s1_v = _gmlp_layer(hp, hs, p1)
    hp, hs, p2_C, p2_n, p2_m, p2_conv, s2_C, s2_n, s2_m, s2_conv = _mlstm_layer(
        hp, hs, state_l2_C, state_l2_n, state_l2_m, state_l2_conv, p2)
    y_prompt, y_sample, p3_ssm, p3_conv, s3_ssm, s3_conv = _ssd_layer(
        hp, hs, state_l3_ssm, state_l3_conv, p3, final_w=final_norm_w)
    return (y_prompt, y_sample,
            p0_ssm, p0_conv, s0_ssm, s0_conv,
            s1_v,
            p2_C, p2_n, p2_m, p2_conv, s2_C, s2_n, s2_m, s2_conv,
            p3_ssm, p3_conv, s3_ssm, s3_conv)
```

```python
import functools

import jax
import jax.numpy as jnp
from jax import lax
from jax.experimental import pallas as pl
from jax.experimental.pallas import tpu as pltpu

F32 = jnp.float32
BF16 = jnp.bfloat16
EPS = 1e-6
CONV_K = 4
CHUNK = 128
LANES = 128
MXU_TILE = 256
SUBLANES = 8
SSD_HEAD_DIM = 64
SSD_STATE = 128
SSD_GROUPS = 8
OUT_GROUPS = 2
ML_HEADS = 4
ML_BLOCK = 4
GM_GROUPS = 8
VMEM_LIMIT = 56 * 1024 * 1024
HI = lax.Precision.HIGHEST
LOG2E = 1.4426950408889634
NT_DIMS = (((1,), (1,)), ((), ()))


def _cparams(*sem):
    return pltpu.CompilerParams(dimension_semantics=sem, vmem_limit_bytes=VMEM_LIMIT)


def _sigmoid(x):
    return 1.0 / (1.0 + jnp.exp2(x * (-LOG2E)))


def _silu(x):
    h = 0.5 * x
    return h + h * jnp.tanh(h)


def _softplus(x):
    return jnp.maximum(x, 0.0) + jnp.log(1.0 + jnp.exp(-jnp.abs(x)))


def _bdot(a, b):
    return jnp.dot(a.astype(BF16), b.astype(BF16), preferred_element_type=F32)


def _hdot(a, b):
    return jnp.dot(a, b, precision=HI, preferred_element_type=F32)


def _rms(x, w):
    return x * lax.rsqrt(jnp.mean(x * x, axis=-1, keepdims=True) + EPS) * w


def _tril(n):
    r = lax.broadcasted_iota(jnp.int32, (n, n), 0)
    c = lax.broadcasted_iota(jnp.int32, (n, n), 1)
    return r >= c


def _expand_sel(n_in, width, n_out):
    r = lax.broadcasted_iota(jnp.int32, (n_in, n_out), 0)
    c = lax.broadcasted_iota(jnp.int32, (n_in, n_out), 1)
    return (c // width == r).astype(F32)


def _split_bf16(x):
    hi = x.astype(BF16)
    return hi, (x - hi.astype(F32)).astype(BF16)


def _expand(pieces, sel_bf16):
    return (jnp.dot(pieces[0], sel_bf16, preferred_element_type=F32)
            + jnp.dot(pieces[1], sel_bf16, preferred_element_type=F32))


def _norm_matmul_kernel(x_ref, nw_ref, w_ref, o_ref):
    o_ref[...] = _bdot(_rms(x_ref[...], nw_ref[...]), w_ref[...])


def _norm_matmul(x2d, norm_w, w_bf16):
    m, k = x2d.shape
    n = w_bf16.shape[1]
    tm = min(m, 512)
    return pl.pallas_call(
        _norm_matmul_kernel,
        name="norm_matmul",
        grid=(m // tm,),
        in_specs=[pl.BlockSpec((tm, k), lambda i: (i, 0)),
                  pl.BlockSpec((1, k), lambda i: (0, 0)),
                  pl.BlockSpec((k, n), lambda i: (0, 0), pipeline_mode=pl.Buffered(1))],
        out_specs=pl.BlockSpec((tm, n), lambda i: (i, 0)),
        out_shape=jax.ShapeDtypeStruct((m, n), F32),
        compiler_params=_cparams("parallel"),
    )(x2d, norm_w.reshape(1, k), w_bf16)


def _matmul_res_kernel(g_ref, w_ref, x_ref, o_ref):
    o_ref[...] = x_ref[...] + _bdot(g_ref[...], w_ref[...])


def _matmul_res(g2d, w_bf16, x2d):
    m, k = g2d.shape
    n = w_bf16.shape[1]
    tm = min(m, 512)
    return pl.pallas_call(
        _matmul_res_kernel,
        name="matmul_res",
        grid=(m // tm,),
        in_specs=[pl.BlockSpec((tm, k), lambda i: (i, 0)),
                  pl.BlockSpec((k, n), lambda i: (0, 0)),
                  pl.BlockSpec((tm, n), lambda i: (i, 0))],
        out_specs=pl.BlockSpec((tm, n), lambda i: (i, 0)),
        out_shape=jax.ShapeDtypeStruct((m, n), F32),
        compiler_params=_cparams("parallel"),
    )(g2d, w_bf16, x2d)


CONV_ROWS = CHUNK + SUBLANES
CONV_VREGS = CONV_ROWS // SUBLANES
CUR = slice(SUBLANES, CONV_ROWS)


def _conv_tail(buf_ref, is_first):
    lead = (slice(None),) * (len(buf_ref.shape) - 2)

    @pl.when(is_first)
    def _():
        buf_ref[lead + (slice(0, SUBLANES), slice(None))] = jnp.zeros(
            buf_ref.shape[:-2] + (SUBLANES, LANES), F32)

    @pl.when(jnp.logical_not(is_first))
    def _():
        buf_ref[lead + (slice(0, SUBLANES), slice(None))] = buf_ref[lead + (slice(CHUNK, CONV_ROWS), slice(None))]


def _conv_fill(buf_ref, src_ref, col0):
    t0 = col0 // LANES
    for j in range(src_ref.shape[1] // LANES):
        buf_ref[t0 + j, CUR, :] = src_ref[:, j * LANES:(j + 1) * LANES]


def _conv_silu(buf_ref, act_ref, cw_ref, cb_ref):
    nv = CONV_VREGS
    for j in range(buf_ref.shape[0]):
        cols = slice(j * LANES, (j + 1) * LANES)
        v = [buf_ref[j, pl.ds(a, SUBLANES, stride=nv), :] for a in range(nv)]
        wrap = {a: pltpu.roll(v[a], 1, 0) for a in range(nv - (CONV_K - 1), nv)}
        coef = [jnp.broadcast_to(cw_ref[k:k + 1, cols], (SUBLANES, LANES)) for k in range(CONV_K)]
        bias = jnp.broadcast_to(cb_ref[:, cols], (SUBLANES, LANES))
        for a in range(nv):
            acc = bias + coef[CONV_K - 1] * v[a]
            for back in range(1, CONV_K):
                tap = v[a - back] if a >= back else wrap[a - back + nv]
                acc = acc + coef[CONV_K - 1 - back] * tap
            act_ref[j, pl.ds(a, SUBLANES, stride=nv), :] = _silu(acc)


def _act_cols(act_ref, col0, width):
    t0 = col0 // LANES
    tiles = [act_ref[t0 + j, CUR, :] for j in range(width // LANES)]
    return tiles[0] if len(tiles) == 1 else jnp.concatenate(tiles, axis=1)


def _ssd_scan_kernel(*refs, inner, heads, has_final, streams):
    (xs_ref, bc_ref, dt_ref, z_ref, x_ref, cw_ref, cb_ref, dtb_ref, alog_ref, drep_ref, sel_ref,
     gnw_ref, wout_ref) = refs[:13]
    fin_ref = refs[13] if has_final else None
    o_ref, h_ref, buf_ref, act_ref = refs[13 + has_final:]
    c = pl.program_id(1)
    L = CHUNK
    gn = SSD_GROUPS * SSD_STATE
    rep = heads // SSD_GROUPS
    gw = rep * SSD_HEAD_DIM
    S = range(streams)

    @pl.when(c == 0)
    def _():
        h_ref[...] = jnp.zeros(h_ref.shape, F32)

    _conv_tail(buf_ref, c == 0)
    for s in S:
        _conv_fill(buf_ref.at[s], xs_ref.at[s], 0)
        _conv_fill(buf_ref.at[s], bc_ref.at[s], inner)
    for s in S:
        _conv_silu(buf_ref.at[s], act_ref.at[s], cw_ref, cb_ref)

    lane = lax.broadcasted_iota(jnp.int32, (1, LANES), 1)
    a_neg = jnp.where(lane < heads, -jnp.exp(alog_ref[...]), 0.0)
    tri = _tril(L)
    cum2, cum2_t, dt_t, dend, e_hi, e_lo, w_hi, w_lo = [], [], [], [], [], [], [], []
    for s in S:
        dt = _softplus(dt_ref[s] + dtb_ref[...])
        cm = _hdot(tri.astype(F32), dt * a_neg)
        cum2.append(cm * LOG2E)
        cum2_t.append(cum2[s].T)
        dt_t.append(dt.T)
        cum_end = cm[L - 1:L, :]
        dend.append(jnp.exp(cum_end))
        hi, lo = _split_bf16(jnp.exp(cm))
        e_hi.append(hi)
        e_lo.append(lo)
        hi, lo = _split_bf16(jnp.exp(cum_end - cm) * dt)
        w_hi.append(hi)
        w_lo.append(lo)
    e_pieces = (jnp.concatenate(e_hi, axis=0), jnp.concatenate(e_lo, axis=0))
    w_pieces = (jnp.concatenate(w_hi, axis=0), jnp.concatenate(w_lo, axis=0))

    lane_g = lax.broadcasted_iota(jnp.int32, (L, gw), 1) // SSD_HEAD_DIM
    pend = []
    for g in range(SSD_GROUPS):
        rows = slice(g * gw, (g + 1) * gw)
        e_all = _expand(e_pieces, sel_ref[:, rows])
        w_all = _expand(w_pieces, sel_ref[:, rows])
        yn = []
        for s in S:
            srows = slice(s * L, (s + 1) * L)
            aref = act_ref.at[s]
            bg = _act_cols(aref, inner + g * SSD_STATE, SSD_STATE).astype(BF16)
            cg = _act_cols(aref, inner + gn + g * SSD_STATE, SSD_STATE).astype(BF16)
            cb = lax.dot_general(cg, bg, NT_DIMS, preferred_element_type=F32)
            xg = _act_cols(aref, g * gw, gw)
            mixes = []
            for r in range(rep):
                h = g * rep + r
                seg = cum2[s][:, h:h + 1] - cum2_t[s][h:h + 1, :]
                decay = jnp.exp2(jnp.where(tri, seg, -jnp.inf))
                mixes.append((cb * decay * dt_t[s][h:h + 1, :]).astype(BF16))
            mixcat = jnp.concatenate(mixes, axis=1)
            xgb = xg.astype(BF16)
            xblk = jnp.concatenate([jnp.where(lane_g == r, xgb, jnp.zeros_like(xgb)) for r in range(rep)],
                                   axis=0)
            y = jnp.dot(mixcat, xblk, preferred_element_type=F32)
            hg = h_ref[s, 0, rows, :]
            yi = lax.dot_general(cg, hg.astype(BF16), NT_DIMS, preferred_element_type=F32)
            y = y + yi * e_all[srows, :] + drep_ref[:, rows] * xg
            upd = jnp.dot((xg * w_all[srows, :]).T.astype(BF16), bg, preferred_element_type=F32)
            for r in range(rep):
                h = g * rep + r
                hr = slice(r * SSD_HEAD_DIM, (r + 1) * SSD_HEAD_DIM)
                h_ref[s, 0, g * gw + r * SSD_HEAD_DIM:g * gw + (r + 1) * SSD_HEAD_DIM, :] = (
                    dend[s][:, h:h + 1] * hg[hr, :] + upd[hr, :])
            y = y * _silu(z_ref[s, :, rows])
            y = y * lax.rsqrt(jnp.mean(y * y, axis=-1, keepdims=True) + EPS) * gnw_ref[:, rows]
            yn.append(y.astype(BF16))
        pend.append(jnp.concatenate(yn, axis=0))
        if len(pend) == OUT_GROUPS:
            g0 = g + 1 - OUT_GROUPS
            part = jnp.dot(jnp.concatenate(pend, axis=1), wout_ref[g0 * gw:(g + 1) * gw, :],
                           preferred_element_type=F32)
            pend = []
            for s in S:
                srows = slice(s * L, (s + 1) * L)
                if g0 == 0:
                    o_ref[s] = x_ref[s] + part[srows, :]
                else:
                    o_ref[s] += part[srows, :]
    if has_final:
        for s in S:
            o_ref[s] = _rms(o_ref[s], fin_ref[...])


def _ssd_scan(zx, x2d, bsz, seq, inner, heads, conv_w, conv_b, dt_bias, a_log, d_skip, gnorm_w,
              w_out_bf16, final_w):
    nc = seq // CHUNK
    d = x2d.shape[1]
    gn = SSD_GROUPS * SSD_STATE
    conv_dim = inner + 2 * gn
    assert inner // SSD_GROUPS == (heads // SSD_GROUPS) * SSD_HEAD_DIM
    xs_blk = inner // inner
    bc_blk = (2 * inner) // (2 * gn)
    dt_blk = (inner + conv_dim) // LANES
    pad = LANES - heads
    dtb = jnp.pad(dt_bias, (0, pad)).reshape(1, LANES)
    alog = jnp.pad(a_log, (0, pad)).reshape(1, LANES)
    drep = jnp.repeat(d_skip, SSD_HEAD_DIM).reshape(1, inner)
    sel = (jnp.arange(inner)[None, :] // SSD_HEAD_DIM == jnp.arange(LANES)[:, None]).astype(BF16)
    has_final = final_w is not None
    streams = 2 if bsz % 2 == 0 else 1
    bh = bsz // streams
    kern = functools.partial(_ssd_scan_kernel, inner=inner, heads=heads, has_final=has_final,
                             streams=streams)
    row = lambda b, c: b * nc + c
    c2 = lambda shape: pl.BlockSpec(shape, lambda b, c: (0, 0))
    zx3 = zx.reshape(streams, bh * seq, zx.shape[1])
    x3 = x2d.reshape(streams, bh * seq, d)
    ins = [zx3, zx3, zx3, zx3, x3, conv_w, conv_b.reshape(1, conv_dim), dtb, alog, drep, sel,
           gnorm_w.reshape(1, inner), w_out_bf16]
    specs = [pl.BlockSpec((streams, CHUNK, inner), lambda b, c: (0, row(b, c), xs_blk)),
             pl.BlockSpec((streams, CHUNK, 2 * gn), lambda b, c: (0, row(b, c), bc_blk)),
             pl.BlockSpec((streams, CHUNK, LANES), lambda b, c: (0, row(b, c), dt_blk)),
             pl.BlockSpec((streams, CHUNK, inner), lambda b, c: (0, row(b, c), 0)),
             pl.BlockSpec((streams, CHUNK, d), lambda b, c: (0, row(b, c), 0)),
             c2((CONV_K, conv_dim)), c2((1, conv_dim)), c2((1, LANES)), c2((1, LANES)),
             c2((1, inner)), c2((LANES, inner)), c2((1, inner)), c2((inner, d))]
    if has_final:
        ins.append(final_w.reshape(1, d))
        specs.append(c2((1, d)))
    out, h = pl.pallas_call(
        kern,
        name="ssd_scan",
        grid=(bh, nc),
        in_specs=specs,
        out_specs=[pl.BlockSpec((streams, CHUNK, d), lambda b, c: (0, row(b, c), 0)),
                   pl.BlockSpec((streams, 1, inner, SSD_STATE), lambda b, c: (0, b, 0, 0))],
        out_shape=[jax.ShapeDtypeStruct((streams, bh * seq, d), F32),
                   jax.ShapeDtypeStruct((streams, bh, inner, SSD_STATE), F32)],
        scratch_shapes=[pltpu.VMEM((streams, conv_dim // LANES, CONV_ROWS, LANES), F32),
                        pltpu.VMEM((streams, conv_dim // LANES, CONV_ROWS, LANES), F32)],
        compiler_params=_cparams("parallel", "arbitrary"),
    )(*ins)
    return out.reshape(bsz * seq, d), h.reshape(bsz, inner, SSD_STATE)


def _ssd_post_kernel(*refs, has_add, has_final):
    y_ref, z_ref, x_ref, gw_ref, w_ref = refs[:5]
    pos = 5
    add_ref = fin_ref = None
    if has_add:
        add_ref = refs[pos]
        pos += 1
    if has_final:
        fin_ref = refs[pos]
        pos += 1
    o_ref = refs[pos]
    inner = y_ref.shape[1]
    gwid = inner // SSD_GROUPS
    parts = []
    for g in range(SSD_GROUPS):
        cols = slice(g * gwid, (g + 1) * gwid)
        y = y_ref[:, cols]
        if has_add:
            y = y + add_ref[:, cols]
        y = y * _silu(z_ref[:, cols])
        y = y * lax.rsqrt(jnp.mean(y * y, axis=-1, keepdims=True) + EPS)
        parts.append((y * gw_ref[:, cols]).astype(BF16))
    out = x_ref[...] + jnp.dot(jnp.concatenate(parts, axis=1), w_ref[...],
                               preferred_element_type=F32)
    if has_final:
        out = _rms(out, fin_ref[...])
    o_ref[...] = out


def _ssd_post(y, zx, x2d, gnorm_w, w_out_bf16, add=None, final_w=None):
    m, inner = y.shape
    d = x2d.shape[1]
    tm = min(m, 512)
    ins = [y, zx, x2d, gnorm_w.reshape(1, inner), w_out_bf16]
    specs = [pl.BlockSpec((tm, inner), lambda i: (i, 0)),
             pl.BlockSpec((tm, inner), lambda i: (i, 0)),
             pl.BlockSpec((tm, d), lambda i: (i, 0)),
             pl.BlockSpec((1, inner), lambda i: (0, 0)),
             pl.BlockSpec((inner, d), lambda i: (0, 0))]
    if add is not None:
        ins.append(add)
        specs.append(pl.BlockSpec((tm, inner), lambda i: (i, 0)))
    if final_w is not None:
        ins.append(final_w.reshape(1, d))
        specs.append(pl.BlockSpec((1, d), lambda i: (0, 0)))
    kern = functools.partial(_ssd_post_kernel, has_add=add is not None, has_final=final_w is not None)
    return pl.pallas_call(
        kern,
        name="ssd_post",
        grid=(m // tm,),
        in_specs=specs,
        out_specs=pl.BlockSpec((tm, d), lambda i: (i, 0)),
        out_shape=jax.ShapeDtypeStruct((m, d), F32),
        compiler_params=_cparams("parallel"),
    )(*ins)


def _ssd_dec_pre_kernel(xbc_ref, dt_ref, c0_ref, cw_ref, cb_ref, dtb_ref, alog_ref, drep_ref,
                        cnew_ref, dtx_ref, da_ref, b_ref, c_ref, dx_ref, *, inner, heads):
    gn = SSD_GROUPS * SSD_STATE
    cd = inner + 2 * gn
    xnew = xbc_ref[...]
    acc = cb_ref[...] + cw_ref[CONV_K - 1:CONV_K, :] * xnew
    for k in range(CONV_K - 1):
        acc = acc + cw_ref[k:k + 1, :] * c0_ref[:, k * cd:(k + 1) * cd]
    for k in range(CONV_K - 2):
        cnew_ref[:, k * cd:(k + 1) * cd] = c0_ref[:, (k + 1) * cd:(k + 2) * cd]
    cnew_ref[:, (CONV_K - 2) * cd:(CONV_K - 1) * cd] = xnew
    act = _silu(acc)
    xs = act[:, 0:inner]
    b_ref[...] = act[:, inner:inner + gn]
    c_ref[...] = act[:, inner + gn:inner + 2 * gn]
    lane = lax.broadcasted_iota(jnp.int32, (1, LANES), 1)
    dt = _softplus(dt_ref[...] + dtb_ref[...])
    a_neg = jnp.where(lane < heads, -jnp.exp(alog_ref[...]), 0.0)
    da_ref[...] = jnp.exp(dt * a_neg)
    dtx_ref[...] = _hdot(dt, _expand_sel(LANES, SSD_HEAD_DIM, inner)) * xs
    dx_ref[...] = drep_ref[...] * xs


def _ssd_dec_state_kernel(da_ref, h_ref, dtxT_ref, b_ref, c_ref, hn_ref, yT_ref, *, bb, rep):
    i = pl.program_id(0)
    gw = rep * SSD_HEAD_DIM
    nb = yT_ref.shape[1]
    lane = lax.broadcasted_iota(jnp.int32, (gw, nb), 1)
    brow = lax.broadcasted_iota(jnp.int32, (nb, SSD_STATE), 0)

    @pl.when(i == 0)
    def _():
        yT_ref[...] = jnp.zeros(yT_ref.shape, F32)

    def body(bi, carry):
        bglob = i * bb + bi
        msk = lane == bglob
        pick = brow == bglob
        bmat = b_ref[bi]
        cmat = c_ref[bi]
        hnews = []
        for g in range(SSD_GROUPS):
            rows = slice(g * gw, (g + 1) * gw)
            dtx = jnp.sum(jnp.where(msk, dtxT_ref[rows, :], 0.0), axis=1, keepdims=True)
            hdec = jnp.concatenate(
                [da_ref[bglob, g * rep + r] * h_ref[bi, g * gw + r * SSD_HEAD_DIM:g * gw + (r + 1) * SSD_HEAD_DIM, :]
                 for r in range(rep)], axis=0)
            hnew = hdec + dtx * bmat[g:g + 1, :]
            hn_ref[bi, rows, :] = hnew
            hnews.append(hnew.astype(BF16))
        for g in range(SSD_GROUPS):
            rows = slice(g * gw, (g + 1) * gw)
            crow = jnp.where(pick, jnp.broadcast_to(cmat[g:g + 1, :], (nb, SSD_STATE)), 0.0).astype(BF16)
            yT_ref[rows, :] += lax.dot_general(hnews[g], crow, NT_DIMS, preferred_element_type=F32)
        return carry

    lax.fori_loop(0, bb, body, 0)


def _ssd_decode(zx, x2d, ssm0, conv0, inner, heads, prm, w_out_bf16, final_w=None):
    norm_w, in_proj, conv_w, conv_b, dt_bias, a_log, d_skip, gnorm_w, out_proj = prm
    nb = zx.shape[0]
    gn = SSD_GROUPS * SSD_STATE
    cd = inner + 2 * gn
    pad = LANES - heads
    dtb = jnp.pad(dt_bias, (0, pad)).reshape(1, LANES)
    alog = jnp.pad(a_log, (0, pad)).reshape(1, LANES)
    drep = jnp.repeat(d_skip, SSD_HEAD_DIM).reshape(1, inner)
    full = lambda shape: pl.BlockSpec(shape, lambda i: tuple(0 for _ in shape))
    kern = functools.partial(_ssd_dec_pre_kernel, inner=inner, heads=heads)
    cnew, dtx, da, bact, cact, dx = pl.pallas_call(
        kern,
        name="ssd_dec_pre",
        grid=(1,),
        in_specs=[full((nb, cd)), full((nb, LANES)), full((nb, (CONV_K - 1) * cd)), full((CONV_K, cd)), full((1, cd)),
                  full((1, LANES)), full((1, LANES)), full((1, inner))],
        out_specs=[full((nb, (CONV_K - 1) * cd)), full((nb, inner)), full((nb, LANES)),
                   full((nb, gn)), full((nb, gn)), full((nb, inner))],
        out_shape=[jax.ShapeDtypeStruct((nb, (CONV_K - 1) * cd), F32),
                   jax.ShapeDtypeStruct((nb, inner), F32), jax.ShapeDtypeStruct((nb, LANES), F32),
                   jax.ShapeDtypeStruct((nb, gn), F32), jax.ShapeDtypeStruct((nb, gn), F32),
                   jax.ShapeDtypeStruct((nb, inner), F32)],
        compiler_params=_cparams("arbitrary"),
    )(zx[:, inner:inner + cd], zx[:, inner + cd:inner + cd + LANES],
      conv0.reshape(nb, (CONV_K - 1) * cd), conv_w, conv_b.reshape(1, cd), dtb, alog, drep)

    bb = 8 if nb % 8 == 0 else 1
    rep = heads // SSD_GROUPS
    kern = functools.partial(_ssd_dec_state_kernel, bb=bb, rep=rep)
    hnew, y_t = pl.pallas_call(
        kern,
        name="ssd_dec_state",
        grid=(nb // bb,),
        in_specs=[pl.BlockSpec(memory_space=pltpu.SMEM),
                  pl.BlockSpec((bb, inner, SSD_STATE), lambda i: (i, 0, 0)),
                  pl.BlockSpec((inner, nb), lambda i: (0, 0)),
                  pl.BlockSpec((bb, SSD_GROUPS, SSD_STATE), lambda i: (i, 0, 0)),
                  pl.BlockSpec((bb, SSD_GROUPS, SSD_STATE), lambda i: (i, 0, 0))],
        out_specs=[pl.BlockSpec((bb, inner, SSD_STATE), lambda i: (i, 0, 0)),
                   pl.BlockSpec((inner, nb), lambda i: (0, 0))],
        out_shape=[jax.ShapeDtypeStruct((nb, inner, SSD_STATE), F32),
                   jax.ShapeDtypeStruct((inner, nb), F32)],
        compiler_params=_cparams("arbitrary"),
    )(da[:, 0:heads], ssm0.reshape(nb, inner, SSD_STATE), dtx.T,
      bact.reshape(nb, SSD_GROUPS, SSD_STATE), cact.reshape(nb, SSD_GROUPS, SSD_STATE))

    out = _ssd_post(y_t.T, zx, x2d, gnorm_w, w_out_bf16, add=dx, final_w=final_w)
    return out, hnew.reshape(nb, heads, SSD_HEAD_DIM, SSD_STATE), cnew.reshape(nb, CONV_K - 1, cd)


def _layernorm(v, w, b):
    mu = jnp.mean(v, axis=-1, keepdims=True)
    vc = v - mu
    return vc * lax.rsqrt(jnp.mean(vc * vc, axis=-1, keepdims=True) + EPS) * w + b


def _gmlp_kernel(x_ref, nw_ref, win_ref, lw_ref, lb_ref, ws_ref, sbT_ref, w_ref, o_ref, *, nck):
    inner = w_ref.shape[0]
    gd = inner // GM_GROUPS
    tri = _tril(CHUNK)
    x = x_ref[...]
    xn = _rms(x, nw_ref[...]).astype(BF16)
    v = jnp.dot(xn, win_ref[:, inner:2 * inner], preferred_element_type=F32)
    vn = _layernorm(v, lw_ref[...], lb_ref[...]).astype(BF16)
    wmask = [jnp.where(tri, ws_ref[g], 0.0).astype(BF16) for g in range(GM_GROUPS)]
    parts = []
    for g in range(GM_GROUPS):
        cols = slice(g * gd, (g + 1) * gd)
        u = jnp.dot(xn, win_ref[:, cols], preferred_element_type=F32)
        z = jnp.dot(xn, win_ref[:, 2 * inner + g * gd:2 * inner + (g + 1) * gd],
                    preferred_element_type=F32)
        mixed = jnp.concatenate(
            [jnp.dot(wmask[g], vn[ck * CHUNK:(ck + 1) * CHUNK, cols], preferred_element_type=F32)
             for ck in range(nck)], axis=0) + jnp.concatenate([sbT_ref[:, g:g + 1]] * nck, axis=0)
        parts.append((u * mixed * _silu(z)).astype(BF16))
    o_ref[...] = x + jnp.dot(jnp.concatenate(parts, axis=1), w_ref[...], preferred_element_type=F32)


def _gmlp_prompt(x2d, norm_w, w_in_bf16, inner, v_ln_w, v_ln_b, spatial_w, spatial_b, w_out_bf16):
    m, d = x2d.shape
    nck = 4 if (m // CHUNK) % 4 == 0 else 1
    tm = nck * CHUNK
    sb_t = jnp.pad(spatial_b.T, ((0, 0), (0, LANES - GM_GROUPS)))
    kern = functools.partial(_gmlp_kernel, nck=nck)
    c2 = lambda shape: pl.BlockSpec(shape, lambda i: (0, 0), pipeline_mode=pl.Buffered(1))
    return pl.pallas_call(
        kern,
        name="gmlp_prompt",
        grid=(m // tm,),
        in_specs=[pl.BlockSpec((tm, d), lambda i: (i, 0)),
                  c2((1, d)), c2((d, 3 * inner)), c2((1, inner)), c2((1, inner)),
                  pl.BlockSpec((GM_GROUPS, CHUNK, CHUNK), lambda i: (0, 0, 0)),
                  c2((CHUNK, LANES)), c2((inner, d))],
        out_specs=pl.BlockSpec((tm, d), lambda i: (i, 0)),
        out_shape=jax.ShapeDtypeStruct((m, d), F32),
        compiler_params=_cparams("parallel"),
    )(x2d, norm_w.reshape(1, d), w_in_bf16, v_ln_w.reshape(1, inner), v_ln_b.reshape(1, inner),
      spatial_w, sb_t, w_out_bf16)


def _gmlp_dec_kernel(u_ref, v_ref, z_ref, x_ref, lw_ref, lb_ref, w00_ref, sb0_ref, w_ref,
                     o_ref, vn_ref):
    vn = _layernorm(v_ref[...], lw_ref[...], lb_ref[...])
    vn_ref[...] = vn
    mixed = w00_ref[...] * vn + sb0_ref[...]
    g = u_ref[...] * mixed * _silu(z_ref[...])
    o_ref[...] = x_ref[...] + _bdot(g, w_ref[...])


def _gmlp_decode(uvz, x2d, inner, v_ln_w, v_ln_b, spatial_w, spatial_b, w_out_bf16):
    nb, d = x2d.shape
    gd = inner // GM_GROUPS
    w00 = jnp.repeat(spatial_w[:, 0, 0], gd).reshape(1, inner)
    sb0 = jnp.repeat(spatial_b[:, 0], gd).reshape(1, inner)
    vec = pl.BlockSpec((1, inner), lambda i: (0, 0))
    return pl.pallas_call(
        _gmlp_dec_kernel,
        name="gmlp_dec",
        grid=(1,),
        in_specs=[pl.BlockSpec((nb, inner), lambda i: (0, 0)),
                  pl.BlockSpec((nb, inner), lambda i: (0, 1)),
                  pl.BlockSpec((nb, inner), lambda i: (0, 2)),
                  pl.BlockSpec((nb, d), lambda i: (0, 0)),
                  vec, vec, vec, vec,
                  pl.BlockSpec((inner, d), lambda i: (0, 0))],
        out_specs=[pl.BlockSpec((nb, d), lambda i: (0, 0)),
                   pl.BlockSpec((nb, inner), lambda i: (0, 0))],
        out_shape=[jax.ShapeDtypeStruct((nb, d), F32), jax.ShapeDtypeStruct((nb, inner), F32)],
        compiler_params=_cparams("arbitrary"),
    )(uvz, uvz, uvz, x2d, v_ln_w.reshape(1, inner), v_ln_b.reshape(1, inner), w00, sb0, w_out_bf16)


def _blockdiag_coefs(w):
    n = w.shape[0]
    rows = []
    for d in range(-(ML_BLOCK - 1), ML_BLOCK):
        cols = []
        for i in range(ML_BLOCK):
            j = i + d
            cols.append(w[:, j, i] if 0 <= j < ML_BLOCK else jnp.zeros((n,), w.dtype))
        rows.append(jnp.stack(cols, axis=1).reshape(n * ML_BLOCK))
    return jnp.stack(rows, axis=0)


def _blockdiag_apply(x, coef_refs):
    width = x.shape[1]
    outs = [None] * len(coef_refs)
    for di, d in enumerate(range(-(ML_BLOCK - 1), ML_BLOCK)):
        xs = x if d == 0 else pltpu.roll(x, (-d) % width, 1)
        for n, cref in enumerate(coef_refs):
            t = xs * cref[di:di + 1, :]
            outs[n] = t if outs[n] is None else outs[n] + t
    return outs


def _mlstm_scan_kernel(xm_ref, z_ref, x_ref, cw_ref, cb_ref, wqk_ref, wvo_ref, bo_ref,
                       wifq_ref, wifk_ref, wifv_ref, bif_ref, mhw_ref, skip_ref, wout_ref,
                       out_ref, cst_ref, nst_ref, mst_ref,
                       buf_ref, act_ref, q_ref, k_ref, v_ref, o_ref, *, heads, streams):
    c = pl.program_id(1)
    L = CHUNK
    inner = xm_ref.shape[2]
    hd = inner // heads
    scale = hd ** -0.5
    S = range(streams)
    srow = [slice(s * L, (s + 1) * L) for s in S]

    @pl.when(c == 0)
    def _():
        cst_ref[...] = jnp.zeros(cst_ref.shape, F32)
        nst_ref[...] = jnp.zeros(nst_ref.shape, F32)
        mst_ref[...] = jnp.zeros(mst_ref.shape, F32)

    _conv_tail(buf_ref, c == 0)
    for s in S:
        _conv_fill(buf_ref.at[s], xm_ref.at[s], 0)
    for s in S:
        _conv_silu(buf_ref.at[s], act_ref.at[s], cw_ref, cb_ref)

    tw = wqk_ref.shape[1]
    gates = bif_ref[...]
    for t in range(inner // tw):
        cols = slice(t * tw, (t + 1) * tw)
        xc_t = jnp.concatenate([_act_cols(act_ref.at[s], t * tw, tw) for s in S], axis=0)
        xm_t = jnp.concatenate([xm_ref[s, :, cols] for s in S], axis=0)
        qk = _bdot(xc_t, wqk_ref[t])
        vo = _bdot(xm_t, wvo_ref[t])
        q_ref[:, cols] = qk[:, 0:tw]
        k_ref[:, cols] = qk[:, tw:2 * tw]
        v_ref[:, cols] = vo[:, 0:tw]
        o_ref[:, cols] = _sigmoid(vo[:, tw:2 * tw] + bo_ref[:, cols])
        gates = (gates + _bdot(qk[:, 0:tw], wifq_ref[cols, :]) + _bdot(qk[:, tw:2 * tw], wifk_ref[cols, :])
                 + _bdot(vo[:, 0:tw], wifv_ref[cols, :]))
    lf = -_softplus(-gates)
    tri = _tril(L)
    trif = tri.astype(F32)
    hcols = [slice(h * hd, (h + 1) * hd) for h in range(heads)]
    pairs = [(s, h) for s in S for h in range(heads)]
    bt = [_hdot(trif, lf[srow[s], :]) for s in S]
    g_t = [gates[srow[s], :].T for s in S]
    b_t = [bt[s].T for s in S]
    qb = {p: q_ref[srow[p[0]], hcols[p[1]]].astype(BF16) for p in pairs}
    kf = {p: k_ref[srow[p[0]], hcols[p[1]]] * scale for p in pairs}
    kb = {p: kf[p].astype(BF16) for p in pairs}
    cmat = {p: cst_ref[p[0], 0, p[1]] for p in pairs}
    s_qk = {p: lax.dot_general(qb[p], kb[p], NT_DIMS, preferred_element_type=F32) for p in pairs}
    cq = {p: lax.dot_general(qb[p], cmat[p].astype(BF16), NT_DIMS, preferred_element_type=F32)
          for p in pairs}
    bcol, igcol, inter, mt, w, wi = {}, {}, {}, {}, {}, {}
    for p in pairs:
        s, h = p
        bcol[p] = bt[s][:, heads + h:heads + h + 1]
        igcol[p] = gates[srow[s], h:h + 1]
        brow = b_t[s][heads + h:heads + h + 1, :]
        igrow = g_t[s][h:h + 1, :]
        d = jnp.where(tri, bcol[p] - brow + igrow, -jnp.inf)
        inter[p] = bcol[p] + mst_ref[s, 0, :, h:h + 1]
        mt[p] = jnp.maximum(inter[p], jnp.max(d, axis=1, keepdims=True))
        w[p] = jnp.exp(d - mt[p]) * s_qk[p]
        wi[p] = jnp.exp(inter[p] - mt[p])
    num = {p: wi[p] * cq[p] + _bdot(w[p], v_ref[srow[p[0]], hcols[p[1]]]) for p in pairs}
    for h in range(heads):
        cols = hcols[h]
        gs = []
        for s in S:
            p = (s, h)
            nrow = nst_ref[s, 0, h:h + 1, :]
            nq = jnp.sum(q_ref[srow[s], cols] * nrow, axis=1, keepdims=True)
            den = wi[p] * nq + jnp.sum(w[p], axis=1, keepdims=True)
            den = jnp.maximum(jnp.abs(den), jnp.exp(-mt[p]))
            hh = o_ref[srow[s], cols] * (num[p] / den)
            mu = jnp.mean(hh, axis=1, keepdims=True)
            hc = hh - mu
            hn = hc * lax.rsqrt(jnp.mean(hc * hc, axis=1, keepdims=True) + EPS)
            hn = hn * mhw_ref[:, cols] + skip_ref[:, cols] * _act_cols(act_ref.at[s], h * hd, hd)
            gs.append((hn * _silu(z_ref[s, :, cols])).astype(BF16))
        part = jnp.dot(jnp.concatenate(gs, axis=0), wout_ref[cols, :], preferred_element_type=F32)
        for s in S:
            if h == 0:
                out_ref[s] = x_ref[s] + part[srow[s], :]
            else:
                out_ref[s] += part[srow[s], :]
    for p in pairs:
        s, h = p
        cols = hcols[h]
        m_new = mt[p][L - 1:L, :]
        we = jnp.exp(bcol[p][L - 1:L, :] - bcol[p] + igcol[p] - m_new)
        dp = jnp.exp(inter[p][L - 1:L, :] - m_new)
        cst_ref[s, 0, h] = dp * cmat[p] + jnp.dot((we * v_ref[srow[s], cols]).T.astype(BF16), kb[p],
                                                  preferred_element_type=F32)
        nst_ref[s, 0, h:h + 1, :] = dp * nst_ref[s, 0, h:h + 1, :] + jnp.sum(we * kf[p], axis=0, keepdims=True)
        mst_ref[s, 0, :, h:h + 1] = m_new


def _blockdiag_tiles(w, tile):
    nt = w.shape[0] * ML_BLOCK // tile
    rows = jnp.tile(w.reshape(nt, tile, ML_BLOCK), (1, 1, tile // ML_BLOCK))
    blk = jnp.arange(tile) // ML_BLOCK
    return jnp.where(blk[:, None] == blk[None, :], rows, 0.0)


def _mlstm_consts(prm, inner):
    (norm_w, in_proj, conv_w, conv_b, w_q, w_k, w_v, w_o, b_o, w_if, b_if, mh_norm_w, skip,
     out_proj) = prm
    heads = ML_HEADS
    padn = LANES - 2 * heads
    wif = jnp.pad(w_if, ((0, 0), (0, padn))).astype(BF16)
    bif = jnp.pad(b_if, (0, padn)).reshape(1, LANES)
    return dict(
        conv_w=conv_w, conv_b=conv_b.reshape(1, inner),
        cq=_blockdiag_coefs(w_q), ck=_blockdiag_coefs(w_k), cv=_blockdiag_coefs(w_v),
        co=_blockdiag_coefs(w_o), bo=b_o.reshape(1, inner),
        wqk=jnp.concatenate([_blockdiag_tiles(w_q, MXU_TILE), _blockdiag_tiles(w_k, MXU_TILE)],
                            axis=2).astype(BF16),
        wvo=jnp.concatenate([_blockdiag_tiles(w_v, MXU_TILE), _blockdiag_tiles(w_o, MXU_TILE)],
                            axis=2).astype(BF16),
        wifq=wif[0:inner], wifk=wif[inner:2 * inner], wifv=wif[2 * inner:3 * inner], bif=bif,
        mhw=mh_norm_w.reshape(1, inner), skip=skip.reshape(1, inner))


def _mlstm_scan(xmz, x2d, bsz, seq, inner, cst, w_out_bf16):
    nc = seq // CHUNK
    d = x2d.shape[1]
    heads = ML_HEADS
    hd = inner // heads
    ntile = inner // MXU_TILE
    streams = 2 if bsz % 2 == 0 else 1
    bh = bsz // streams
    row = lambda b, c: b * nc + c
    once = pl.Buffered(1)
    c2 = lambda shape: pl.BlockSpec(shape, lambda b, c: (0, 0), pipeline_mode=once)
    c3 = lambda shape: pl.BlockSpec(shape, lambda b, c: (0, 0, 0), pipeline_mode=once)
    kern = functools.partial(_mlstm_scan_kernel, heads=heads, streams=streams)
    xmz3 = xmz.reshape(streams, bh * seq, xmz.shape[1])
    x3 = x2d.reshape(streams, bh * seq, d)
    out, c_p, n_p, m_p = pl.pallas_call(
        kern,
        name="mlstm_scan",
        grid=(bh, nc),
        in_specs=[pl.BlockSpec((streams, CHUNK, inner), lambda b, c: (0, row(b, c), 0)),
                  pl.BlockSpec((streams, CHUNK, inner), lambda b, c: (0, row(b, c), 1)),
                  pl.BlockSpec((streams, CHUNK, d), lambda b, c: (0, row(b, c), 0)),
                  c2((CONV_K, inner)), c2((1, inner)),
                  c3((ntile, MXU_TILE, 2 * MXU_TILE)), c3((ntile, MXU_TILE, 2 * MXU_TILE)),
                  c2((1, inner)),
                  c2((inner, LANES)), c2((inner, LANES)), c2((inner, LANES)), c2((1, LANES)),
                  c2((1, inner)), c2((1, inner)), c2((inner, d))],
        out_specs=[pl.BlockSpec((streams, CHUNK, d), lambda b, c: (0, row(b, c), 0)),
                   pl.BlockSpec((streams, 1, heads, hd, hd), lambda b, c: (0, b, 0, 0, 0)),
                   pl.BlockSpec((streams, 1, heads, hd), lambda b, c: (0, b, 0, 0)),
                   pl.BlockSpec((streams, 1, 1, heads), lambda b, c: (0, b, 0, 0))],
        out_shape=[jax.ShapeDtypeStruct((streams, bh * seq, d), F32),
                   jax.ShapeDtypeStruct((streams, bh, heads, hd, hd), F32),
                   jax.ShapeDtypeStruct((streams, bh, heads, hd), F32),
                   jax.ShapeDtypeStruct((streams, bh, 1, heads), F32)],
        scratch_shapes=[pltpu.VMEM((streams, inner // LANES, CONV_ROWS, LANES), F32)] * 2
                       + [pltpu.VMEM((streams * CHUNK, inner), F32)] * 4,
        compiler_params=_cparams("parallel", "arbitrary"),
    )(xmz3, xmz3, x3, cst["conv_w"], cst["conv_b"], cst["wqk"], cst["wvo"], cst["bo"],
      cst["wifq"], cst["wifk"], cst["wifv"], cst["bif"], cst["mhw"], cst["skip"], w_out_bf16)
    return (out.reshape(bsz * seq, d), c_p.reshape(bsz, heads, hd, hd), n_p.reshape(bsz, heads, hd),
            m_p.reshape(bsz, 1, heads))


def _mlstm_dec_pre_kernel(xm_ref, c0_ref, n0_ref, m0_ref, cw_ref, cb_ref, cq_ref, ck_ref, cv_ref,
                          co_ref, bo_ref, wifq_ref, wifk_ref, wifv_ref, bif_ref,
                          cnew_ref, q_ref, k_ref, v_ref, og_ref, xc_ref, wev_ref, dprep_ref,
                          wrep_ref, denrep_ref, nnew_ref, mnew_ref, *, heads):
    inner = xm_ref.shape[1]
    nb = xm_ref.shape[0]
    hd = inner // heads
    xm = xm_ref[...]
    acc = cb_ref[...] + cw_ref[CONV_K - 1:CONV_K, :] * xm
    for kk in range(CONV_K - 1):
        acc = acc + cw_ref[kk:kk + 1, :] * c0_ref[:, kk * inner:(kk + 1) * inner]
    for kk in range(CONV_K - 2):
        cnew_ref[:, kk * inner:(kk + 1) * inner] = c0_ref[:, (kk + 1) * inner:(kk + 2) * inner]
    cnew_ref[:, (CONV_K - 2) * inner:(CONV_K - 1) * inner] = xm
    xc = _silu(acc)
    q, k = _blockdiag_apply(xc, [cq_ref, ck_ref])
    v, o_pre = _blockdiag_apply(xm, [cv_ref, co_ref])
    gates = _bdot(q, wifq_ref[...]) + _bdot(k, wifk_ref[...]) + _bdot(v, wifv_ref[...]) + bif_ref[...]
    ksc = k * (hd ** -0.5)
    lane = lax.broadcasted_iota(jnp.int32, (nb, LANES), 1)
    ig = gates
    lf = pltpu.roll(-_softplus(-gates), LANES - heads, 1)
    n0 = n0_ref[...]
    qk = jnp.zeros((nb, LANES), F32)
    nq = jnp.zeros((nb, LANES), F32)
    for h in range(heads):
        cols = slice(h * hd, (h + 1) * hd)
        qk = jnp.where(lane == h, jnp.sum(q[:, cols] * ksc[:, cols], axis=1, keepdims=True), qk)
        nq = jnp.where(lane == h, jnp.sum(q[:, cols] * n0[:, cols], axis=1, keepdims=True), nq)
    inter = lf + m0_ref[...]
    mt = jnp.maximum(inter, ig)
    wi = jnp.exp(inter - mt)
    we = jnp.exp(ig - mt)
    w = we * qk
    den = jnp.maximum(jnp.abs(wi * nq + w), jnp.exp(-mt))
    mnew_ref[...] = mt
    sel = _expand_sel(LANES, hd, inner)
    wi_rep = _hdot(wi, sel)
    we_rep = _hdot(we, sel)
    dprep_ref[...] = wi_rep
    wrep_ref[...] = _hdot(w, sel)
    denrep_ref[...] = _hdot(den, sel)
    wev_ref[...] = we_rep * v
    nnew_ref[...] = wi_rep * n0 + we_rep * ksc
    q_ref[...] = q
    k_ref[...] = ksc
    v_ref[...] = v
    og_ref[...] = _sigmoid(o_pre + bo_ref[...])
    xc_ref[...] = xc


def _mlstm_dec_state_kernel(c_ref, q_ref, k_ref, dp_ref, wevT_ref, cn_ref, cqT_ref, *, heads, bb):
    i = pl.program_id(0)
    hd = c_ref.shape[2]
    nb = cqT_ref.shape[1]
    lane = lax.broadcasted_iota(jnp.int32, (hd, nb), 1)
    for bi in range(bb):
        msk = lane == i * bb + bi
        for h in range(heads):
            rows = slice(h * hd, (h + 1) * hd)
            cmat = c_ref[bi, h]
            cqcol = jnp.sum(cmat * q_ref[bi, h:h + 1, :], axis=1, keepdims=True)
            wev = jnp.sum(jnp.where(msk, wevT_ref[rows, :], 0.0), axis=1, keepdims=True)
            cn_ref[bi, h] = dp_ref[bi, h:h + 1, :] * cmat + wev * k_ref[bi, h:h + 1, :]
            cqT_ref[rows, :] = jnp.where(msk, cqcol, cqT_ref[rows, :])


def _mlstm_dec_post_kernel(cq_ref, v_ref, dp_ref, w_ref, den_ref, og_ref, xc_ref, z_ref,
                           mhw_ref, skip_ref, g_ref, *, heads):
    inner = cq_ref.shape[1]
    hd = inner // heads
    for h in range(heads):
        cols = slice(h * hd, (h + 1) * hd)
        num = dp_ref[:, cols] * cq_ref[:, cols] + w_ref[:, cols] * v_ref[:, cols]
        hh = og_ref[:, cols] * (num / den_ref[:, cols])
        mu = jnp.mean(hh, axis=1, keepdims=True)
        hc = hh - mu
        hn = hc * lax.rsqrt(jnp.mean(hc * hc, axis=1, keepdims=True) + EPS)
        hn = hn * mhw_ref[:, cols] + skip_ref[:, cols] * xc_ref[:, cols]
        g_ref[:, cols] = hn * _silu(z_ref[:, cols])


def _mlstm_decode(xmz, c0, n0, m0, conv0, inner, cst):
    nb = xmz.shape[0]
    heads = ML_HEADS
    hd = inner // heads
    nco = 2 * ML_BLOCK - 1
    full = lambda shape: pl.BlockSpec(shape, lambda i: tuple(0 for _ in shape))
    m0p = jnp.pad(m0, ((0, 0), (0, LANES - heads)))
    kern = functools.partial(_mlstm_dec_pre_kernel, heads=heads)
    big = jax.ShapeDtypeStruct((nb, inner), F32)
    outs = pl.pallas_call(
        kern,
        name="mlstm_dec_pre",
        grid=(1,),
        in_specs=[pl.BlockSpec((nb, inner), lambda i: (0, 0)),
                  full((nb, (CONV_K - 1) * inner)), full((nb, inner)), full((nb, LANES)),
                  full((CONV_K, inner)), full((1, inner)),
                  full((nco, inner)), full((nco, inner)), full((nco, inner)), full((nco, inner)),
                  full((1, inner)),
                  full((inner, LANES)), full((inner, LANES)), full((inner, LANES)), full((1, LANES))],
        out_specs=[full((nb, (CONV_K - 1) * inner))] + [full((nb, inner))] * 10 + [full((nb, LANES))],
        out_shape=[jax.ShapeDtypeStruct((nb, (CONV_K - 1) * inner), F32)] + [big] * 10
                  + [jax.ShapeDtypeStruct((nb, LANES), F32)],
        compiler_params=_cparams("arbitrary"),
    )(xmz, conv0.reshape(nb, (CONV_K - 1) * inner), n0.reshape(nb, inner), m0p,
      cst["conv_w"], cst["conv_b"], cst["cq"], cst["ck"], cst["cv"], cst["co"], cst["bo"],
      cst["wifq"], cst["wifk"], cst["wifv"], cst["bif"])
    cnew, q, ksc, v, og, xc, wev, dprep, wrep, denrep, nnew, mnew = outs

    bb = 2 if nb % 2 == 0 else 1
    kern = functools.partial(_mlstm_dec_state_kernel, heads=heads, bb=bb)
    h3 = lambda: pl.BlockSpec((bb, heads, hd), lambda i: (i, 0, 0))
    c_new, cq_t = pl.pallas_call(
        kern,
        name="mlstm_dec_state",
        grid=(nb // bb,),
        in_specs=[pl.BlockSpec((bb, heads, hd, hd), lambda i: (i, 0, 0, 0)),
                  h3(), h3(), h3(),
                  pl.BlockSpec((inner, nb), lambda i: (0, 0))],
        out_specs=[pl.BlockSpec((bb, heads, hd, hd), lambda i: (i, 0, 0, 0)),
                   pl.BlockSpec((inner, nb), lambda i: (0, 0))],
        out_shape=[jax.ShapeDtypeStruct((nb, heads, hd, hd), F32),
                   jax.ShapeDtypeStruct((inner, nb), F32)],
        compiler_params=_cparams("arbitrary"),
    )(c0, q.reshape(nb, heads, hd), ksc.reshape(nb, heads, hd), dprep.reshape(nb, heads, hd), wev.T)

    kern = functools.partial(_mlstm_dec_post_kernel, heads=heads)
    g = pl.pallas_call(
        kern,
        name="mlstm_dec_post",
        grid=(1,),
        in_specs=[full((nb, inner))] * 7
                 + [pl.BlockSpec((nb, inner), lambda i: (0, 1)), full((1, inner)), full((1, inner))],
        out_specs=full((nb, inner)),
        out_shape=big,
        compiler_params=_cparams("arbitrary"),
    )(cq_t.T, v, dprep, wrep, denrep, og, xc, xmz, cst["mhw"], cst["skip"])
    return (g, c_new, nnew.reshape(nb, heads, hd), mnew[:, 0:heads],
            cnew.reshape(nb, CONV_K - 1, inner))


def _ssd_in_weights(in_proj, inner, conv_dim):
    ncol = inner + conv_dim
    pad = LANES - (in_proj.shape[1] - ncol)
    return jnp.pad(in_proj.astype(BF16), ((0, 0), (0, pad)))


def _ssd_layer(xp, xs, ssm0, conv0, prm, final_w=None):
    norm_w, in_proj, conv_w, conv_b, dt_bias, a_log, d_skip, gnorm_w, out_proj = prm
    bsz, seq, d = xp.shape
    nb = xs.shape[0]
    inner = out_proj.shape[0]
    heads = a_log.shape[0]
    conv_dim = conv_w.shape[1]
    w_in = _ssd_in_weights(in_proj, inner, conv_dim)
    w_out = out_proj.astype(BF16)
    xp2 = xp.reshape(bsz * seq, d)
    xs2 = xs.reshape(nb, d)

    zx_p = _norm_matmul(xp2, norm_w, w_in)
    out_p, h_p = _ssd_scan(zx_p, xp2, bsz, seq, inner, heads, conv_w, conv_b, dt_bias, a_log, d_skip,
                           gnorm_w, w_out, final_w)
    conv_p = zx_p.reshape(bsz, seq, -1)[:, seq - (CONV_K - 1):, inner:inner + conv_dim]

    zx_s = _norm_matmul(xs2, norm_w, w_in)
    out_s, h_s, conv_s = _ssd_decode(zx_s, xs2, ssm0, conv0, inner, heads, prm, w_out, final_w=final_w)
    return (out_p.reshape(bsz, seq, d), out_s.reshape(nb, 1, d),
            h_p.reshape(bsz, heads, SSD_HEAD_DIM, SSD_STATE), conv_p, h_s, conv_s)


def _gmlp_layer(xp, xs, prm):
    norm_w, in_proj, v_ln_w, v_ln_b, spatial_w, spatial_b, out_proj = prm
    bsz, seq, d = xp.shape
    nb = xs.shape[0]
    inner = out_proj.shape[0]
    w_in = in_proj.astype(BF16)
    w_out = out_proj.astype(BF16)
    xp2 = xp.reshape(bsz * seq, d)
    xs2 = xs.reshape(nb, d)
    out_p = _gmlp_prompt(xp2, norm_w, w_in, inner, v_ln_w, v_ln_b, spatial_w, spatial_b, w_out)
    uvz_s = _norm_matmul(xs2, norm_w, w_in)
    out_s, vn_s = _gmlp_decode(uvz_s, xs2, inner, v_ln_w, v_ln_b, spatial_w, spatial_b, w_out)
    return out_p.reshape(bsz, seq, d), out_s.reshape(nb, 1, d), vn_s.reshape(nb, 1, inner)


def _mlstm_layer(xp, xs, c0, n0, m0, conv0, prm):
    norm_w, in_proj = prm[0], prm[1]
    out_proj = prm[-1]
    bsz, seq, d = xp.shape
    nb = xs.shape[0]
    inner = out_proj.shape[0]
    w_in = in_proj.astype(BF16)
    w_out = out_proj.astype(BF16)
    cst = _mlstm_consts(prm, inner)
    xp2 = xp.reshape(bsz * seq, d)
    xs2 = xs.reshape(nb, d)

    xmz_p = _norm_matmul(xp2, norm_w, w_in)
    out_p, c_p, n_p, m_p = _mlstm_scan(xmz_p, xp2, bsz, seq, inner, cst, w_out)
    conv_p = xmz_p.reshape(bsz, seq, -1)[:, seq - (CONV_K - 1):, 0:inner]

    xmz_s = _norm_matmul(xs2, norm_w, w_in)
    g_s, c_s, n_s, m_s, conv_s = _mlstm_decode(xmz_s, c0, n0, m0, conv0, inner, cst)
    out_s = _matmul_res(g_s, w_out, xs2)
    return (out_p.reshape(bsz, seq, d), out_s.reshape(nb, 1, d),
            c_p, n_p, m_p.reshape(bsz, ML_HEADS), conv_p, c_s, n_s, m_s, conv_s)


def kernel(x_prompt, x_sample, state_l0_ssm, state_l0_conv, state_l2_C, state_l2_n, state_l2_m, state_l2_conv, state_l3_ssm, state_l3_conv, l0_norm_w, l0_in_proj, l0_conv_w, l0_conv_b, l0_dt_bias, l0_A_log, l0_D_skip, l0_gnorm_w, l0_out_proj, l1_norm_w, l1_in_proj, l1_v_ln_w, l1_v_ln_b, l1_spatial_w, l1_spatial_b, l1_out_proj, l2_norm_w, l2_in_proj, l2_conv_w, l2_conv_b, l2_w_q, l2_w_k, l2_w_v, l2_w_o, l2_b_o, l2_w_if, l2_b_if, l2_mh_norm_w, l2_skip, l2_out_proj, l3_norm_w, l3_in_proj, l3_conv_w, l3_conv_b, l3_dt_bias, l3_A_log, l3_D_skip, l3_gnorm_w, l3_out_proj, final_norm_w):
    p0 = (l0_norm_w, l0_in_proj, l0_conv_w, l0_conv_b, l0_dt_bias, l0_A_log, l0_D_skip, l0_gnorm_w, l0_out_proj)
    p1 = (l1_norm_w, l1_in_proj, l1_v_ln_w, l1_v_ln_b, l1_spatial_w, l1_spatial_b, l1_out_proj)
    p2 = (l2_norm_w, l2_in_proj, l2_conv_w, l2_conv_b, l2_w_q, l2_w_k, l2_w_v, l2_w_o, l2_b_o,
          l2_w_if, l2_b_if, l2_mh_norm_w, l2_skip, l2_out_proj)
    p3 = (l3_norm_w, l3_in_proj, l3_conv_w, l3_conv_b, l3_dt_bias, l3_A_log, l3_D_skip, l3_gnorm_w, l3_out_proj)

    hp, hs, p0_ssm, p0_conv, s0_ssm, s0_conv = _ssd_layer(x_prompt, x_sample, state_l0_ssm, state_l0_conv, p0)
    hp, hs, s1_v = _gmlp_layer(hp, hs, p1)
    hp, hs, p2_C, p2_n, p2_m, p2_conv, s2_C, s2_n, s2_m, s2_conv = _mlstm_layer(
        hp, hs, state_l2_C, state_l2_n, state_l2_m, state_l2_conv, p2)
    y_prompt, y_sample, p3_ssm, p3_conv, s3_ssm, s3_conv = _ssd_layer(
        hp, hs, state_l3_ssm, state_l3_conv, p3, final_w=final_norm_w)
    return (y_prompt, y_sample,
            p0_ssm, p0_conv, s0_ssm, s0_conv,
            s1_v,
            p2_C, p2_n, p2_m, p2_conv, s2_C, s2_n, s2_m, s2_conv,
            p3_ssm, p3_conv, s3_ssm, s3_conv)
```

```python
import functools

import jax
import jax.numpy as jnp
from jax import lax
from jax.experimental import pallas as pl
from jax.experimental.pallas import tpu as pltpu

F32 = jnp.float32
BF16 = jnp.bfloat16
EPS = 1e-6
CONV_K = 4
CHUNK = 128
LANES = 128
MXU_TILE = 256
SUBLANES = 8
SSD_HEAD_DIM = 64
SSD_STATE = 128
SSD_GROUPS = 8
OUT_GROUPS = 2
ML_HEADS = 4
ML_BLOCK = 4
GM_GROUPS = 8
VMEM_LIMIT = 56 * 1024 * 1024
HI = lax.Precision.HIGHEST
LOG2E = 1.4426950408889634
NT_DIMS = (((1,), (1,)), ((), ()))


def _cparams(*sem):
    return pltpu.CompilerParams(dimension_semantics=sem, vmem_limit_bytes=VMEM_LIMIT)


def _sigmoid(x):
    return 1.0 / (1.0 + jnp.exp2(x * (-LOG2E)))


def _silu(x):
    h = 0.5 * x
    return h + h * jnp.tanh(h)


def _softplus(x):
    return jnp.maximum(x, 0.0) + jnp.log(1.0 + jnp.exp(-jnp.abs(x)))


def _bdot(a, b):
    return jnp.dot(a.astype(BF16), b.astype(BF16), preferred_element_type=F32)


def _hdot(a, b):
    return jnp.dot(a, b, precision=HI, preferred_element_type=F32)


def _rms(x, w):
    return x * lax.rsqrt(jnp.mean(x * x, axis=-1, keepdims=True) + EPS) * w


def _tril(n):
    r = lax.broadcasted_iota(jnp.int32, (n, n), 0)
    c = lax.broadcasted_iota(jnp.int32, (n, n), 1)
    return r >= c


def _expand_sel(n_in, width, n_out):
    r = lax.broadcasted_iota(jnp.int32, (n_in, n_out), 0)
    c = lax.broadcasted_iota(jnp.int32, (n_in, n_out), 1)
    return (c // width == r).astype(F32)


def _split_bf16(x):
    hi = x.astype(BF16)
    return hi, (x - hi.astype(F32)).astype(BF16)


def _expand(pieces, sel_bf16):
    return (jnp.dot(pieces[0], sel_bf16, preferred_element_type=F32)
            + jnp.dot(pieces[1], sel_bf16, preferred_element_type=F32))


def _norm_matmul_kernel(x_ref, nw_ref, *refs):
    xn = _rms(x_ref[...], nw_ref[...]).astype(BF16)
    nw = len(refs) // 2
    for w_ref, o_ref in zip(refs[:nw], refs[nw:]):
        o_ref[...] = jnp.dot(xn, w_ref[...], preferred_element_type=F32)


def _norm_matmul(x2d, norm_w, *ws_bf16):
    m, k = x2d.shape
    tm = min(m, 512)
    return pl.pallas_call(
        _norm_matmul_kernel,
        name="norm_matmul",
        grid=(m // tm,),
        in_specs=[pl.BlockSpec((tm, k), lambda i: (i, 0)),
                  pl.BlockSpec((1, k), lambda i: (0, 0))]
                 + [pl.BlockSpec(w.shape, lambda i: (0, 0), pipeline_mode=pl.Buffered(1)) for w in ws_bf16],
        out_specs=[pl.BlockSpec((tm, w.shape[1]), lambda i: (i, 0)) for w in ws_bf16],
        out_shape=[jax.ShapeDtypeStruct((m, w.shape[1]), F32) for w in ws_bf16],
        compiler_params=_cparams("parallel"),
    )(x2d, norm_w.reshape(1, k), *ws_bf16)


def _matmul_res_kernel(g_ref, w_ref, x_ref, o_ref):
    o_ref[...] = x_ref[...] + _bdot(g_ref[...], w_ref[...])


def _matmul_res(g2d, w_bf16, x2d):
    m, k = g2d.shape
    n = w_bf16.shape[1]
    tm = min(m, 512)
    return pl.pallas_call(
        _matmul_res_kernel,
        name="matmul_res",
        grid=(m // tm,),
        in_specs=[pl.BlockSpec((tm, k), lambda i: (i, 0)),
                  pl.BlockSpec((k, n), lambda i: (0, 0)),
                  pl.BlockSpec((tm, n), lambda i: (i, 0))],
        out_specs=pl.BlockSpec((tm, n), lambda i: (i, 0)),
        out_shape=jax.ShapeDtypeStruct((m, n), F32),
        compiler_params=_cparams("parallel"),
    )(g2d, w_bf16, x2d)


CONV_ROWS = CHUNK + SUBLANES
CONV_VREGS = CONV_ROWS // SUBLANES
CUR = slice(SUBLANES, CONV_ROWS)


def _conv_tail(buf_ref, is_first):
    lead = (slice(None),) * (len(buf_ref.shape) - 2)

    @pl.when(is_first)
    def _():
        buf_ref[lead + (slice(0, SUBLANES), slice(None))] = jnp.zeros(
            buf_ref.shape[:-2] + (SUBLANES, LANES), F32)

    @pl.when(jnp.logical_not(is_first))
    def _():
        buf_ref[lead + (slice(0, SUBLANES), slice(None))] = buf_ref[lead + (slice(CHUNK, CONV_ROWS), slice(None))]


def _conv_fill(buf_ref, src_ref, col0):
    t0 = col0 // LANES
    for j in range(src_ref.shape[1] // LANES):
        buf_ref[t0 + j, CUR, :] = src_ref[:, j * LANES:(j + 1) * LANES]


def _conv_silu(buf_ref, act_ref, cw_ref, cb_ref):
    nv = CONV_VREGS
    for j in range(buf_ref.shape[0]):
        cols = slice(j * LANES, (j + 1) * LANES)
        v = [buf_ref[j, pl.ds(a, SUBLANES, stride=nv), :] for a in range(nv)]
        wrap = {a: pltpu.roll(v[a], 1, 0) for a in range(nv - (CONV_K - 1), nv)}
        coef = [jnp.broadcast_to(cw_ref[k:k + 1, cols], (SUBLANES, LANES)) for k in range(CONV_K)]
        bias = jnp.broadcast_to(cb_ref[:, cols], (SUBLANES, LANES))
        for a in range(nv):
            acc = bias + coef[CONV_K - 1] * v[a]
            for back in range(1, CONV_K):
                tap = v[a - back] if a >= back else wrap[a - back + nv]
                acc = acc + coef[CONV_K - 1 - back] * tap
            act_ref[j, pl.ds(a, SUBLANES, stride=nv), :] = _silu(acc)


def _act_cols(act_ref, col0, width):
    t0 = col0 // LANES
    tiles = [act_ref[t0 + j, CUR, :] for j in range(width // LANES)]
    return tiles[0] if len(tiles) == 1 else jnp.concatenate(tiles, axis=1)


def _ssd_scan_kernel(*refs, inner, heads, has_final, streams):
    (xs_ref, bc_ref, dt_ref, z_ref, x_ref, cw_ref, cb_ref, dtb_ref, alog_ref, drep_ref, sel_ref,
     gnw_ref, wout_ref) = refs[:13]
    fin_ref = refs[13] if has_final else None
    o_ref, h_ref, buf_ref, act_ref = refs[13 + has_final:]
    c = pl.program_id(1)
    L = CHUNK
    gn = SSD_GROUPS * SSD_STATE
    rep = heads // SSD_GROUPS
    gw = rep * SSD_HEAD_DIM
    S = range(streams)

    @pl.when(c == 0)
    def _():
        h_ref[...] = jnp.zeros(h_ref.shape, F32)

    _conv_tail(buf_ref, c == 0)
    for s in S:
        _conv_fill(buf_ref.at[s], xs_ref.at[s], 0)
        _conv_fill(buf_ref.at[s], bc_ref.at[s], inner)
    for s in S:
        _conv_silu(buf_ref.at[s], act_ref.at[s], cw_ref, cb_ref)

    lane = lax.broadcasted_iota(jnp.int32, (1, LANES), 1)
    a_neg = jnp.where(lane < heads, -jnp.exp(alog_ref[...]), 0.0)
    tri = _tril(L)
    cum2, cum2_t, dt_t, dend, e_hi, e_lo, w_hi, w_lo = [], [], [], [], [], [], [], []
    for s in S:
        dt = _softplus(dt_ref[s] + dtb_ref[...])
        cm = _hdot(tri.astype(F32), dt * a_neg)
        cum2.append(cm * LOG2E)
        cum2_t.append(cum2[s].T)
        dt_t.append(dt.T)
        cum_end = cm[L - 1:L, :]
        dend.append(jnp.exp(cum_end))
        hi, lo = _split_bf16(jnp.exp(cm))
        e_hi.append(hi)
        e_lo.append(lo)
        hi, lo = _split_bf16(jnp.exp(cum_end - cm) * dt)
        w_hi.append(hi)
        w_lo.append(lo)
    e_pieces = (jnp.concatenate(e_hi, axis=0), jnp.concatenate(e_lo, axis=0))
    w_pieces = (jnp.concatenate(w_hi, axis=0), jnp.concatenate(w_lo, axis=0))

    lane_g = lax.broadcasted_iota(jnp.int32, (L, gw), 1) // SSD_HEAD_DIM
    pend = []
    for g in range(SSD_GROUPS):
        rows = slice(g * gw, (g + 1) * gw)
        e_all = _expand(e_pieces, sel_ref[:, rows])
        w_all = _expand(w_pieces, sel_ref[:, rows])
        yn = []
        for s in S:
            srows = slice(s * L, (s + 1) * L)
            aref = act_ref.at[s]
            bg = _act_cols(aref, inner + g * SSD_STATE, SSD_STATE).astype(BF16)
            cg = _act_cols(aref, inner + gn + g * SSD_STATE, SSD_STATE).astype(BF16)
            cb = lax.dot_general(cg, bg, NT_DIMS, preferred_element_type=F32)
            xg = _act_cols(aref, g * gw, gw)
            mixes = []
            for r in range(rep):
                h = g * rep + r
                seg = cum2[s][:, h:h + 1] - cum2_t[s][h:h + 1, :]
                decay = jnp.exp2(jnp.where(tri, seg, -jnp.inf))
                mixes.append((cb * decay * dt_t[s][h:h + 1, :]).astype(BF16))
            mixcat = jnp.concatenate(mixes, axis=1)
            xgb = xg.astype(BF16)
            xblk = jnp.concatenate([jnp.where(lane_g == r, xgb, jnp.zeros_like(xgb)) for r in range(rep)],
                                   axis=0)
            y = jnp.dot(mixcat, xblk, preferred_element_type=F32)
            hg = h_ref[s, 0, rows, :]
            yi = lax.dot_general(cg, hg.astype(BF16), NT_DIMS, preferred_element_type=F32)
            y = y + yi * e_all[srows, :] + drep_ref[:, rows] * xg
            upd = jnp.dot((xg * w_all[srows, :]).T.astype(BF16), bg, preferred_element_type=F32)
            for r in range(rep):
                h = g * rep + r
                hr = slice(r * SSD_HEAD_DIM, (r + 1) * SSD_HEAD_DIM)
                h_ref[s, 0, g * gw + r * SSD_HEAD_DIM:g * gw + (r + 1) * SSD_HEAD_DIM, :] = (
                    dend[s][:, h:h + 1] * hg[hr, :] + upd[hr, :])
            y = y * _silu(z_ref[s, :, rows])
            y = y * lax.rsqrt(jnp.mean(y * y, axis=-1, keepdims=True) + EPS) * gnw_ref[:, rows]
            yn.append(y.astype(BF16))
        pend.append(jnp.concatenate(yn, axis=0))
        if len(pend) == OUT_GROUPS:
            g0 = g + 1 - OUT_GROUPS
            part = jnp.dot(jnp.concatenate(pend, axis=1), wout_ref[g0 * gw:(g + 1) * gw, :],
                           preferred_element_type=F32)
            pend = []
            for s in S:
                srows = slice(s * L, (s + 1) * L)
                if g0 == 0:
                    o_ref[s] = x_ref[s] + part[srows, :]
                else:
                    o_ref[s] += part[srows, :]
    if has_final:
        for s in S:
            o_ref[s] = _rms(o_ref[s], fin_ref[...])


def _ssd_scan(zx, dt, x2d, bsz, seq, inner, heads, conv_w, conv_b, dt_bias, a_log, d_skip, gnorm_w,
              w_out_bf16, final_w):
    nc = seq // CHUNK
    d = x2d.shape[1]
    gn = SSD_GROUPS * SSD_STATE
    conv_dim = inner + 2 * gn
    assert inner // SSD_GROUPS == (heads // SSD_GROUPS) * SSD_HEAD_DIM
    xs_blk = inner // inner
    bc_blk = (2 * inner) // (2 * gn)
    pad = LANES - heads
    dtb = jnp.pad(dt_bias, (0, pad)).reshape(1, LANES)
    alog = jnp.pad(a_log, (0, pad)).reshape(1, LANES)
    drep = jnp.repeat(d_skip, SSD_HEAD_DIM).reshape(1, inner)
    sel = (jnp.arange(inner)[None, :] // SSD_HEAD_DIM == jnp.arange(LANES)[:, None]).astype(BF16)
    has_final = final_w is not None
    streams = 2 if bsz % 2 == 0 else 1
    bh = bsz // streams
    kern = functools.partial(_ssd_scan_kernel, inner=inner, heads=heads, has_final=has_final,
                             streams=streams)
    row = lambda b, c: b * nc + c
    c2 = lambda shape: pl.BlockSpec(shape, lambda b, c: (0, 0))
    zx3 = zx.reshape(streams, bh * seq, zx.shape[1])
    dt3 = dt.reshape(streams, bh * seq, LANES)
    x3 = x2d.reshape(streams, bh * seq, d)
    ins = [zx3, zx3, dt3, zx3, x3, conv_w, conv_b.reshape(1, conv_dim), dtb, alog, drep, sel,
           gnorm_w.reshape(1, inner), w_out_bf16]
    specs = [pl.BlockSpec((streams, CHUNK, inner), lambda b, c: (0, row(b, c), xs_blk)),
             pl.BlockSpec((streams, CHUNK, 2 * gn), lambda b, c: (0, row(b, c), bc_blk)),
             pl.BlockSpec((streams, CHUNK, LANES), lambda b, c: (0, row(b, c), 0)),
             pl.BlockSpec((streams, CHUNK, inner), lambda b, c: (0, row(b, c), 0)),
             pl.BlockSpec((streams, CHUNK, d), lambda b, c: (0, row(b, c), 0)),
             c2((CONV_K, conv_dim)), c2((1, conv_dim)), c2((1, LANES)), c2((1, LANES)),
             c2((1, inner)), c2((LANES, inner)), c2((1, inner)), c2((inner, d))]
    if has_final:
        ins.append(final_w.reshape(1, d))
        specs.append(c2((1, d)))
    out, h = pl.pallas_call(
        kern,
        name="ssd_scan",
        grid=(bh, nc),
        in_specs=specs,
        out_specs=[pl.BlockSpec((streams, CHUNK, d), lambda b, c: (0, row(b, c), 0)),
                   pl.BlockSpec((streams, 1, inner, SSD_STATE), lambda b, c: (0, b, 0, 0))],
        out_shape=[jax.ShapeDtypeStruct((streams, bh * seq, d), F32),
                   jax.ShapeDtypeStruct((streams, bh, inner, SSD_STATE), F32)],
        scratch_shapes=[pltpu.VMEM((streams, conv_dim // LANES, CONV_ROWS, LANES), F32),
                        pltpu.VMEM((streams, conv_dim // LANES, CONV_ROWS, LANES), F32)],
        compiler_params=_cparams("parallel", "arbitrary"),
    )(*ins)
    return out.reshape(bsz * seq, d), h.reshape(bsz, inner, SSD_STATE)


def _ssd_post_kernel(*refs, has_add, has_final):
    y_ref, z_ref, x_ref, gw_ref, w_ref = refs[:5]
    pos = 5
    add_ref = fin_ref = None
    if has_add:
        add_ref = refs[pos]
        pos += 1
    if has_final:
        fin_ref = refs[pos]
        pos += 1
    o_ref = refs[pos]
    inner = y_ref.shape[1]
    gwid = inner // SSD_GROUPS
    parts = []
    for g in range(SSD_GROUPS):
        cols = slice(g * gwid, (g + 1) * gwid)
        y = y_ref[:, cols]
        if has_add:
            y = y + add_ref[:, cols]
        y = y * _silu(z_ref[:, cols])
        y = y * lax.rsqrt(jnp.mean(y * y, axis=-1, keepdims=True) + EPS)
        parts.append((y * gw_ref[:, cols]).astype(BF16))
    out = x_ref[...] + jnp.dot(jnp.concatenate(parts, axis=1), w_ref[...],
                               preferred_element_type=F32)
    if has_final:
        out = _rms(out, fin_ref[...])
    o_ref[...] = out


def _ssd_post(y, zx, x2d, gnorm_w, w_out_bf16, add=None, final_w=None):
    m, inner = y.shape
    d = x2d.shape[1]
    tm = min(m, 512)
    ins = [y, zx, x2d, gnorm_w.reshape(1, inner), w_out_bf16]
    specs = [pl.BlockSpec((tm, inner), lambda i: (i, 0)),
             pl.BlockSpec((tm, inner), lambda i: (i, 0)),
             pl.BlockSpec((tm, d), lambda i: (i, 0)),
             pl.BlockSpec((1, inner), lambda i: (0, 0)),
             pl.BlockSpec((inner, d), lambda i: (0, 0))]
    if add is not None:
        ins.append(add)
        specs.append(pl.BlockSpec((tm, inner), lambda i: (i, 0)))
    if final_w is not None:
        ins.append(final_w.reshape(1, d))
        specs.append(pl.BlockSpec((1, d), lambda i: (0, 0)))
    kern = functools.partial(_ssd_post_kernel, has_add=add is not None, has_final=final_w is not None)
    return pl.pallas_call(
        kern,
        name="ssd_post",
        grid=(m // tm,),
        in_specs=specs,
        out_specs=pl.BlockSpec((tm, d), lambda i: (i, 0)),
        out_shape=jax.ShapeDtypeStruct((m, d), F32),
        compiler_params=_cparams("parallel"),
    )(*ins)


def _ssd_dec_pre_kernel(xbc_ref, dt_ref, c0_ref, cw_ref, cb_ref, dtb_ref, alog_ref, drep_ref,
                        cnew_ref, dtx_ref, da_ref, b_ref, c_ref, dx_ref, *, inner, heads):
    gn = SSD_GROUPS * SSD_STATE
    cd = inner + 2 * gn
    xnew = xbc_ref[...]
    acc = cb_ref[...] + cw_ref[CONV_K - 1:CONV_K, :] * xnew
    for k in range(CONV_K - 1):
        acc = acc + cw_ref[k:k + 1, :] * c0_ref[:, k * cd:(k + 1) * cd]
    for k in range(CONV_K - 2):
        cnew_ref[:, k * cd:(k + 1) * cd] = c0_ref[:, (k + 1) * cd:(k + 2) * cd]
    cnew_ref[:, (CONV_K - 2) * cd:(CONV_K - 1) * cd] = xnew
    act = _silu(acc)
    xs = act[:, 0:inner]
    b_ref[...] = act[:, inner:inner + gn]
    c_ref[...] = act[:, inner + gn:inner + 2 * gn]
    lane = lax.broadcasted_iota(jnp.int32, (1, LANES), 1)
    dt = _softplus(dt_ref[...] + dtb_ref[...])
    a_neg = jnp.where(lane < heads, -jnp.exp(alog_ref[...]), 0.0)
    da_ref[...] = jnp.exp(dt * a_neg)
    dtx_ref[...] = _hdot(dt, _expand_sel(LANES, SSD_HEAD_DIM, inner)) * xs
    dx_ref[...] = drep_ref[...] * xs


def _ssd_dec_state_kernel(da_ref, h_ref, dtxT_ref, b_ref, c_ref, hn_ref, yT_ref, *, bb, rep):
    i = pl.program_id(0)
    gw = rep * SSD_HEAD_DIM
    nb = yT_ref.shape[1]
    lane = lax.broadcasted_iota(jnp.int32, (gw, nb), 1)
    brow = lax.broadcasted_iota(jnp.int32, (nb, SSD_STATE), 0)

    @pl.when(i == 0)
    def _():
        yT_ref[...] = jnp.zeros(yT_ref.shape, F32)

    def body(bi, carry):
        bglob = i * bb + bi
        msk = lane == bglob
        pick = brow == bglob
        bmat = b_ref[bi]
        cmat = c_ref[bi]
        hnews = []
        for g in range(SSD_GROUPS):
            rows = slice(g * gw, (g + 1) * gw)
            dtx = jnp.sum(jnp.where(msk, dtxT_ref[rows, :], 0.0), axis=1, keepdims=True)
            hdec = jnp.concatenate(
                [da_ref[bglob, g * rep + r] * h_ref[bi, g * gw + r * SSD_HEAD_DIM:g * gw + (r + 1) * SSD_HEAD_DIM, :]
                 for r in range(rep)], axis=0)
            hnew = hdec + dtx * bmat[g:g + 1, :]
            hn_ref[bi, rows, :] = hnew
            hnews.append(hnew.astype(BF16))
        for g in range(SSD_GROUPS):
            rows = slice(g * gw, (g + 1) * gw)
            crow = jnp.where(pick, jnp.broadcast_to(cmat[g:g + 1, :], (nb, SSD_STATE)), 0.0).astype(BF16)
            yT_ref[rows, :] += lax.dot_general(hnews[g], crow, NT_DIMS, preferred_element_type=F32)
        return carry

    lax.fori_loop(0, bb, body, 0)


def _ssd_decode(zx, dt, x2d, ssm0, conv0, inner, heads, prm, w_out_bf16, final_w=None):
    norm_w, in_proj, conv_w, conv_b, dt_bias, a_log, d_skip, gnorm_w, out_proj = prm
    nb = zx.shape[0]
    gn = SSD_GROUPS * SSD_STATE
    cd = inner + 2 * gn
    pad = LANES - heads
    dtb = jnp.pad(dt_bias, (0, pad)).reshape(1, LANES)
    alog = jnp.pad(a_log, (0, pad)).reshape(1, LANES)
    drep = jnp.repeat(d_skip, SSD_HEAD_DIM).reshape(1, inner)
    full = lambda shape: pl.BlockSpec(shape, lambda i: tuple(0 for _ in shape))
    kern = functools.partial(_ssd_dec_pre_kernel, inner=inner, heads=heads)
    cnew, dtx, da, bact, cact, dx = pl.pallas_call(
        kern,
        name="ssd_dec_pre",
        grid=(1,),
        in_specs=[full((nb, cd)), full((nb, LANES)), full((nb, (CONV_K - 1) * cd)), full((CONV_K, cd)), full((1, cd)),
                  full((1, LANES)), full((1, LANES)), full((1, inner))],
        out_specs=[full((nb, (CONV_K - 1) * cd)), full((nb, inner)), full((nb, LANES)),
                   full((nb, gn)), full((nb, gn)), full((nb, inner))],
        out_shape=[jax.ShapeDtypeStruct((nb, (CONV_K - 1) * cd), F32),
                   jax.ShapeDtypeStruct((nb, inner), F32), jax.ShapeDtypeStruct((nb, LANES), F32),
                   jax.ShapeDtypeStruct((nb, gn), F32), jax.ShapeDtypeStruct((nb, gn), F32),
                   jax.ShapeDtypeStruct((nb, inner), F32)],
        compiler_params=_cparams("arbitrary"),
    )(zx[:, inner:inner + cd], dt,
      conv0.reshape(nb, (CONV_K - 1) * cd), conv_w, conv_b.reshape(1, cd), dtb, alog, drep)

    bb = 8 if nb % 8 == 0 else 1
    rep = heads // SSD_GROUPS
    kern = functools.partial(_ssd_dec_state_kernel, bb=bb, rep=rep)
    hnew, y_t = pl.pallas_call(
        kern,
        name="ssd_dec_state",
        grid=(nb // bb,),
        in_specs=[pl.BlockSpec(memory_space=pltpu.SMEM),
                  pl.BlockSpec((bb, inner, SSD_STATE), lambda i: (i, 0, 0)),
                  pl.BlockSpec((inner, nb), lambda i: (0, 0)),
                  pl.BlockSpec((bb, SSD_GROUPS, SSD_STATE), lambda i: (i, 0, 0)),
                  pl.BlockSpec((bb, SSD_GROUPS, SSD_STATE), lambda i: (i, 0, 0))],
        out_specs=[pl.BlockSpec((bb, inner, SSD_STATE), lambda i: (i, 0, 0)),
                   pl.BlockSpec((inner, nb), lambda i: (0, 0))],
        out_shape=[jax.ShapeDtypeStruct((nb, inner, SSD_STATE), F32),
                   jax.ShapeDtypeStruct((inner, nb), F32)],
        compiler_params=_cparams("arbitrary"),
    )(da[:, 0:heads], ssm0.reshape(nb, inner, SSD_STATE), dtx.T,
      bact.reshape(nb, SSD_GROUPS, SSD_STATE), cact.reshape(nb, SSD_GROUPS, SSD_STATE))

    out = _ssd_post(y_t.T, zx, x2d, gnorm_w, w_out_bf16, add=dx, final_w=final_w)
    return out, hnew.reshape(nb, heads, SSD_HEAD_DIM, SSD_STATE), cnew.reshape(nb, CONV_K - 1, cd)


def _layernorm(v, w, b):
    mu = jnp.mean(v, axis=-1, keepdims=True)
    vc = v - mu
    return vc * lax.rsqrt(jnp.mean(vc * vc, axis=-1, keepdims=True) + EPS) * w + b


def _gmlp_kernel(x_ref, nw_ref, win_ref, lw_ref, lb_ref, ws_ref, sbT_ref, w_ref, o_ref, *, nck):
    inner = w_ref.shape[0]
    gd = inner // GM_GROUPS
    tri = _tril(CHUNK)
    x = x_ref[...]
    xn = _rms(x, nw_ref[...]).astype(BF16)
    v = jnp.dot(xn, win_ref[:, inner:2 * inner], preferred_element_type=F32)
    vn = _layernorm(v, lw_ref[...], lb_ref[...]).astype(BF16)
    wmask = [jnp.where(tri, ws_ref[g], 0.0).astype(BF16) for g in range(GM_GROUPS)]
    parts = []
    for g in range(GM_GROUPS):
        cols = slice(g * gd, (g + 1) * gd)
        u = jnp.dot(xn, win_ref[:, cols], preferred_element_type=F32)
        z = jnp.dot(xn, win_ref[:, 2 * inner + g * gd:2 * inner + (g + 1) * gd],
                    preferred_element_type=F32)
        mixed = jnp.concatenate(
            [jnp.dot(wmask[g], vn[ck * CHUNK:(ck + 1) * CHUNK, cols], preferred_element_type=F32)
             for ck in range(nck)], axis=0) + jnp.concatenate([sbT_ref[:, g:g + 1]] * nck, axis=0)
        parts.append((u * mixed * _silu(z)).astype(BF16))
    o_ref[...] = x + jnp.dot(jnp.concatenate(parts, axis=1), w_ref[...], preferred_element_type=F32)


def _gmlp_prompt(x2d, norm_w, w_in_bf16, inner, v_ln_w, v_ln_b, spatial_w, spatial_b, w_out_bf16):
    m, d = x2d.shape
    nck = 4 if (m // CHUNK) % 4 == 0 else 1
    tm = nck * CHUNK
    sb_t = jnp.pad(spatial_b.T, ((0, 0), (0, LANES - GM_GROUPS)))
    kern = functools.partial(_gmlp_kernel, nck=nck)
    c2 = lambda shape: pl.BlockSpec(shape, lambda i: (0, 0), pipeline_mode=pl.Buffered(1))
    return pl.pallas_call(
        kern,
        name="gmlp_prompt",
        grid=(m // tm,),
        in_specs=[pl.BlockSpec((tm, d), lambda i: (i, 0)),
                  c2((1, d)), c2((d, 3 * inner)), c2((1, inner)), c2((1, inner)),
                  pl.BlockSpec((GM_GROUPS, CHUNK, CHUNK), lambda i: (0, 0, 0)),
                  c2((CHUNK, LANES)), c2((inner, d))],
        out_specs=pl.BlockSpec((tm, d), lambda i: (i, 0)),
        out_shape=jax.ShapeDtypeStruct((m, d), F32),
        compiler_params=_cparams("parallel"),
    )(x2d, norm_w.reshape(1, d), w_in_bf16, v_ln_w.reshape(1, inner), v_ln_b.reshape(1, inner),
      spatial_w, sb_t, w_out_bf16)


def _gmlp_dec_kernel(u_ref, v_ref, z_ref, x_ref, lw_ref, lb_ref, w00_ref, sb0_ref, w_ref,
                     o_ref, vn_ref):
    vn = _layernorm(v_ref[...], lw_ref[...], lb_ref[...])
    vn_ref[...] = vn
    mixed = w00_ref[...] * vn + sb0_ref[...]
    g = u_ref[...] * mixed * _silu(z_ref[...])
    o_ref[...] = x_ref[...] + _bdot(g, w_ref[...])


def _gmlp_decode(uvz, x2d, inner, v_ln_w, v_ln_b, spatial_w, spatial_b, w_out_bf16):
    nb, d = x2d.shape
    gd = inner // GM_GROUPS
    w00 = jnp.repeat(spatial_w[:, 0, 0], gd).reshape(1, inner)
    sb0 = jnp.repeat(spatial_b[:, 0], gd).reshape(1, inner)
    vec = pl.BlockSpec((1, inner), lambda i: (0, 0))
    return pl.pallas_call(
        _gmlp_dec_kernel,
        name="gmlp_dec",
        grid=(1,),
        in_specs=[pl.BlockSpec((nb, inner), lambda i: (0, 0)),
                  pl.BlockSpec((nb, inner), lambda i: (0, 1)),
                  pl.BlockSpec((nb, inner), lambda i: (0, 2)),
                  pl.BlockSpec((nb, d), lambda i: (0, 0)),
                  vec, vec, vec, vec,
                  pl.BlockSpec((inner, d), lambda i: (0, 0))],
        out_specs=[pl.BlockSpec((nb, d), lambda i: (0, 0)),
                   pl.BlockSpec((nb, inner), lambda i: (0, 0))],
        out_shape=[jax.ShapeDtypeStruct((nb, d), F32), jax.ShapeDtypeStruct((nb, inner), F32)],
        compiler_params=_cparams("arbitrary"),
    )(uvz, uvz, uvz, x2d, v_ln_w.reshape(1, inner), v_ln_b.reshape(1, inner), w00, sb0, w_out_bf16)


def _blockdiag_coefs(w):
    n = w.shape[0]
    rows = []
    for d in range(-(ML_BLOCK - 1), ML_BLOCK):
        cols = []
        for i in range(ML_BLOCK):
            j = i + d
            cols.append(w[:, j, i] if 0 <= j < ML_BLOCK else jnp.zeros((n,), w.dtype))
        rows.append(jnp.stack(cols, axis=1).reshape(n * ML_BLOCK))
    return jnp.stack(rows, axis=0)


def _blockdiag_apply(x, coef_refs):
    width = x.shape[1]
    outs = [None] * len(coef_refs)
    for di, d in enumerate(range(-(ML_BLOCK - 1), ML_BLOCK)):
        xs = x if d == 0 else pltpu.roll(x, (-d) % width, 1)
        for n, cref in enumerate(coef_refs):
            t = xs * cref[di:di + 1, :]
            outs[n] = t if outs[n] is None else outs[n] + t
    return outs


def _mlstm_scan_kernel(xm_ref, z_ref, x_ref, cw_ref, cb_ref, wqk_ref, wvo_ref, bo_ref,
                       wifq_ref, wifk_ref, wifv_ref, bif_ref, mhw_ref, skip_ref, wout_ref,
                       out_ref, cst_ref, nst_ref, mst_ref,
                       buf_ref, act_ref, q_ref, k_ref, v_ref, o_ref, *, heads, streams):
    c = pl.program_id(1)
    L = CHUNK
    inner = xm_ref.shape[2]
    hd = inner // heads
    scale = hd ** -0.5
    S = range(streams)
    srow = [slice(s * L, (s + 1) * L) for s in S]

    @pl.when(c == 0)
    def _():
        cst_ref[...] = jnp.zeros(cst_ref.shape, F32)
        nst_ref[...] = jnp.zeros(nst_ref.shape, F32)
        mst_ref[...] = jnp.zeros(mst_ref.shape, F32)

    _conv_tail(buf_ref, c == 0)
    for s in S:
        _conv_fill(buf_ref.at[s], xm_ref.at[s], 0)
    for s in S:
        _conv_silu(buf_ref.at[s], act_ref.at[s], cw_ref, cb_ref)

    tw = wqk_ref.shape[1]
    gates = bif_ref[...]
    for t in range(inner // tw):
        cols = slice(t * tw, (t + 1) * tw)
        xc_t = jnp.concatenate([_act_cols(act_ref.at[s], t * tw, tw) for s in S], axis=0)
        xm_t = jnp.concatenate([xm_ref[s, :, cols] for s in S], axis=0)
        qk = _bdot(xc_t, wqk_ref[t])
        vo = _bdot(xm_t, wvo_ref[t])
        q_ref[:, cols] = qk[:, 0:tw]
        k_ref[:, cols] = qk[:, tw:2 * tw]
        v_ref[:, cols] = vo[:, 0:tw]
        o_ref[:, cols] = _sigmoid(vo[:, tw:2 * tw] + bo_ref[:, cols])
        gates = (gates + _bdot(qk[:, 0:tw], wifq_ref[cols, :]) + _bdot(qk[:, tw:2 * tw], wifk_ref[cols, :])
                 + _bdot(vo[:, 0:tw], wifv_ref[cols, :]))
    lf = -_softplus(-gates)
    tri = _tril(L)
    trif = tri.astype(F32)
    hcols = [slice(h * hd, (h + 1) * hd) for h in range(heads)]
    pairs = [(s, h) for s in S for h in range(heads)]
    bt = [_hdot(trif, lf[srow[s], :]) for s in S]
    g_t = [gates[srow[s], :].T for s in S]
    b_t = [bt[s].T for s in S]
    qb = {p: q_ref[srow[p[0]], hcols[p[1]]].astype(BF16) for p in pairs}
    kf = {p: k_ref[srow[p[0]], hcols[p[1]]] * scale for p in pairs}
    kb = {p: kf[p].astype(BF16) for p in pairs}
    cmat = {p: cst_ref[p[0], 0, p[1]] for p in pairs}
    s_qk = {p: lax.dot_general(qb[p], kb[p], NT_DIMS, preferred_element_type=F32) for p in pairs}
    cq = {p: lax.dot_general(qb[p], cmat[p].astype(BF16), NT_DIMS, preferred_element_type=F32)
          for p in pairs}
    bcol, igcol, inter, mt, w, wi = {}, {}, {}, {}, {}, {}
    for p in pairs:
        s, h = p
        bcol[p] = bt[s][:, heads + h:heads + h + 1]
        igcol[p] = gates[srow[s], h:h + 1]
        brow = b_t[s][heads + h:heads + h + 1, :]
        igrow = g_t[s][h:h + 1, :]
        d = jnp.where(tri, bcol[p] - brow + igrow, -jnp.inf)
        inter[p] = bcol[p] + mst_ref[s, 0, :, h:h + 1]
        mt[p] = jnp.maximum(inter[p], jnp.max(d, axis=1, keepdims=True))
        w[p] = jnp.exp(d - mt[p]) * s_qk[p]
        wi[p] = jnp.exp(inter[p] - mt[p])
    num = {p: wi[p] * cq[p] + _bdot(w[p], v_ref[srow[p[0]], hcols[p[1]]]) for p in pairs}
    for h in range(heads):
        cols = hcols[h]
        gs = []
        for s in S:
            p = (s, h)
            nrow = nst_ref[s, 0, h:h + 1, :]
            nq = jnp.sum(q_ref[srow[s], cols] * nrow, axis=1, keepdims=True)
            den = wi[p] * nq + jnp.sum(w[p], axis=1, keepdims=True)
            den = jnp.maximum(jnp.abs(den), jnp.exp(-mt[p]))
            hh = o_ref[srow[s], cols] * (num[p] / den)
            mu = jnp.mean(hh, axis=1, keepdims=True)
            hc = hh - mu
            hn = hc * lax.rsqrt(jnp.mean(hc * hc, axis=1, keepdims=True) + EPS)
            hn = hn * mhw_ref[:, cols] + skip_ref[:, cols] * _act_cols(act_ref.at[s], h * hd, hd)
            gs.append((hn * _silu(z_ref[s, :, cols])).astype(BF16))
        part = jnp.dot(jnp.concatenate(gs, axis=0), wout_ref[cols, :], preferred_element_type=F32)
        for s in S:
            if h == 0:
                out_ref[s] = x_ref[s] + part[srow[s], :]
            else:
                out_ref[s] += part[srow[s], :]
    for p in pairs:
        s, h = p
        cols = hcols[h]
        m_new = mt[p][L - 1:L, :]
        we = jnp.exp(bcol[p][L - 1:L, :] - bcol[p] + igcol[p] - m_new)
        dp = jnp.exp(inter[p][L - 1:L, :] - m_new)
        cst_ref[s, 0, h] = dp * cmat[p] + jnp.dot((we * v_ref[srow[s], cols]).T.astype(BF16), kb[p],
                                                  preferred_element_type=F32)
        nst_ref[s, 0, h:h + 1, :] = dp * nst_ref[s, 0, h:h + 1, :] + jnp.sum(we * kf[p], axis=0, keepdims=True)
        mst_ref[s, 0, :, h:h + 1] = m_new


def _blockdiag_tiles(w, tile):
    nt = w.shape[0] * ML_BLOCK // tile
    rows = jnp.tile(w.reshape(nt, tile, ML_BLOCK), (1, 1, tile // ML_BLOCK))
    blk = jnp.arange(tile) // ML_BLOCK
    return jnp.where(blk[:, None] == blk[None, :], rows, 0.0)


def _mlstm_consts(prm, inner):
    (norm_w, in_proj, conv_w, conv_b, w_q, w_k, w_v, w_o, b_o, w_if, b_if, mh_norm_w, skip,
     out_proj) = prm
    heads = ML_HEADS
    padn = LANES - 2 * heads
    wif = jnp.pad(w_if, ((0, 0), (0, padn))).astype(BF16)
    bif = jnp.pad(b_if, (0, padn)).reshape(1, LANES)
    return dict(
        conv_w=conv_w, conv_b=conv_b.reshape(1, inner),
        cq=_blockdiag_coefs(w_q), ck=_blockdiag_coefs(w_k), cv=_blockdiag_coefs(w_v),
        co=_blockdiag_coefs(w_o), bo=b_o.reshape(1, inner),
        wqk=jnp.concatenate([_blockdiag_tiles(w_q, MXU_TILE), _blockdiag_tiles(w_k, MXU_TILE)],
                            axis=2).astype(BF16),
        wvo=jnp.concatenate([_blockdiag_tiles(w_v, MXU_TILE), _blockdiag_tiles(w_o, MXU_TILE)],
                            axis=2).astype(BF16),
        wifq=wif[0:inner], wifk=wif[inner:2 * inner], wifv=wif[2 * inner:3 * inner], bif=bif,
        mhw=mh_norm_w.reshape(1, inner), skip=skip.reshape(1, inner))


def _mlstm_scan(xmz, x2d, bsz, seq, inner, cst, w_out_bf16):
    nc = seq // CHUNK
    d = x2d.shape[1]
    heads = ML_HEADS
    hd = inner // heads
    ntile = inner // MXU_TILE
    streams = 2 if bsz % 2 == 0 else 1
    bh = bsz // streams
    row = lambda b, c: b * nc + c
    once = pl.Buffered(1)
    c2 = lambda shape: pl.BlockSpec(shape, lambda b, c: (0, 0), pipeline_mode=once)
    c3 = lambda shape: pl.BlockSpec(shape, lambda b, c: (0, 0, 0), pipeline_mode=once)
    kern = functools.partial(_mlstm_scan_kernel, heads=heads, streams=streams)
    xmz3 = xmz.reshape(streams, bh * seq, xmz.shape[1])
    x3 = x2d.reshape(streams, bh * seq, d)
    out, c_p, n_p, m_p = pl.pallas_call(
        kern,
        name="mlstm_scan",
        grid=(bh, nc),
        in_specs=[pl.BlockSpec((streams, CHUNK, inner), lambda b, c: (0, row(b, c), 0)),
                  pl.BlockSpec((streams, CHUNK, inner), lambda b, c: (0, row(b, c), 1)),
                  pl.BlockSpec((streams, CHUNK, d), lambda b, c: (0, row(b, c), 0)),
                  c2((CONV_K, inner)), c2((1, inner)),
                  c3((ntile, MXU_TILE, 2 * MXU_TILE)), c3((ntile, MXU_TILE, 2 * MXU_TILE)),
                  c2((1, inner)),
                  c2((inner, LANES)), c2((inner, LANES)), c2((inner, LANES)), c2((1, LANES)),
                  c2((1, inner)), c2((1, inner)), c2((inner, d))],
        out_specs=[pl.BlockSpec((streams, CHUNK, d), lambda b, c: (0, row(b, c), 0)),
                   pl.BlockSpec((streams, 1, heads, hd, hd), lambda b, c: (0, b, 0, 0, 0)),
                   pl.BlockSpec((streams, 1, heads, hd), lambda b, c: (0, b, 0, 0)),
                   pl.BlockSpec((streams, 1, 1, heads), lambda b, c: (0, b, 0, 0))],
        out_shape=[jax.ShapeDtypeStruct((streams, bh * seq, d), F32),
                   jax.ShapeDtypeStruct((streams, bh, heads, hd, hd), F32),
                   jax.ShapeDtypeStruct((streams, bh, heads, hd), F32),
                   jax.ShapeDtypeStruct((streams, bh, 1, heads), F32)],
        scratch_shapes=[pltpu.VMEM((streams, inner // LANES, CONV_ROWS, LANES), F32)] * 2
                       + [pltpu.VMEM((streams * CHUNK, inner), F32)] * 4,
        compiler_params=_cparams("parallel", "arbitrary"),
    )(xmz3, xmz3, x3, cst["conv_w"], cst["conv_b"], cst["wqk"], cst["wvo"], cst["bo"],
      cst["wifq"], cst["wifk"], cst["wifv"], cst["bif"], cst["mhw"], cst["skip"], w_out_bf16)
    return (out.reshape(bsz * seq, d), c_p.reshape(bsz, heads, hd, hd), n_p.reshape(bsz, heads, hd),
            m_p.reshape(bsz, 1, heads))


def _mlstm_dec_pre_kernel(xm_ref, c0_ref, n0_ref, m0_ref, cw_ref, cb_ref, cq_ref, ck_ref, cv_ref,
                          co_ref, bo_ref, wifq_ref, wifk_ref, wifv_ref, bif_ref,
                          cnew_ref, q_ref, k_ref, v_ref, og_ref, xc_ref, wev_ref, dprep_ref,
                          wrep_ref, denrep_ref, nnew_ref, mnew_ref, *, heads):
    inner = xm_ref.shape[1]
    nb = xm_ref.shape[0]
    hd = inner // heads
    xm = xm_ref[...]
    acc = cb_ref[...] + cw_ref[CONV_K - 1:CONV_K, :] * xm
    for kk in range(CONV_K - 1):
        acc = acc + cw_ref[kk:kk + 1, :] * c0_ref[:, kk * inner:(kk + 1) * inner]
    for kk in range(CONV_K - 2):
        cnew_ref[:, kk * inner:(kk + 1) * inner] = c0_ref[:, (kk + 1) * inner:(kk + 2) * inner]
    cnew_ref[:, (CONV_K - 2) * inner:(CONV_K - 1) * inner] = xm
    xc = _silu(acc)
    q, k = _blockdiag_apply(xc, [cq_ref, ck_ref])
    v, o_pre = _blockdiag_apply(xm, [cv_ref, co_ref])
    gates = _bdot(q, wifq_ref[...]) + _bdot(k, wifk_ref[...]) + _bdot(v, wifv_ref[...]) + bif_ref[...]
    ksc = k * (hd ** -0.5)
    lane = lax.broadcasted_iota(jnp.int32, (nb, LANES), 1)
    ig = gates
    lf = pltpu.roll(-_softplus(-gates), LANES - heads, 1)
    n0 = n0_ref[...]
    qk = jnp.zeros((nb, LANES), F32)
    nq = jnp.zeros((nb, LANES), F32)
    for h in range(heads):
        cols = slice(h * hd, (h + 1) * hd)
        qk = jnp.where(lane == h, jnp.sum(q[:, cols] * ksc[:, cols], axis=1, keepdims=True), qk)
        nq = jnp.where(lane == h, jnp.sum(q[:, cols] * n0[:, cols], axis=1, keepdims=True), nq)
    inter = lf + m0_ref[...]
    mt = jnp.maximum(inter, ig)
    wi = jnp.exp(inter - mt)
    we = jnp.exp(ig - mt)
    w = we * qk
    den = jnp.maximum(jnp.abs(wi * nq + w), jnp.exp(-mt))
    mnew_ref[...] = mt
    sel = _expand_sel(LANES, hd, inner)
    wi_rep = _hdot(wi, sel)
    we_rep = _hdot(we, sel)
    dprep_ref[...] = wi_rep
    wrep_ref[...] = _hdot(w, sel)
    denrep_ref[...] = _hdot(den, sel)
    wev_ref[...] = we_rep * v
    nnew_ref[...] = wi_rep * n0 + we_rep * ksc
    q_ref[...] = q
    k_ref[...] = ksc
    v_ref[...] = v
    og_ref[...] = _sigmoid(o_pre + bo_ref[...])
    xc_ref[...] = xc


def _mlstm_dec_state_kernel(c_ref, q_ref, k_ref, dp_ref, wevT_ref, cn_ref, cqT_ref, *, heads, bb):
    i = pl.program_id(0)
    hd = c_ref.shape[2]
    nb = cqT_ref.shape[1]
    lane = lax.broadcasted_iota(jnp.int32, (hd, nb), 1)
    for bi in range(bb):
        msk = lane == i * bb + bi
        for h in range(heads):
            rows = slice(h * hd, (h + 1) * hd)
            cmat = c_ref[bi, h]
            cqcol = jnp.sum(cmat * q_ref[bi, h:h + 1, :], axis=1, keepdims=True)
            wev = jnp.sum(jnp.where(msk, wevT_ref[rows, :], 0.0), axis=1, keepdims=True)
            cn_ref[bi, h] = dp_ref[bi, h:h + 1, :] * cmat + wev * k_ref[bi, h:h + 1, :]
            cqT_ref[rows, :] = jnp.where(msk, cqcol, cqT_ref[rows, :])


def _mlstm_dec_post_kernel(cq_ref, v_ref, dp_ref, w_ref, den_ref, og_ref, xc_ref, z_ref,
                           mhw_ref, skip_ref, g_ref, *, heads):
    inner = cq_ref.shape[1]
    hd = inner // heads
    for h in range(heads):
        cols = slice(h * hd, (h + 1) * hd)
        num = dp_ref[:, cols] * cq_ref[:, cols] + w_ref[:, cols] * v_ref[:, cols]
        hh = og_ref[:, cols] * (num / den_ref[:, cols])
        mu = jnp.mean(hh, axis=1, keepdims=True)
        hc = hh - mu
        hn = hc * lax.rsqrt(jnp.mean(hc * hc, axis=1, keepdims=True) + EPS)
        hn = hn * mhw_ref[:, cols] + skip_ref[:, cols] * xc_ref[:, cols]
        g_ref[:, cols] = hn * _silu(z_ref[:, cols])


def _mlstm_decode(xmz, c0, n0, m0, conv0, inner, cst):
    nb = xmz.shape[0]
    heads = ML_HEADS
    hd = inner // heads
    nco = 2 * ML_BLOCK - 1
    full = lambda shape: pl.BlockSpec(shape, lambda i: tuple(0 for _ in shape))
    m0p = jnp.pad(m0, ((0, 0), (0, LANES - heads)))
    kern = functools.partial(_mlstm_dec_pre_kernel, heads=heads)
    big = jax.ShapeDtypeStruct((nb, inner), F32)
    outs = pl.pallas_call(
        kern,
        name="mlstm_dec_pre",
        grid=(1,),
        in_specs=[pl.BlockSpec((nb, inner), lambda i: (0, 0)),
                  full((nb, (CONV_K - 1) * inner)), full((nb, inner)), full((nb, LANES)),
                  full((CONV_K, inner)), full((1, inner)),
                  full((nco, inner)), full((nco, inner)), full((nco, inner)), full((nco, inner)),
                  full((1, inner)),
                  full((inner, LANES)), full((inner, LANES)), full((inner, LANES)), full((1, LANES))],
        out_specs=[full((nb, (CONV_K - 1) * inner))] + [full((nb, inner))] * 10 + [full((nb, LANES))],
        out_shape=[jax.ShapeDtypeStruct((nb, (CONV_K - 1) * inner), F32)] + [big] * 10
                  + [jax.ShapeDtypeStruct((nb, LANES), F32)],
        compiler_params=_cparams("arbitrary"),
    )(xmz, conv0.reshape(nb, (CONV_K - 1) * inner), n0.reshape(nb, inner), m0p,
      cst["conv_w"], cst["conv_b"], cst["cq"], cst["ck"], cst["cv"], cst["co"], cst["bo"],
      cst["wifq"], cst["wifk"], cst["wifv"], cst["bif"])
    cnew, q, ksc, v, og, xc, wev, dprep, wrep, denrep, nnew, mnew = outs

    bb = 2 if nb % 2 == 0 else 1
    kern = functools.partial(_mlstm_dec_state_kernel, heads=heads, bb=bb)
    h3 = lambda: pl.BlockSpec((bb, heads, hd), lambda i: (i, 0, 0))
    c_new, cq_t = pl.pallas_call(
        kern,
        name="mlstm_dec_state",
        grid=(nb // bb,),
        in_specs=[pl.BlockSpec((bb, heads, hd, hd), lambda i: (i, 0, 0, 0)),
                  h3(), h3(), h3(),
                  pl.BlockSpec((inner, nb), lambda i: (0, 0))],
        out_specs=[pl.BlockSpec((bb, heads, hd, hd), lambda i: (i, 0, 0, 0)),
                   pl.BlockSpec((inner, nb), lambda i: (0, 0))],
        out_shape=[jax.ShapeDtypeStruct((nb, heads, hd, hd), F32),
                   jax.ShapeDtypeStruct((inner, nb), F32)],
        compiler_params=_cparams("arbitrary"),
    )(c0, q.reshape(nb, heads, hd), ksc.reshape(nb, heads, hd), dprep.reshape(nb, heads, hd), wev.T)

    kern = functools.partial(_mlstm_dec_post_kernel, heads=heads)
    g = pl.pallas_call(
        kern,
        name="mlstm_dec_post",
        grid=(1,),
        in_specs=[full((nb, inner))] * 7
                 + [pl.BlockSpec((nb, inner), lambda i: (0, 1)), full((1, inner)), full((1, inner))],
        out_specs=full((nb, inner)),
        out_shape=big,
        compiler_params=_cparams("arbitrary"),
    )(cq_t.T, v, dprep, wrep, denrep, og, xc, xmz, cst["mhw"], cst["skip"])
    return (g, c_new, nnew.reshape(nb, heads, hd), mnew[:, 0:heads],
            cnew.reshape(nb, CONV_K - 1, inner))


def _ssd_in_weights(in_proj, inner, conv_dim):
    ncol = inner + conv_dim
    pad = LANES - (in_proj.shape[1] - ncol)
    return in_proj[:, 0:ncol].astype(BF16), jnp.pad(in_proj[:, ncol:], ((0, 0), (0, pad))).astype(BF16)


def _ssd_layer(xp, xs, ssm0, conv0, prm, final_w=None):
    norm_w, in_proj, conv_w, conv_b, dt_bias, a_log, d_skip, gnorm_w, out_proj = prm
    bsz, seq, d = xp.shape
    nb = xs.shape[0]
    inner = out_proj.shape[0]
    heads = a_log.shape[0]
    conv_dim = conv_w.shape[1]
    w_in, w_dt = _ssd_in_weights(in_proj, inner, conv_dim)
    w_out = out_proj.astype(BF16)
    xp2 = xp.reshape(bsz * seq, d)
    xs2 = xs.reshape(nb, d)

    zx_p, dt_p = _norm_matmul(xp2, norm_w, w_in, w_dt)
    out_p, h_p = _ssd_scan(zx_p, dt_p, xp2, bsz, seq, inner, heads, conv_w, conv_b, dt_bias, a_log, d_skip,
                           gnorm_w, w_out, final_w)
    conv_p = zx_p.reshape(bsz, seq, -1)[:, seq - (CONV_K - 1):, inner:inner + conv_dim]

    zx_s, dt_s = _norm_matmul(xs2, norm_w, w_in, w_dt)
    out_s, h_s, conv_s = _ssd_decode(zx_s, dt_s, xs2, ssm0, conv0, inner, heads, prm, w_out, final_w=final_w)
    return (out_p.reshape(bsz, seq, d), out_s.reshape(nb, 1, d),
            h_p.reshape(bsz, heads, SSD_HEAD_DIM, SSD_STATE), conv_p, h_s, conv_s)


def _gmlp_layer(xp, xs, prm):
    norm_w, in_proj, v_ln_w, v_ln_b, spatial_w, spatial_b, out_proj = prm
    bsz, seq, d = xp.shape
    nb = xs.shape[0]
    inner = out_proj.shape[0]
    w_in = in_proj.astype(BF16)
    w_out = out_proj.astype(BF16)
    xp2 = xp.reshape(bsz * seq, d)
    xs2 = xs.reshape(nb, d)
    out_p = _gmlp_prompt(xp2, norm_w, w_in, inner, v_ln_w, v_ln_b, spatial_w, spatial_b, w_out)
    uvz_s, = _norm_matmul(xs2, norm_w, w_in)
    out_s, vn_s = _gmlp_decode(uvz_s, xs2, inner, v_ln_w, v_ln_b, spatial_w, spatial_b, w_out)
    return out_p.reshape(bsz, seq, d), out_s.reshape(nb, 1, d), vn_s.reshape(nb, 1, inner)


def _mlstm_layer(xp, xs, c0, n0, m0, conv0, prm):
    norm_w, in_proj = prm[0], prm[1]
    out_proj = prm[-1]
    bsz, seq, d = xp.shape
    nb = xs.shape[0]
    inner = out_proj.shape[0]
    w_in = in_proj.astype(BF16)
    w_out = out_proj.astype(BF16)
    cst = _mlstm_consts(prm, inner)
    xp2 = xp.reshape(bsz * seq, d)
    xs2 = xs.reshape(nb, d)

    xmz_p, = _norm_matmul(xp2, norm_w, w_in)
    out_p, c_p, n_p, m_p = _mlstm_scan(xmz_p, xp2, bsz, seq, inner, cst, w_out)
    conv_p = xmz_p.reshape(bsz, seq, -1)[:, seq - (CONV_K - 1):, 0:inner]

    xmz_s, = _norm_matmul(xs2, norm_w, w_in)
    g_s, c_s, n_s, m_s, conv_s = _mlstm_decode(xmz_s, c0, n0, m0, conv0, inner, cst)
    out_s = _matmul_res(g_s, w_out, xs2)
    return (out_p.reshape(bsz, seq, d), out_s.reshape(nb, 1, d),
            c_p, n_p, m_p.reshape(bsz, ML_HEADS), conv_p, c_s, n_s, m_s, conv_s)


def kernel(x_prompt, x_sample, state_l0_ssm, state_l0_conv, state_l2_C, state_l2_n, state_l2_m, state_l2_conv, state_l3_ssm, state_l3_conv, l0_norm_w, l0_in_proj, l0_conv_w, l0_conv_b, l0_dt_bias, l0_A_log, l0_D_skip, l0_gnorm_w, l0_out_proj, l1_norm_w, l1_in_proj, l1_v_ln_w, l1_v_ln_b, l1_spatial_w, l1_spatial_b, l1_out_proj, l2_norm_w, l2_in_proj, l2_conv_w, l2_conv_b, l2_w_q, l2_w_k, l2_w_v, l2_w_o, l2_b_o, l2_w_if, l2_b_if, l2_mh_norm_w, l2_skip, l2_out_proj, l3_norm_w, l3_in_proj, l3_conv_w, l3_conv_b, l3_dt_bias, l3_A_log, l3_D_skip, l3_gnorm_w, l3_out_proj, final_norm_w):
    p0 = (l0_norm_w, l0_in_proj, l0_conv_w, l0_conv_b, l0_dt_bias, l0_A_log, l0_D_skip, l0_gnorm_w, l0_out_proj)
    p1 = (l1_norm_w, l1_in_proj, l1_v_ln_w, l1_v_ln_b, l1_spatial_w, l1_spatial_b, l1_out_proj)
    p2 = (l2_norm_w, l2_in_proj, l2_conv_w, l2_conv_b, l2_w_q, l2_w_k, l2_w_v, l2_w_o, l2_b_o,
          l2_w_if, l2_b_if, l2_mh_norm_w, l2_skip, l2_out_proj)
    p3 = (l3_norm_w, l3_in_proj, l3_conv_w, l3_conv_b, l3_dt_bias, l3_A_log, l3_D_skip, l3_gnorm_w, l3_out_proj)

    hp, hs, p0_ssm, p0_conv, s0_ssm, s0_conv = _ssd_layer(x_prompt, x_sample, state_l0_ssm, state_l0_conv, p0)
    hp, hs, s1_v = _gmlp_layer(hp, hs, p1)
    hp, hs, p2_C, p2_n, p2_m, p2_conv, s2_C, s2_n, s2_m, s2_conv = _mlstm_layer(
        hp, hs, state_l2_C, state_l2_n, state_l2_m, state_l2_conv, p2)
    y_prompt, y_sample, p3_ssm, p3_conv, s3_ssm, s3_conv = _ssd_layer(
        hp, hs, state_l3_ssm, state_l3_conv, p3, final_w=final_norm_w)
    return (y_prompt, y_sample,
            p0_ssm, p0_conv, s0_ssm, s0_conv,
            s1_v,
            p2_C, p2_n, p2_m, p2_conv, s2_C, s2_n, s2_m, s2_conv,
            p3_ssm, p3_conv, s3_ssm, s3_conv)
```
